```python
import math
import jax, jax.numpy as jnp
from jax import lax
import numpy as np

D_MODEL = 2048
BATCH = 32
SEQ = 256
DEPTH = 2
DEC_BATCH = 4
DEC_SEQ = 2048
PAST_LEN = 256

GRID_W = 64
N_EVEN = (DEPTH + 1) // 2
N_ODD = DEPTH // 2
HY_DIM = 1024
HY_ORDER = 2
HY_SHORT = 3
HY_EMB = 33
HY_FILTER_HIDDEN = 64
HY_DECAY_TARGET = 1e-2
HY_FAST_DECAY = 0.3
HY_SLOW_DECAY = 1.5
NA_HEADS = 16
NA_HEAD_DIM = 64
NA_DIM = NA_HEADS * NA_HEAD_DIM
WIN_R = 8
WIN_C = 16
CTX_Q_BLOCK = 128
AB_IN = 3 * HY_DIM + 3 * NA_DIM
AB_OUT = HY_DIM + NA_DIM
POOL_DIM = D_MODEL
POOL_WINDOWS = (2, 4, 8, 16)
POOL_GROUP = POOL_DIM // len(POOL_WINDOWS)
N_EXPERTS = 32
TOP_K = 4
D_FF = D_MODEL
SWIGLU_LIMIT = 7.0
SWIGLU_ALPHA = 1.702
MOE_BLOCK = 128
RMS_EPS = 1e-6

kernel_name = 'hybrid_hyena_natten_pool_moe_diffusion_step'


def _rms_norm(x, g):
    xf = x.astype(jnp.float32)
    y = xf * lax.rsqrt(jnp.mean(xf * xf, axis=-1, keepdims=True) + RMS_EPS)
    return (y * g.astype(jnp.float32)).astype(x.dtype)


def _adaln(cvec, w, b):
    return jnp.split(jax.nn.silu(cvec) @ w + b, 6, axis=-1)


def _modulate(x, g, shift, scale):
    return _rms_norm(x, g) * (1 + scale) + shift


def _hyena_filter_fft(L, w1, b1, w2, b2, w3, freq):
    f32 = jnp.float32
    t = jnp.linspace(0.0, 1.0, L, dtype=f32)[:, None]
    bands = (HY_EMB - 1) // 2
    ang = 2.0 * math.pi * jnp.arange(L, dtype=f32)[:, None] / L
    fb = jnp.linspace(1e-4, bands - 1, bands, dtype=f32)[None, :]
    z = jnp.concatenate([t, jnp.cos(fb * ang), -jnp.sin(fb * ang)], axis=-1)
    fr = freq.astype(f32)
    h = jnp.sin(fr * (z @ w1.astype(f32) + b1.astype(f32)))
    h = jnp.sin(fr * (h @ w2.astype(f32) + b2.astype(f32)))
    h = (h @ w3.astype(f32)).reshape(L, 2, HY_ORDER, HY_DIM)
    deltas = jnp.linspace(math.log(HY_DECAY_TARGET) / HY_SLOW_DECAY,
                          math.log(HY_DECAY_TARGET) / HY_FAST_DECAY, HY_DIM, dtype=f32)
    decay = jnp.exp(-t * jnp.abs(deltas))
    h = h * decay[:, None, None, :]
    h = h * lax.rsqrt(jnp.sum(h * h, axis=(0, 1), keepdims=True))
    k = jnp.concatenate([h[:, 0], jnp.zeros((1, HY_ORDER, HY_DIM), f32), h[:0:-1, 1]], axis=0)
    return jnp.fft.rfft(k, axis=0)


def _short_conv(z, w, b):
    zp = jnp.pad(z, ((0, 0), (1, 1), (0, 0)))
    return w[0] * zp[:, :-2] + w[1] * zp[:, 1:-1] + w[2] * zp[:, 2:] + b


def _hyena(zp, kf, short_w, short_b, bias):
    L = zp.shape[1]
    x1, x2, v = jnp.split(_short_conv(zp, short_w, short_b), 3, axis=-1)
    y = v
    for o, gate in enumerate((x1, x2)):
        yf = y.astype(jnp.float32)
        conv = jnp.fft.irfft(jnp.fft.rfft(yf, n=2 * L, axis=1) * kf[None, :, o], n=2 * L, axis=1)[:, :L]
        y = gate * (conv + yf * bias[o].astype(jnp.float32)).astype(gate.dtype)
    return y


def _split_qkv(p):
    B, L = p.shape[:2]
    qkv = p[..., 3 * HY_DIM:].reshape(B, L, 3, NA_HEADS, NA_HEAD_DIM)
    return qkv[:, :, 0], qkv[:, :, 1], qkv[:, :, 2]


def _ctx_attention(q, k, v):
    B, S, H, hd = q.shape
    nblk = S // CTX_Q_BLOCK
    qb = q.reshape(B, nblk, CTX_Q_BLOCK, H, hd).swapaxes(0, 1)
    scale = hd ** -0.5

    def block(qi):
        s = jnp.einsum('bqhd,bshd->bhqs', qi, k).astype(jnp.float32) * scale
        p = jax.nn.softmax(s, axis=-1).astype(v.dtype)
        return jnp.einsum('bhqs,bshd->bqhd', p, v)

    o = lax.map(block, qb)
    return o.swapaxes(0, 1).reshape(B, S, H * hd)


def _na_latent(q, k, v, k_ctx, v_ctx, rpb):
    B, T, H, hd = q.shape
    rows = T // GRID_W
    kr = min(WIN_R, rows)
    ncb = GRID_W // WIN_C
    reg = 2 * WIN_C
    scale = hd ** -0.5
    cs = np.clip(np.arange(GRID_W) - WIN_C // 2, 0, GRID_W - WIN_C).reshape(ncb, WIN_C)
    reg_start = np.clip(np.arange(ncb) * WIN_C - WIN_C // 2, 0, GRID_W - reg)
    key_col = reg_start[:, None] + np.arange(reg)
    q_col = np.arange(GRID_W).reshape(ncb, WIN_C)
    col_valid = (key_col[:, None, :] >= cs[:, :, None]) & (key_col[:, None, :] < cs[:, :, None] + WIN_C)
    col_off = np.clip(key_col[:, None, :] - q_col[:, :, None], -(WIN_C - 1), WIN_C - 1) + WIN_C - 1
    rpb_col = rpb.astype(jnp.float32)[:, :, col_off]
    mask = jnp.asarray(col_valid)[:, :, None, :]
    q_rows = q.reshape(B, rows, ncb, WIN_C, H, hd).transpose(1, 0, 2, 3, 4, 5)
    n_loc = kr * reg

    def row_block(args):
        r, q_r = args
        rs = jnp.clip(r - kr // 2, 0, rows - kr)
        key_rows = rs + jnp.arange(kr)
        idx = (key_rows[None, :, None] * GRID_W + key_col[:, None, :]).reshape(-1)
        k_b = jnp.take(k, idx, axis=1).reshape(B, ncb, kr, reg, H, hd)
        v_b = jnp.take(v, idx, axis=1).reshape(B, ncb, kr, reg, H, hd)
        s_loc = jnp.einsum('bnqhd,bnijhd->bhnqij', q_r, k_b).astype(jnp.float32) * scale
        bias = jnp.take(rpb_col, key_rows - r + WIN_R - 1, axis=1).transpose(0, 2, 3, 1, 4)
        s_loc = jnp.where(mask, s_loc + bias[None], -1e30)
        s_ctx = jnp.einsum('bnqhd,bshd->bhnqs', q_r, k_ctx).astype(jnp.float32) * scale
        p = jax.nn.softmax(jnp.concatenate([s_loc.reshape(B, H, ncb, WIN_C, n_loc), s_ctx], axis=-1), axis=-1)
        p = p.astype(v.dtype)
        p_loc = p[..., :n_loc].reshape(B, H, ncb, WIN_C, kr, reg)
        return (jnp.einsum('bhnqij,bnijhd->bnqhd', p_loc, v_b)
                + jnp.einsum('bhnqs,bshd->bnqhd', p[..., n_loc:], v_ctx))

    o = lax.map(row_block, (jnp.arange(rows), q_rows))
    return o.transpose(1, 0, 2, 3, 4, 5).reshape(B, T, H * hd)


def _pool_mixer(h, in_w, group_w, scale, out_w):
    u = h @ in_w
    B, L, _ = u.shape
    uf = u.astype(jnp.float32)
    csum = jnp.concatenate([jnp.zeros((B, 1, POOL_DIM), jnp.float32), jnp.cumsum(uf, axis=1)], axis=1)
    pos = np.arange(L)
    outs = []
    for g, w in enumerate(POOL_WINDOWS):
        lo = np.clip(pos - w // 2, 0, L)
        hi = np.clip(pos + w // 2, 0, L)
        cnt = (hi - lo).astype(np.float32)[:, None]
        sl = slice(g * POOL_GROUP, (g + 1) * POOL_GROUP)
        cg = csum[..., sl]
        d = (cg[:, hi] - cg[:, lo]) / cnt - uf[..., sl]
        outs.append(d.astype(h.dtype) @ group_w[g])
    return (jnp.concatenate(outs, axis=-1) * scale) @ out_w


def _moe(x, router_w, router_b, w_gu, b_gu, w_down, b_down):
    T, D = x.shape
    logits = (x @ router_w + router_b).astype(jnp.float32)
    top_vals, top_idx = lax.top_k(logits, TOP_K)
    gates = jax.nn.softmax(top_vals, axis=-1)
    A = T * TOP_K
    e_flat = top_idx.reshape(A)
    tok_flat = jnp.repeat(jnp.arange(T, dtype=jnp.int32), TOP_K)
    g_flat = gates.reshape(A)
    order = jnp.argsort(e_flat)
    e_sorted = e_flat[order]
    counts = jnp.bincount(e_flat, length=N_EXPERTS)
    padded = (counts + MOE_BLOCK - 1) // MOE_BLOCK * MOE_BLOCK
    pad_end = jnp.cumsum(padded)
    pad_start = pad_end - padded
    cnt_start = jnp.cumsum(counts) - counts
    dest = pad_start[e_sorted] + jnp.arange(A, dtype=jnp.int32) - cnt_start[e_sorted]
    n_blocks = -(-(A + N_EXPERTS * (MOE_BLOCK - 1)) // MOE_BLOCK)
    n_pad = n_blocks * MOE_BLOCK
    buf_tok = jnp.full((n_pad,), T, jnp.int32).at[dest].set(tok_flat[order])
    buf_gate = jnp.zeros((n_pad,), jnp.float32).at[dest].set(g_flat[order])
    blk_e = jnp.minimum(jnp.searchsorted(pad_end, jnp.arange(n_blocks, dtype=jnp.int32) * MOE_BLOCK,
                                         side='right'), N_EXPERTS - 1)
    x_pad = jnp.concatenate([x, jnp.zeros((1, D), x.dtype)], axis=0)

    def expert_block(args):
        tok, e = args
        hb = x_pad[tok] @ w_gu[e] + b_gu[e]
        g = jnp.minimum(hb[:, ::2], SWIGLU_LIMIT)
        lin = jnp.clip(hb[:, 1::2], -SWIGLU_LIMIT, SWIGLU_LIMIT)
        a = g * jax.nn.sigmoid(SWIGLU_ALPHA * g) * (lin + 1)
        return a @ w_down[e] + b_down[e]

    y_buf = lax.map(expert_block, (buf_tok.reshape(n_blocks, MOE_BLOCK), blk_e))
    y = y_buf.reshape(n_pad, D) * buf_gate[:, None].astype(x.dtype)
    return jax.ops.segment_sum(y, buf_tok, num_segments=T + 1)[:T]


def setup_inputs(seed: int = 0) -> dict:
    key = jax.random.key(seed)
    ks = iter(jax.random.split(key, 48))
    f32 = jnp.float32
    D = D_MODEL

    def nrm(shape, s):
        return jax.random.normal(next(ks), shape, f32) * s

    return {
        'x_prompt': nrm((BATCH, SEQ, D), 1.0),
        'x_sample': nrm((DEC_BATCH, DEC_SEQ, D), 1.0),
        'cache_k': nrm((DEC_BATCH, N_EVEN, PAST_LEN, NA_HEADS, NA_HEAD_DIM), 1.0),
        'cache_v': nrm((DEC_BATCH, N_EVEN, PAST_LEN, NA_HEADS, NA_HEAD_DIM), 1.0),
        'c': nrm((DEC_BATCH, D), 1.0),
        'c_ctx': nrm((D,), 1.0),
        'ada_w': nrm((DEPTH, D, 6 * D), 0.5 * D ** -0.5),
        'ada_b': nrm((DEPTH, 6 * D), 0.02),
        'norm_mix_pre': 1.0 + nrm((DEPTH, D), 0.05),
        'norm_mix_post': 1.0 + nrm((DEPTH, D), 0.05),
        'norm_ffn_pre': 1.0 + nrm((DEPTH, D), 0.05),
        'norm_ffn_post': 1.0 + nrm((DEPTH, D), 0.05),
        'ab_in_w': nrm((N_EVEN, D, AB_IN), D ** -0.5),
        'ab_out_w': nrm((N_EVEN, AB_OUT, D), AB_OUT ** -0.5),
        'hy_short_w': nrm((N_EVEN, HY_SHORT, 3 * HY_DIM), HY_SHORT ** -0.5),
        'hy_short_b': nrm((N_EVEN, 3 * HY_DIM), 0.02),
        'hy_w1': nrm((N_EVEN, HY_EMB, HY_FILTER_HIDDEN), HY_EMB ** -0.5),
        'hy_b1': nrm((N_EVEN, HY_FILTER_HIDDEN), 0.1),
        'hy_w2': nrm((N_EVEN, HY_FILTER_HIDDEN, HY_FILTER_HIDDEN), HY_FILTER_HIDDEN ** -0.5),
        'hy_b2': nrm((N_EVEN, HY_FILTER_HIDDEN), 0.1),
        'hy_w3': nrm((N_EVEN, HY_FILTER_HIDDEN, 2 * HY_ORDER * HY_DIM), HY_FILTER_HIDDEN ** -0.5),
        'hy_freq': 1.0 + nrm((N_EVEN, HY_FILTER_HIDDEN), 0.1),
        'hy_bias': nrm((N_EVEN, HY_ORDER, HY_DIM), 1.0),
        'na_rpb': nrm((N_EVEN, NA_HEADS, 2 * WIN_R - 1, 2 * WIN_C - 1), 0.1),
        'pool_in_w': nrm((N_ODD, D, POOL_DIM), D ** -0.5),
        'pool_group_w': nrm((N_ODD, len(POOL_WINDOWS), POOL_GROUP, POOL_GROUP), POOL_GROUP ** -0.5),
        'pool_scale': 1.0 + nrm((N_ODD, POOL_DIM), 0.1),
        'pool_out_w': nrm((N_ODD, POOL_DIM, D), POOL_DIM ** -0.5),
        'router_w': nrm((DEPTH, D, N_EXPERTS), D ** -0.5),
        'router_b': nrm((DEPTH, N_EXPERTS), 0.01),
        'exp_w_gu': nrm((DEPTH, N_EXPERTS, D, 2 * D_FF), D ** -0.5),
        'exp_b_gu': nrm((DEPTH, N_EXPERTS, 2 * D_FF), 0.02),
        'exp_w_down': nrm((DEPTH, N_EXPERTS, D_FF, D), D_FF ** -0.5),
        'exp_b_down': nrm((DEPTH, N_EXPERTS, D), 0.02),
    }


def reference(x_prompt, x_sample, cache_k, cache_v, c, c_ctx, ada_w, ada_b,
              norm_mix_pre, norm_mix_post, norm_ffn_pre, norm_ffn_post,
              ab_in_w, ab_out_w, hy_short_w, hy_short_b, hy_w1, hy_b1, hy_w2, hy_b2, hy_w3,
              hy_freq, hy_bias, na_rpb, pool_in_w, pool_group_w, pool_scale, pool_out_w,
              router_w, router_b, exp_w_gu, exp_b_gu, exp_w_down, exp_b_down):
    L_ctx = x_prompt.shape[1]
    L_lat = x_sample.shape[1]
    xp, xs = x_prompt, x_sample
    new_k, new_v = [], []
    for layer in range(DEPTH):
        j = layer // 2
        sp1, cp1, gp1, sp2, cp2, gp2 = _adaln(c_ctx, ada_w[layer], ada_b[layer])
        ms = [m[:, None, :] for m in _adaln(c, ada_w[layer], ada_b[layer])]
        ss1, cs1, gs1, ss2, cs2, gs2 = ms
        hp = _modulate(xp, norm_mix_pre[layer], sp1, cp1)
        hs = _modulate(xs, norm_mix_pre[layer], ss1, cs1)
        if layer % 2 == 0:
            pp = hp @ ab_in_w[j]
            ps = hs @ ab_in_w[j]
            kf_p = _hyena_filter_fft(L_ctx, hy_w1[j], hy_b1[j], hy_w2[j], hy_b2[j], hy_w3[j], hy_freq[j])
            kf_s = _hyena_filter_fft(L_lat, hy_w1[j], hy_b1[j], hy_w2[j], hy_b2[j], hy_w3[j], hy_freq[j])
            yh_p = _hyena(pp[..., :3 * HY_DIM], kf_p, hy_short_w[j], hy_short_b[j], hy_bias[j])
            yh_s = _hyena(ps[..., :3 * HY_DIM], kf_s, hy_short_w[j], hy_short_b[j], hy_bias[j])
            qp, kp, vp = _split_qkv(pp)
            new_k.append(kp)
            new_v.append(vp)
            ya_p = _ctx_attention(qp, kp, vp)
            qs, ks_, vs = _split_qkv(ps)
            ya_s = _na_latent(qs, ks_, vs, cache_k[:, j], cache_v[:, j], na_rpb[j])
            mp = jnp.concatenate([yh_p, ya_p], axis=-1) @ ab_out_w[j]
            msx = jnp.concatenate([yh_s, ya_s], axis=-1) @ ab_out_w[j]
        else:
            mp = _pool_mixer(hp, pool_in_w[j], pool_group_w[j], pool_scale[j], pool_out_w[j])
            msx = _pool_mixer(hs, pool_in_w[j], pool_group_w[j], pool_scale[j], pool_out_w[j])
        xp = xp + gp1 * _rms_norm(mp, norm_mix_post[layer])
        xs = xs + gs1 * _rms_norm(msx, norm_mix_post[layer])
        hp = _modulate(xp, norm_ffn_pre[layer], sp2, cp2)
        hs = _modulate(xs, norm_ffn_pre[layer], ss2, cs2)
        moe_args = (router_w[layer], router_b[layer], exp_w_gu[layer], exp_b_gu[layer],
                    exp_w_down[layer], exp_b_down[layer])
        fp = _moe(hp.reshape(-1, D_MODEL), *moe_args).reshape(hp.shape)
        fs = _moe(hs.reshape(-1, D_MODEL), *moe_args).reshape(hs.shape)
        xp = xp + gp2 * _rms_norm(fp, norm_ffn_post[layer])
        xs = xs + gs2 * _rms_norm(fs, norm_ffn_post[layer])
    new_cache_k = jnp.stack(new_k, axis=1)
    new_cache_v = jnp.stack(new_v, axis=1)
    return (xp, xs, new_cache_k, new_cache_v)
```

```python
import functools
import math

import jax
import jax.numpy as jnp
import numpy as np
from jax import lax
from jax.experimental import pallas as pl
from jax.experimental.pallas import tpu as pltpu

F32 = jnp.float32
BF16 = jnp.bfloat16

D = 2048
N_PROMPT_SEQ = 32
L_PROMPT = 256
N_LATENT_SEQ = 4
L_LATENT = 2048
T_PROMPT = N_PROMPT_SEQ * L_PROMPT
T_LATENT = N_LATENT_SEQ * L_LATENT
T_ALL = T_PROMPT + T_LATENT
N_LAYERS = 2
GRID_W = 64
GRID_ROWS = L_LATENT // GRID_W
HY_DIM = 1024
HY_EMB = 33
HY_DECAY_TARGET = 1e-2
HY_FAST_DECAY = 0.3
HY_SLOW_DECAY = 1.5
N_HEADS = 16
HEAD_DIM = 64
NA_DIM = N_HEADS * HEAD_DIM
WIN_R = 8
WIN_C = 16
AB_IN = 3 * HY_DIM + 3 * NA_DIM
POOL_WINDOWS = (2, 4, 8, 16)
POOL_GROUP = D // len(POOL_WINDOWS)
POOL_HALO = 8
N_EXPERTS = 32
TOP_K = 4
D_FF = D
SWIGLU_LIMIT = 7.0
SWIGLU_ALPHA = 1.702
RMS_EPS = 1e-6
NEG_BIG = -1e30

LANES = 128
MOE_TM = 256
N_ASSIGN = T_ALL * TOP_K
MOE_BLOCKS = -(-(N_ASSIGN + N_EXPERTS * (MOE_TM - 1)) // MOE_TM)
MOE_ROWS = MOE_BLOCKS * MOE_TM
MIB = 1 << 20


def _params(semantics, vmem_mib):
    return pltpu.CompilerParams(dimension_semantics=semantics, vmem_limit_bytes=vmem_mib * MIB)


def _mod_row(row0):
    return jnp.where(row0 < T_PROMPT, 0, 1 + (row0 - T_PROMPT) // L_LATENT)


def _rms(x):
    return x * lax.rsqrt(jnp.mean(x * x, axis=-1, keepdims=True) + RMS_EPS)


def _split_bf16(x):
    hi = x.astype(BF16)
    lo = (x - hi.astype(F32)).astype(BF16)
    return hi, lo


def _dot(a, b):
    return jnp.dot(a, b, preferred_element_type=F32)


def _adaln_kernel(cv_ref, w_ref, b_ref, o_ref):
    s = jax.nn.silu(cv_ref[...]).astype(BF16)
    o_ref[0] = _dot(s, w_ref[0].astype(BF16)) + b_ref[0]


def _adaln(cvec, ada_w, ada_b):
    n = ada_w.shape[-1]
    tn = 1024
    return pl.pallas_call(
        _adaln_kernel,
        grid=(N_LAYERS, n // tn),
        in_specs=[pl.BlockSpec((8, D), lambda l, j: (0, 0)),
                  pl.BlockSpec((1, D, tn), lambda l, j: (l, 0, j)),
                  pl.BlockSpec((1, 1, tn), lambda l, j: (l, 0, j))],
        out_specs=pl.BlockSpec((1, 8, tn), lambda l, j: (l, 0, j)),
        out_shape=jax.ShapeDtypeStruct((N_LAYERS, 8, n), F32),
        compiler_params=_params(("parallel", "arbitrary"), 40),
        name="adaln",
    )(cvec, ada_w, ada_b.reshape(N_LAYERS, 1, n))


def _modmm_kernel(x_ref, g_ref, sh_ref, sc_ref, w_ref, o_ref, h_ref):
    @pl.when(pl.program_id(1) == 0)
    def _():
        h = _rms(x_ref[...]) * g_ref[...] * (1.0 + sc_ref[0]) + sh_ref[0]
        h_ref[...] = h.astype(BF16)

    o_ref[...] = _dot(h_ref[...], w_ref[...])


def _modmm(x, g, shift, scale, w_bf16, name):
    n = w_bf16.shape[1]
    tm, tn = 512, 1024
    mod_spec = pl.BlockSpec((1, 1, D), lambda i, j: (_mod_row(i * tm), 0, 0))
    return pl.pallas_call(
        _modmm_kernel,
        grid=(T_ALL // tm, n // tn),
        in_specs=[pl.BlockSpec((tm, D), lambda i, j: (i, 0)),
                  pl.BlockSpec((1, D), lambda i, j: (0, 0)),
                  mod_spec, mod_spec,
                  pl.BlockSpec((D, tn), lambda i, j: (0, j))],
        out_specs=pl.BlockSpec((tm, tn), lambda i, j: (i, j)),
        out_shape=jax.ShapeDtypeStruct((T_ALL, n), F32),
        scratch_shapes=[pltpu.VMEM((tm, D), BF16)],
        compiler_params=_params(("parallel", "arbitrary"), 40),
        name=name,
    )(x, g, shift, scale, w_bf16)


def _shortconv_kernel(z_ref, w_ref, b_ref, o_ref):
    z = z_ref[...]
    n = z.shape[0]
    row = lax.broadcasted_iota(jnp.int32, z.shape, 0)
    prev = jnp.where(row == 0, 0.0, pltpu.roll(z, 1, 0))
    nxt = jnp.where(row == n - 1, 0.0, pltpu.roll(z, n - 1, 0))
    o_ref[...] = w_ref[0:1] * prev + w_ref[1:2] * z + w_ref[2:3] * nxt + b_ref[...]


def _shortconv(p, w, b, n_seq, seq_len, row_block0):
    c = 3 * HY_DIM
    tc = 512
    return pl.pallas_call(
        _shortconv_kernel,
        grid=(n_seq, c // tc),
        in_specs=[pl.BlockSpec((seq_len, tc), lambda s, j: (row_block0 + s, j)),
                  pl.BlockSpec((3, tc), lambda s, j: (0, j)),
                  pl.BlockSpec((1, tc), lambda s, j: (0, j))],
        out_specs=pl.BlockSpec((seq_len, tc), lambda s, j: (s, j)),
        out_shape=jax.ShapeDtypeStruct((n_seq * seq_len, c), F32),
        compiler_params=_params(("parallel", "parallel"), 40),
        name=f"shortconv_{seq_len}",
    )(p, w, b.reshape(1, c))


def _seqmm_kernel(a_ref, x_ref, o_ref):
    o_ref[...] = _dot(a_ref[...], x_ref[...].astype(BF16))


def _seqmm(a_bf16, x, n_seq, col_block0, n_cols, name):
    m, k = a_bf16.shape
    tm = min(m, 1024)
    tn = 512
    return pl.pallas_call(
        _seqmm_kernel,
        grid=(n_seq, n_cols // tn, m // tm),
        in_specs=[pl.BlockSpec((tm, k), lambda s, j, i: (i, 0)),
                  pl.BlockSpec((k, tn), lambda s, j, i: (s, col_block0 + j))],
        out_specs=pl.BlockSpec((tm, tn), lambda s, j, i: (s * (m // tm) + i, j)),
        out_shape=jax.ShapeDtypeStruct((n_seq * m, n_cols), F32),
        compiler_params=_params(("parallel", "parallel", "arbitrary"), 40),
        name=name,
    )(a_bf16, x)


def _hyena_inv_kernel(g_ref, y_ref, k_ref, yin_ref, gate_ref, bias_ref, o_ref, z_ref, *, seq_len):
    @pl.when(pl.program_id(2) == 0)
    def _():
        yc, ys = y_ref[0:seq_len], y_ref[seq_len:]
        kc, ks = k_ref[0:seq_len], k_ref[seq_len:]
        z_ref[0:seq_len] = (yc * kc - ys * ks).astype(BF16)
        z_ref[seq_len:] = (yc * ks + ys * kc).astype(BF16)

    conv = _dot(g_ref[...], z_ref[...]) * (1.0 / seq_len)
    o_ref[...] = gate_ref[...] * (conv + yin_ref[...] * bias_ref[...])


def _hyena_inv(g_bf16, yspec, kspec, order, yin, yin_col0, u, gate_col0, bias, n_seq, seq_len):
    tn = 256
    tt = min(seq_len, 512)
    nb = HY_DIM // tn
    kern = functools.partial(_hyena_inv_kernel, seq_len=seq_len)
    return pl.pallas_call(
        kern,
        grid=(n_seq, nb, seq_len // tt),
        in_specs=[pl.BlockSpec((tt, 2 * seq_len), lambda s, j, t: (t, 0)),
                  pl.BlockSpec((2 * seq_len, tn), lambda s, j, t: (s, j)),
                  pl.BlockSpec((2 * seq_len, tn), lambda s, j, t: (0, order * nb + j)),
                  pl.BlockSpec((tt, tn), lambda s, j, t: (s * (seq_len // tt) + t, yin_col0 // tn + j)),
                  pl.BlockSpec((tt, tn), lambda s, j, t: (s * (seq_len // tt) + t, gate_col0 // tn + j)),
                  pl.BlockSpec((1, tn), lambda s, j, t: (0, order * nb + j))],
        out_specs=pl.BlockSpec((tt, tn), lambda s, j, t: (s * (seq_len // tt) + t, j)),
        out_shape=jax.ShapeDtypeStruct((n_seq * seq_len, HY_DIM), F32),
        scratch_shapes=[pltpu.VMEM((2 * seq_len, tn), BF16)],
        compiler_params=_params(("parallel", "parallel", "arbitrary"), 48),
        name=f"hyena_inv_{seq_len}_{order}",
    )(g_bf16, yspec, kspec, yin, u, bias.reshape(1, 2 * HY_DIM))


def _dft_matrices(seq_len):
    f = lax.broadcasted_iota(jnp.int32, (seq_len, seq_len), 0)
    s = lax.broadcasted_iota(jnp.int32, (seq_len, seq_len), 1)
    ang = (((2 * f + 1) * s) % (4 * seq_len)).astype(F32) * (math.pi / (2 * seq_len))
    c, sn = jnp.cos(ang), jnp.sin(ang)
    fwd = jnp.concatenate([c, sn], axis=0).astype(BF16)
    inv = jnp.concatenate([c.T, sn.T], axis=1).astype(BF16)
    return fwd, inv


def _hyena_filter_time(seq_len, w1, b1, w2, b2, w3, freq):
    hp = lax.Precision.HIGHEST
    t = jnp.linspace(0.0, 1.0, seq_len, dtype=F32)[:, None]
    bands = (HY_EMB - 1) // 2
    ang = 2.0 * math.pi * jnp.arange(seq_len, dtype=F32)[:, None] / seq_len
    fb = jnp.linspace(1e-4, bands - 1, bands, dtype=F32)[None, :]
    z = jnp.concatenate([t, jnp.cos(fb * ang), -jnp.sin(fb * ang)], axis=-1)
    h = jnp.sin(freq * (jnp.dot(z, w1, precision=hp) + b1))
    h = jnp.sin(freq * (jnp.dot(h, w2, precision=hp) + b2))
    h = jnp.dot(h, w3, precision=hp).reshape(seq_len, 2, 2, HY_DIM)
    deltas = jnp.linspace(math.log(HY_DECAY_TARGET) / HY_SLOW_DECAY,
                          math.log(HY_DECAY_TARGET) / HY_FAST_DECAY, HY_DIM, dtype=F32)
    h = h * jnp.exp(-t * jnp.abs(deltas))[:, None, None, :]
    return h * lax.rsqrt(jnp.sum(h * h, axis=(0, 1), keepdims=True))


def _filter_spec_kernel(a_ref, hf_ref, hb_ref, o_ref, *, seq_len, tm):
    sign = jnp.where(pl.program_id(1) * tm < seq_len, 1.0, -1.0)
    hb = hb_ref[...]
    hb = jnp.where(lax.broadcasted_iota(jnp.int32, hb.shape, 0) == 0, 0.0, hb)
    hi, lo = _split_bf16(hf_ref[...] + sign * hb)
    o_ref[...] = _dot(a_ref[...], hi) + _dot(a_ref[...], lo)


def _hyena_filter_spec(h, fwd_bf16, seq_len):
    h2d = h.reshape(seq_len, 4 * HY_DIM)
    tm = min(seq_len, 1024)
    tn = 512
    nb = 2 * HY_DIM // tn
    kern = functools.partial(_filter_spec_kernel, seq_len=seq_len, tm=tm)
    return pl.pallas_call(
        kern,
        grid=(nb, 2 * seq_len // tm),
        in_specs=[pl.BlockSpec((tm, seq_len), lambda j, i: (i, 0)),
                  pl.BlockSpec((seq_len, tn), lambda j, i: (0, j)),
                  pl.BlockSpec((seq_len, tn), lambda j, i: (0, nb + j))],
        out_specs=pl.BlockSpec((tm, tn), lambda j, i: (i, j)),
        out_shape=jax.ShapeDtypeStruct((2 * seq_len, 2 * HY_DIM), F32),
        compiler_params=_params(("parallel", "arbitrary"), 40),
        name=f"filter_spec_{seq_len}",
    )(fwd_bf16, h2d, h2d)


def _hyena(p, short_w, short_b, bias, h_time, n_seq, seq_len, row_block0):
    fwd, inv = _dft_matrices(seq_len)
    kspec = _hyena_filter_spec(h_time, fwd, seq_len)
    u = _shortconv(p, short_w, short_b, n_seq, seq_len, row_block0)
    yspec = _seqmm(fwd, u, n_seq, 2 * HY_DIM // 512, HY_DIM, f"hyena_fwd_{seq_len}_0")
    y1 = _hyena_inv(inv, yspec, kspec, 0, u, 2 * HY_DIM, u, 0, bias, n_seq, seq_len)
    yspec = _seqmm(fwd, y1, n_seq, 0, HY_DIM, f"hyena_fwd_{seq_len}_1")
    return _hyena_inv(inv, yspec, kspec, 1, y1, 0, u, HY_DIM, bias, n_seq, seq_len)


def _nt_dot(a, b):
    return lax.dot_general(a, b, (((1,), (1,)), ((), ())), preferred_element_type=F32)


def _ctx_attn_kernel(q_ref, k_ref, v_ref, o_ref, ko_ref, vo_ref):
    scale = HEAD_DIM ** -0.5
    ko_ref[...] = k_ref[...]
    vo_ref[...] = v_ref[...]
    for h in range(N_HEADS):
        sl = slice(h * HEAD_DIM, (h + 1) * HEAD_DIM)
        s = _nt_dot(q_ref[:, sl].astype(BF16), k_ref[:, sl].astype(BF16)) * scale
        e = jnp.exp(s - jnp.max(s, axis=-1, keepdims=True))
        pr = e / jnp.sum(e, axis=-1, keepdims=True)
        o_ref[:, sl] = _dot(pr.astype(BF16), v_ref[:, sl].astype(BF16))


def _ctx_attn(p):
    qb = 3 * HY_DIM // NA_DIM
    spec = lambda c: pl.BlockSpec((L_PROMPT, NA_DIM), lambda s: (s, c))
    out = jax.ShapeDtypeStruct((T_PROMPT, NA_DIM), F32)
    return pl.pallas_call(
        _ctx_attn_kernel,
        grid=(N_PROMPT_SEQ,),
        in_specs=[spec(qb), spec(qb + 1), spec(qb + 2)],
        out_specs=[spec(0), spec(0), spec(0)],
        out_shape=[out, out, out],
        compiler_params=_params(("parallel",), 40),
        name="ctx_attn",
    )(p, p, p)


def _na_kernel(q_ref, k_ref, v_ref, kc_ref, vc_ref, b_ref, o_ref):
    r = pl.program_id(1)
    scale = HEAD_DIM ** -0.5
    row_start = jnp.clip(r - WIN_R // 2, 0, GRID_ROWS - WIN_R)
    k0 = pl.multiple_of(row_start * GRID_W, GRID_W)
    n_loc = WIN_R * GRID_W
    for h in range(N_HEADS):
        sl = slice(h * HEAD_DIM, (h + 1) * HEAD_DIM)
        q = q_ref[:, sl].astype(BF16)
        s_loc = _nt_dot(q, k_ref[pl.ds(k0, n_loc), sl].astype(BF16)) * scale + b_ref[0, h]
        s_ctx = _nt_dot(q, kc_ref[0, :, sl].astype(BF16)) * scale
        m = jnp.maximum(jnp.max(s_loc, axis=-1, keepdims=True), jnp.max(s_ctx, axis=-1, keepdims=True))
        e_loc = jnp.exp(s_loc - m)
        e_ctx = jnp.exp(s_ctx - m)
        den = jnp.sum(e_loc, axis=-1, keepdims=True) + jnp.sum(e_ctx, axis=-1, keepdims=True)
        o_ref[:, sl] = (_dot((e_loc / den).astype(BF16), v_ref[pl.ds(k0, n_loc), sl].astype(BF16))
                        + _dot((e_ctx / den).astype(BF16), vc_ref[0, :, sl].astype(BF16)))


def _na_bias_table(rpb):
    q = np.arange(GRID_W)[:, None]
    kc = np.arange(GRID_W)[None, :]
    cs = np.clip(q - WIN_C // 2, 0, GRID_W - WIN_C)
    valid = (kc >= cs) & (kc < cs + WIN_C)
    col_off = np.clip(kc - q, -(WIN_C - 1), WIN_C - 1) + WIN_C - 1
    b = rpb.astype(F32)[:, :, col_off]
    b = jnp.where(jnp.asarray(valid)[None, None], b, NEG_BIG)
    rows = np.arange(WIN_R)[:, None] + np.arange(WIN_R)[None, :]
    b = b[:, rows]
    return b.transpose(1, 0, 3, 2, 4).reshape(WIN_R, N_HEADS, GRID_W, WIN_R * GRID_W)


def _na_attn(p, cache_k, cache_v, rpb):
    qb = 3 * HY_DIM // NA_DIM
    q_row0 = T_PROMPT // GRID_W
    seq0 = T_PROMPT // L_LATENT
    bias = _na_bias_table(rpb)

    def bias_idx(b, r):
        row_start = jnp.clip(r - WIN_R // 2, 0, GRID_ROWS - WIN_R)
        return (row_start - r + WIN_R - 1, 0, 0, 0)

    ctx_spec = pl.BlockSpec((1, cache_k.shape[1], NA_DIM), lambda b, r: (b, 0, 0))
    return pl.pallas_call(
        _na_kernel,
        grid=(N_LATENT_SEQ, GRID_ROWS),
        in_specs=[pl.BlockSpec((GRID_W, NA_DIM), lambda b, r: (q_row0 + b * GRID_ROWS + r, qb)),
                  pl.BlockSpec((L_LATENT, NA_DIM), lambda b, r: (seq0 + b, qb + 1)),
                  pl.BlockSpec((L_LATENT, NA_DIM), lambda b, r: (seq0 + b, qb + 2)),
                  ctx_spec, ctx_spec,
                  pl.BlockSpec((1, N_HEADS, GRID_W, WIN_R * GRID_W), bias_idx)],
        out_specs=pl.BlockSpec((GRID_W, NA_DIM), lambda b, r: (b * GRID_ROWS + r, 0)),
        out_shape=jax.ShapeDtypeStruct((T_LATENT, NA_DIM), F32),
        compiler_params=_params(("parallel", "arbitrary"), 52),
        name="na_attn",
    )(p, p, p, cache_k, cache_v, bias)


def _pool_kernel(prev_ref, cur_ref, next_ref, gw_ref, sc_ref, o_ref, ext_ref, *, tm):
    i = pl.program_id(0)
    row0 = i * tm
    seq_len = jnp.where(row0 < T_PROMPT, L_PROMPT, L_LATENT)
    pos0 = jnp.where(row0 < T_PROMPT, row0 % L_PROMPT, (row0 - T_PROMPT) % L_LATENT)
    first = pos0 == 0
    last = pos0 + tm == seq_len
    h = POOL_HALO
    ext_ref[0:h] = jnp.where(first, 0.0, prev_ref[...])
    ext_ref[h:h + tm] = cur_ref[...]
    ext_ref[h + tm:] = jnp.where(last, 0.0, next_ref[...])
    pos = pos0 + lax.broadcasted_iota(jnp.int32, (tm, 1), 0)
    for g, w in enumerate(POOL_WINDOWS):
        cols = slice(g * POOL_GROUP, (g + 1) * POOL_GROUP)
        acc = ext_ref[h - w // 2:h - w // 2 + tm, cols]
        for j in range(1, w):
            acc = acc + ext_ref[h - w // 2 + j:h - w // 2 + j + tm, cols]
        cnt = jnp.minimum(pos + w // 2, seq_len) - jnp.maximum(pos - w // 2, 0)
        d = acc / cnt.astype(F32) - cur_ref[:, cols]
        o_ref[:, cols] = _dot(d.astype(BF16), gw_ref[g]) * sc_ref[:, cols]


def _pool(u, group_w_bf16, scale):
    tm = 256
    hb = tm // POOL_HALO
    n_halo = T_ALL // POOL_HALO
    kern = functools.partial(_pool_kernel, tm=tm)
    return pl.pallas_call(
        kern,
        grid=(T_ALL // tm,),
        in_specs=[pl.BlockSpec((POOL_HALO, D), lambda i: (jnp.maximum(i * hb - 1, 0), 0)),
                  pl.BlockSpec((tm, D), lambda i: (i, 0)),
                  pl.BlockSpec((POOL_HALO, D), lambda i: (jnp.minimum((i + 1) * hb, n_halo - 1), 0)),
                  pl.BlockSpec((len(POOL_WINDOWS), POOL_GROUP, POOL_GROUP), lambda i: (0, 0, 0)),
                  pl.BlockSpec((1, D), lambda i: (0, 0))],
        out_specs=pl.BlockSpec((tm, D), lambda i: (i, 0)),
        out_shape=jax.ShapeDtypeStruct((T_ALL, D), F32),
        scratch_shapes=[pltpu.VMEM((tm + 2 * POOL_HALO, D), F32)],
        compiler_params=_params(("parallel",), 40),
        name="pool",
    )(u, u, u, group_w_bf16, scale.reshape(1, D))


def _post_mixer_kernel(a1_ref, a2_ref, w_ref, x_ref, gpost_ref, gate_ref, gpre_ref, sh_ref, sc_ref,
                       rwh_ref, rwl_ref, rb_ref, xo_ref, h_ref, lg_ref):
    k1 = a1_ref.shape[1]
    m = _dot(a1_ref[...].astype(BF16), w_ref[0:k1]) + _dot(a2_ref[...].astype(BF16), w_ref[k1:])
    xn = x_ref[...] + gate_ref[0] * (_rms(m) * gpost_ref[...])
    xo_ref[...] = xn
    h = _rms(xn) * gpre_ref[...] * (1.0 + sc_ref[0]) + sh_ref[0]
    h_ref[...] = h
    hh, hl = _split_bf16(h)
    lg_ref[...] = _dot(hh, rwh_ref[...]) + _dot(hl, rwh_ref[...]) + _dot(hh, rwl_ref[...]) + rb_ref[...]


def _post_mixer(a1, a1_col, a2, a2_col, w_bf16, x, g_post, gate, g_pre, shift, scale, rw_hi, rw_lo, rb, name):
    tm = 256
    kh = w_bf16.shape[0] // 2
    row = lambda i: (i, 0)
    const = lambda i: (0, 0)
    mod_spec = pl.BlockSpec((1, 1, D), lambda i: (_mod_row(i * tm), 0, 0))
    vec = pl.BlockSpec((1, D), const)
    return pl.pallas_call(
        _post_mixer_kernel,
        grid=(T_ALL // tm,),
        in_specs=[pl.BlockSpec((tm, kh), lambda i: (i, a1_col)),
                  pl.BlockSpec((tm, kh), lambda i: (i, a2_col)),
                  pl.BlockSpec((2 * kh, D), const),
                  pl.BlockSpec((tm, D), row), vec, mod_spec, vec, mod_spec, mod_spec,
                  pl.BlockSpec((D, LANES), const), pl.BlockSpec((D, LANES), const),
                  pl.BlockSpec((1, LANES), const)],
        out_specs=[pl.BlockSpec((tm, D), row), pl.BlockSpec((tm, D), row), pl.BlockSpec((tm, LANES), row)],
        out_shape=[jax.ShapeDtypeStruct((T_ALL, D), F32), jax.ShapeDtypeStruct((T_ALL, D), F32),
                   jax.ShapeDtypeStruct((T_ALL, LANES), F32)],
        compiler_params=_params(("parallel",), 48),
        name=name,
    )(a1, a2, w_bf16, x, g_post, gate, g_pre, shift, scale, rw_hi, rw_lo, rb)


def _route_kernel(lg_ref, tri_ref, eidx_ref, gate_ref, rank_ref, cnt_ref, carry_ref):
    @pl.when(pl.program_id(0) == 0)
    def _():
        carry_ref[...] = jnp.zeros_like(carry_ref)

    shape = lg_ref.shape
    lane = lax.broadcasted_iota(jnp.int32, shape, 1).astype(F32)
    lg = jnp.where(lane < N_EXPERTS, lg_ref[...], -jnp.inf)
    multi = jnp.zeros(shape, F32)
    vals, sels = [], []
    eidx = jnp.zeros(shape, F32)
    for k in range(TOP_K):
        m = jnp.max(lg, axis=-1, keepdims=True)
        idx = jnp.min(jnp.where(lg == m, lane, float(LANES)), axis=-1, keepdims=True)
        sel = lane == idx
        multi = jnp.where(sel, 1.0, multi)
        lg = jnp.where(sel, -jnp.inf, lg)
        eidx = jnp.where(lane == k, idx, eidx)
        vals.append(m)
        sels.append(sel)
    exps = [jnp.exp(v - vals[0]) for v in vals]
    den = exps[0] + exps[1] + exps[2] + exps[3]
    rank_all = _dot(tri_ref[...], multi.astype(BF16)) + carry_ref[0:1]
    gate = jnp.zeros(shape, F32)
    rank = jnp.zeros(shape, F32)
    for k in range(TOP_K):
        gate = jnp.where(lane == k, exps[k] / den, gate)
        rank = jnp.where(lane == k, jnp.sum(jnp.where(sels[k], rank_all, 0.0), axis=-1, keepdims=True), rank)
    eidx_ref[...] = eidx.astype(jnp.int32)
    gate_ref[...] = gate
    rank_ref[...] = rank.astype(jnp.int32)
    carry_ref[0:1] = carry_ref[0:1] + jnp.sum(multi, axis=0, keepdims=True)
    cnt_ref[...] = carry_ref[...]


def _route(logits):
    tm = 256
    tri = (np.arange(tm)[:, None] > np.arange(tm)[None, :]).astype(np.float32)
    row = lambda i: (i, 0)
    spec = pl.BlockSpec((tm, LANES), row)
    return pl.pallas_call(
        _route_kernel,
        grid=(T_ALL // tm,),
        in_specs=[spec, pl.BlockSpec((tm, tm), lambda i: (0, 0))],
        out_specs=[spec, spec, spec, pl.BlockSpec((8, LANES), lambda i: (0, 0))],
        out_shape=[jax.ShapeDtypeStruct((T_ALL, LANES), jnp.int32), jax.ShapeDtypeStruct((T_ALL, LANES), F32),
                   jax.ShapeDtypeStruct((T_ALL, LANES), jnp.int32), jax.ShapeDtypeStruct((8, LANES), F32)],
        scratch_shapes=[pltpu.VMEM((8, LANES), F32)],
        compiler_params=_params(("arbitrary",), 32),
        name="route",
    )(logits, jnp.asarray(tri, BF16))


def _dest_kernel(eidx_ref, rank_ref, start_ref, o_ref):
    shape = eidx_ref.shape
    lane = lax.broadcasted_iota(jnp.int32, shape, 1).astype(F32)
    eidx = eidx_ref[...].astype(F32)
    starts = start_ref[...].astype(F32)
    dest = jnp.zeros(shape, F32)
    for k in range(TOP_K):
        e_k = jnp.sum(jnp.where(lane == k, eidx, 0.0), axis=-1, keepdims=True)
        start = jnp.sum(jnp.where(lane == e_k, starts, 0.0), axis=-1, keepdims=True)
        dest = jnp.where(lane == k, start, dest)
    o_ref[...] = jnp.where(lane < TOP_K, dest.astype(jnp.int32) + rank_ref[...], 0)


def _dest(eidx, rank, group_start):
    tm = 1024
    spec = pl.BlockSpec((tm, LANES), lambda i: (i, 0))
    return pl.pallas_call(
        _dest_kernel,
        grid=(T_ALL // tm,),
        in_specs=[spec, spec, pl.BlockSpec((1, LANES), lambda i: (0, 0))],
        out_specs=spec,
        out_shape=jax.ShapeDtypeStruct((T_ALL, LANES), jnp.int32),
        compiler_params=_params(("parallel",), 32),
        name="dest",
    )(eidx, rank, group_start)


def _dispatch_kernel(dest_ref, lo_ref, hi_ref, nu_ref, h_ref, o_ref, zero_ref, sem, zsem, *, tm):
    i = pl.program_id(0)
    base = i * tm * TOP_K

    @pl.when(i == 0)
    def _():
        zero_ref[...] = jnp.zeros_like(zero_ref)

    def issue(t, carry):
        for k in range(TOP_K):
            d = dest_ref[base + t * TOP_K + k]
            pltpu.make_async_copy(h_ref.at[pl.ds(t, 1)], o_ref.at[pl.ds(d, 1)], sem).start()
        return carry

    lax.fori_loop(0, tm, issue, 0)

    @pl.when(i < N_EXPERTS)
    def _():
        zrow = lambda r: pltpu.make_async_copy(zero_ref.at[pl.ds(0, 1)], o_ref.at[pl.ds(r, 1)], zsem)
        lax.fori_loop(lo_ref[i], hi_ref[i], lambda r, c: (zrow(r).start(), c)[1], 0)
        lax.fori_loop(lo_ref[i], hi_ref[i], lambda r, c: (zrow(r).wait(), c)[1], 0)

    blk = nu_ref[0] + i - N_EXPERTS

    @pl.when((i >= N_EXPERTS) & (blk < MOE_BLOCKS))
    def _():
        r0 = pl.multiple_of(blk * MOE_TM, MOE_TM)
        cp = pltpu.make_async_copy(zero_ref, o_ref.at[pl.ds(r0, MOE_TM)], zsem)
        cp.start()
        cp.wait()

    for k in range(TOP_K):
        pltpu.make_async_copy(h_ref, o_ref.at[pl.ds(0, tm)], sem).wait()


def _dispatch(dest_flat, pad_lo, pad_hi, n_used, h):
    tm = 128
    assert T_ALL // tm >= 2 * N_EXPERTS
    kern = functools.partial(_dispatch_kernel, tm=tm)
    return pl.pallas_call(
        kern,
        grid_spec=pltpu.PrefetchScalarGridSpec(
            num_scalar_prefetch=4,
            grid=(T_ALL // tm,),
            in_specs=[pl.BlockSpec((tm, D), lambda i, *_: (i, 0))],
            out_specs=pl.BlockSpec(memory_space=pl.ANY),
            scratch_shapes=[pltpu.VMEM((MOE_TM, D), F32), pltpu.SemaphoreType.DMA, pltpu.SemaphoreType.DMA]),
        out_shape=jax.ShapeDtypeStruct((MOE_ROWS, D), F32),
        compiler_params=_params(("arbitrary",), 32),
        name="dispatch",
    )(dest_flat, pad_lo, pad_hi, n_used, h)


def _expert_up_kernel(be_ref, nu_ref, x_ref, wg_ref, wl_ref, bg_ref, bl_ref, o_ref):
    @pl.when(pl.program_id(1) < nu_ref[0])
    def _():
        x = x_ref[...].astype(BF16)
        g = jnp.minimum(_dot(x, wg_ref[0]) + bg_ref[0], SWIGLU_LIMIT)
        lin = jnp.clip(_dot(x, wl_ref[0]) + bl_ref[0], -SWIGLU_LIMIT, SWIGLU_LIMIT)
        o_ref[...] = (g * jax.nn.sigmoid(SWIGLU_ALPHA * g) * (lin + 1.0)).astype(BF16)

    @pl.when(pl.program_id(1) >= nu_ref[0])
    def _():
        o_ref[...] = jnp.zeros_like(o_ref)


def _expert_up(blk_e, n_used, xs, wg, wl, bg, bl):
    tn = 1024
    blk = lambda j, i, be, nu: (jnp.minimum(i, nu[0] - 1), 0)
    wspec = pl.BlockSpec((1, D, tn), lambda j, i, be, nu: (be[i], 0, j))
    bspec = pl.BlockSpec((1, 1, tn), lambda j, i, be, nu: (be[i], 0, j))
    return pl.pallas_call(
        _expert_up_kernel,
        grid_spec=pltpu.PrefetchScalarGridSpec(
            num_scalar_prefetch=2,
            grid=(D_FF // tn, MOE_BLOCKS),
            in_specs=[pl.BlockSpec((MOE_TM, D), blk), wspec, wspec, bspec, bspec],
            out_specs=pl.BlockSpec((MOE_TM, tn), lambda j, i, be, nu: (i, j))),
        out_shape=jax.ShapeDtypeStruct((MOE_ROWS, D_FF), BF16),
        compiler_params=_params(("arbitrary", "arbitrary"), 48),
        name="expert_up",
    )(blk_e, n_used, xs, wg, wl, bg, bl)


def _expert_down_kernel(be_ref, nu_ref, a_ref, w_ref, b_ref, o_ref):
    @pl.when(pl.program_id(0) < nu_ref[0])
    def _():
        o_ref[...] = _dot(a_ref[...], w_ref[0]) + b_ref[0]

    @pl.when(pl.program_id(0) >= nu_ref[0])
    def _():
        o_ref[...] = jnp.zeros_like(o_ref)


def _expert_down(blk_e, n_used, a, wd, bd):
    blk = lambda i, be, nu: (jnp.minimum(i, nu[0] - 1), 0)
    return pl.pallas_call(
        _expert_down_kernel,
        grid_spec=pltpu.PrefetchScalarGridSpec(
            num_scalar_prefetch=2,
            grid=(MOE_BLOCKS,),
            in_specs=[pl.BlockSpec((MOE_TM, D_FF), blk),
                      pl.BlockSpec((1, D_FF, D), lambda i, be, nu: (be[i], 0, 0)),
                      pl.BlockSpec((1, 1, D), lambda i, be, nu: (be[i], 0, 0))],
            out_specs=pl.BlockSpec((MOE_TM, D), lambda i, be, nu: (i, 0))),
        out_shape=jax.ShapeDtypeStruct((MOE_ROWS, D), F32),
        compiler_params=_params(("arbitrary",), 48),
        name="expert_down",
    )(blk_e, n_used, a, wd, bd)


def _combine_kernel(dest_ref, y_ref, gate_ref, x_ref, gpost_ref, g2_ref, o_ref, buf_ref, sem, *, tm):
    base = pl.program_id(0) * tm * TOP_K

    def issue(t, carry):
        for k in range(TOP_K):
            d = dest_ref[base + t * TOP_K + k]
            pltpu.make_async_copy(y_ref.at[pl.ds(d, 1)], buf_ref.at[k, pl.ds(t, 1)], sem).start()
        return carry

    lax.fori_loop(0, tm, issue, 0)
    for k in range(TOP_K):
        pltpu.make_async_copy(y_ref.at[pl.ds(0, tm)], buf_ref.at[k], sem).wait()
    f = gate_ref[:, 0:1] * buf_ref[0]
    for k in range(1, TOP_K):
        f = f + gate_ref[:, k:k + 1] * buf_ref[k]
    o_ref[...] = x_ref[...] + g2_ref[0] * (_rms(f) * gpost_ref[...])


def _combine(dest_flat, ys, gate, x, g_post, gate2, name):
    tm = 128
    kern = functools.partial(_combine_kernel, tm=tm)
    return pl.pallas_call(
        kern,
        grid_spec=pltpu.PrefetchScalarGridSpec(
            num_scalar_prefetch=1,
            grid=(T_ALL // tm,),
            in_specs=[pl.BlockSpec(memory_space=pl.ANY),
                      pl.BlockSpec((tm, LANES), lambda i, d: (i, 0)),
                      pl.BlockSpec((tm, D), lambda i, d: (i, 0)),
                      pl.BlockSpec((1, D), lambda i, d: (0, 0)),
                      pl.BlockSpec((1, 1, D), lambda i, d: (_mod_row(i * tm), 0, 0))],
            out_specs=pl.BlockSpec((tm, D), lambda i, d: (i, 0)),
            scratch_shapes=[pltpu.VMEM((TOP_K, tm, D), F32), pltpu.SemaphoreType.DMA]),
        out_shape=jax.ShapeDtypeStruct((T_ALL, D), F32),
        compiler_params=_params(("arbitrary",), 32),
        name=name,
    )(dest_flat, ys, gate, x, g_post, gate2)


def _moe(h, logits, x, g_post, gate2, wg, wl, bg, bl, wd, bd, name):
    eidx, gate, rank, counts = _route(logits)
    cnt = counts[0, :N_EXPERTS].astype(jnp.int32)
    padded = (cnt + MOE_TM - 1) // MOE_TM * MOE_TM
    group_end = jnp.cumsum(padded)
    group_start = jnp.zeros((1, LANES), jnp.int32).at[0, :N_EXPERTS].set(group_end - padded)
    n_used = (group_end[-1] // MOE_TM).astype(jnp.int32)
    blk_e = jnp.searchsorted(group_end, jnp.arange(MOE_BLOCKS, dtype=jnp.int32) * MOE_TM, side="right")
    last_e = jnp.searchsorted(group_end, (n_used - 1) * MOE_TM, side="right")
    blk_e = jnp.minimum(blk_e, last_e).astype(jnp.int32)
    n_used = n_used.reshape(1)
    dest = _dest(eidx, rank, group_start)[:, :TOP_K].reshape(N_ASSIGN)
    xs = _dispatch(dest, group_end - padded + cnt, group_end, n_used, h)
    a = _expert_up(blk_e, n_used, xs, wg, wl, bg, bl)
    ys = _expert_down(blk_e, n_used, a, wd, bd)
    return _combine(dest, ys, gate, x, g_post, gate2, name)


def kernel(x_prompt, x_sample, cache_k, cache_v, c, c_ctx, ada_w, ada_b, norm_mix_pre, norm_mix_post,
           norm_ffn_pre, norm_ffn_post, ab_in_w, ab_out_w, hy_short_w, hy_short_b, hy_w1, hy_b1, hy_w2,
           hy_b2, hy_w3, hy_freq, hy_bias, na_rpb, pool_in_w, pool_group_w, pool_scale, pool_out_w,
           router_w, router_b, exp_w_gu, exp_b_gu, exp_w_down, exp_b_down):
    x = jnp.concatenate([x_prompt.reshape(T_PROMPT, D), x_sample.reshape(T_LATENT, D)], axis=0)
    cvec = jnp.concatenate([c_ctx[None], c, jnp.zeros((8 - 1 - N_LATENT_SEQ, D), F32)], axis=0)
    mods = _adaln(cvec, ada_w, ada_b).reshape(N_LAYERS, 8, 6, D)

    new_k = new_v = None
    for layer in range(N_LAYERS):
        sh1, sc1, g1, sh2, sc2, g2 = [mods[layer, :, k][:, None, :] for k in range(6)]
        row = lambda a: a[layer].reshape(1, D)
        if layer % 2 == 0:
            j = layer // 2
            p = _modmm(x, row(norm_mix_pre), sh1, sc1, ab_in_w[j].astype(BF16), "ab_in")
            filt = (hy_w1[j], hy_b1[j], hy_w2[j], hy_b2[j], hy_w3[j], hy_freq[j])
            hy = (hy_short_w[j], hy_short_b[j], hy_bias[j])
            yh_p = _hyena(p, *hy, _hyena_filter_time(L_PROMPT, *filt), N_PROMPT_SEQ, L_PROMPT, 0)
            yh_s = _hyena(p, *hy, _hyena_filter_time(L_LATENT, *filt), N_LATENT_SEQ, L_LATENT,
                          T_PROMPT // L_LATENT)
            ya_p, new_k, new_v = _ctx_attn(p)
            ck = cache_k[:, j].reshape(N_LATENT_SEQ, -1, NA_DIM)
            cv = cache_v[:, j].reshape(N_LATENT_SEQ, -1, NA_DIM)
            ya_s = _na_attn(p, ck, cv, na_rpb[j])
            a1 = jnp.concatenate([yh_p, yh_s], axis=0)
            a2 = jnp.concatenate([ya_p, ya_s], axis=0)
            a1_col = a2_col = 0
            w_out = ab_out_w[j].astype(BF16)
        else:
            j = layer // 2
            u = _modmm(x, row(norm_mix_pre), sh1, sc1, pool_in_w[j].astype(BF16), "pool_in")
            a1 = a2 = _pool(u, pool_group_w[j].astype(BF16), pool_scale[j])
            a1_col, a2_col = 0, 1
            w_out = pool_out_w[j].astype(BF16)
        rw = jnp.zeros((D, LANES), F32).at[:, :N_EXPERTS].set(router_w[layer])
        rw_hi, rw_lo = _split_bf16(rw)
        rb = jnp.zeros((1, LANES), F32).at[0, :N_EXPERTS].set(router_b[layer])
        x, h, logits = _post_mixer(a1, a1_col, a2, a2_col, w_out, x, row(norm_mix_post), g1,
                                   row(norm_ffn_pre), sh2, sc2, rw_hi, rw_lo, rb, f"post_mixer_{layer}")
        wg = exp_w_gu[layer, :, :, 0::2].astype(BF16)
        wl = exp_w_gu[layer, :, :, 1::2].astype(BF16)
        bg = exp_b_gu[layer, :, 0::2].reshape(N_EXPERTS, 1, D_FF)
        bl = exp_b_gu[layer, :, 1::2].reshape(N_EXPERTS, 1, D_FF)
        wd = exp_w_down[layer].astype(BF16)
        bd = exp_b_down[layer].reshape(N_EXPERTS, 1, D)
        x = _moe(h, logits, x, row(norm_ffn_post), g2, wg, wl, bg, bl, wd, bd, f"combine_{layer}")

    y_prompt = x[:T_PROMPT].reshape(x_prompt.shape)
    y_sample = x[T_PROMPT:].reshape(x_sample.shape)
    kv_shape = (N_PROMPT_SEQ, 1, L_PROMPT, N_HEADS, HEAD_DIM)
    return y_prompt, y_sample, new_k.reshape(kv_shape), new_v.reshape(kv_shape)
```

```python
import functools
import math

import jax
import jax.numpy as jnp
import numpy as np
from jax import lax
from jax.experimental import pallas as pl
from jax.experimental.pallas import tpu as pltpu

F32 = jnp.float32
BF16 = jnp.bfloat16

D = 2048
N_PROMPT_SEQ = 32
L_PROMPT = 256
N_LATENT_SEQ = 4
L_LATENT = 2048
T_PROMPT = N_PROMPT_SEQ * L_PROMPT
T_LATENT = N_LATENT_SEQ * L_LATENT
T_ALL = T_PROMPT + T_LATENT
N_LAYERS = 2
GRID_W = 64
GRID_ROWS = L_LATENT // GRID_W
HY_DIM = 1024
HY_EMB = 33
HY_DECAY_TARGET = 1e-2
HY_FAST_DECAY = 0.3
HY_SLOW_DECAY = 1.5
N_HEADS = 16
HEAD_DIM = 64
NA_DIM = N_HEADS * HEAD_DIM
WIN_R = 8
WIN_C = 16
AB_IN = 3 * HY_DIM + 3 * NA_DIM
POOL_WINDOWS = (2, 4, 8, 16)
POOL_GROUP = D // len(POOL_WINDOWS)
POOL_HALO = 8
N_EXPERTS = 32
TOP_K = 4
D_FF = D
SWIGLU_LIMIT = 7.0
SWIGLU_ALPHA = 1.702
RMS_EPS = 1e-6
NEG_BIG = -1e30

LANES = 128
MOE_TM = 256
FF_TILE = 1024
N_ASSIGN = T_ALL * TOP_K
MOE_BLOCKS = -(-(N_ASSIGN + N_EXPERTS * (MOE_TM - 1)) // MOE_TM)
MOE_ROWS = MOE_BLOCKS * MOE_TM
MIB = 1 << 20


def _params(semantics, vmem_mib):
    return pltpu.CompilerParams(dimension_semantics=semantics, vmem_limit_bytes=vmem_mib * MIB)


def _mod_row(row0):
    return jnp.where(row0 < T_PROMPT, 0, 1 + (row0 - T_PROMPT) // L_LATENT)


def _rms(x):
    return x * lax.rsqrt(jnp.mean(x * x, axis=-1, keepdims=True) + RMS_EPS)


def _split_bf16(x):
    hi = x.astype(BF16)
    lo = (x - hi.astype(F32)).astype(BF16)
    return hi, lo


def _dot(a, b):
    return jnp.dot(a, b, preferred_element_type=F32)


def _pack_bf16_pairs(x_bf16):
    half = x_bf16.shape[1] // 2
    bits = lax.bitcast_convert_type(x_bf16.astype(F32), jnp.uint32)
    return (bits[:, :half] & jnp.uint32(0xFFFF0000)) | (bits[:, half:] >> 16)


def _unpack_bf16_pairs(packed):
    hi = lax.bitcast_convert_type(packed & jnp.uint32(0xFFFF0000), F32).astype(BF16)
    lo = lax.bitcast_convert_type(packed << 16, F32).astype(BF16)
    return hi, lo


def _adaln_kernel(cv_ref, w_ref, b_ref, o_ref):
    s = jax.nn.silu(cv_ref[...]).astype(BF16)
    o_ref[0] = _dot(s, w_ref[0].astype(BF16)) + b_ref[0]


def _adaln(cvec, ada_w, ada_b):
    n = ada_w.shape[-1]
    tn = 1024
    return pl.pallas_call(
        _adaln_kernel,
        grid=(N_LAYERS, n // tn),
        in_specs=[pl.BlockSpec((8, D), lambda l, j: (0, 0)),
                  pl.BlockSpec((1, D, tn), lambda l, j: (l, 0, j)),
                  pl.BlockSpec((1, 1, tn), lambda l, j: (l, 0, j))],
        out_specs=pl.BlockSpec((1, 8, tn), lambda l, j: (l, 0, j)),
        out_shape=jax.ShapeDtypeStruct((N_LAYERS, 8, n), F32),
        compiler_params=_params(("parallel", "arbitrary"), 40),
        name="adaln",
    )(cvec, ada_w, ada_b.reshape(N_LAYERS, 1, n))


def _modmm_kernel(x_ref, g_ref, sh_ref, sc_ref, w_ref, o_ref, h_ref):
    @pl.when(pl.program_id(1) == 0)
    def _():
        h = _rms(x_ref[...]) * g_ref[...] * (1.0 + sc_ref[0]) + sh_ref[0]
        h_ref[...] = h.astype(BF16)

    o_ref[...] = _dot(h_ref[...], w_ref[...])


def _modmm(x, g, shift, scale, w_bf16, name):
    n = w_bf16.shape[1]
    tm, tn = 512, 1024
    mod_spec = pl.BlockSpec((1, 1, D), lambda i, j: (_mod_row(i * tm), 0, 0))
    return pl.pallas_call(
        _modmm_kernel,
        grid=(T_ALL // tm, n // tn),
        in_specs=[pl.BlockSpec((tm, D), lambda i, j: (i, 0)),
                  pl.BlockSpec((1, D), lambda i, j: (0, 0)),
                  mod_spec, mod_spec,
                  pl.BlockSpec((D, tn), lambda i, j: (0, j))],
        out_specs=pl.BlockSpec((tm, tn), lambda i, j: (i, j)),
        out_shape=jax.ShapeDtypeStruct((T_ALL, n), F32),
        scratch_shapes=[pltpu.VMEM((tm, D), BF16)],
        compiler_params=_params(("parallel", "arbitrary"), 40),
        name=name,
    )(x, g, shift, scale, w_bf16)


def _shortconv_kernel(z_ref, w_ref, b_ref, o_ref):
    z = z_ref[...]
    n = z.shape[0]
    row = lax.broadcasted_iota(jnp.int32, z.shape, 0)
    prev = jnp.where(row == 0, 0.0, pltpu.roll(z, 1, 0))
    nxt = jnp.where(row == n - 1, 0.0, pltpu.roll(z, n - 1, 0))
    o_ref[...] = w_ref[0:1] * prev + w_ref[1:2] * z + w_ref[2:3] * nxt + b_ref[...]


def _shortconv(p, w, b, n_seq, seq_len, row_block0):
    c = 3 * HY_DIM
    tc = 512
    return pl.pallas_call(
        _shortconv_kernel,
        grid=(n_seq, c // tc),
        in_specs=[pl.BlockSpec((seq_len, tc), lambda s, j: (row_block0 + s, j)),
                  pl.BlockSpec((3, tc), lambda s, j: (0, j)),
                  pl.BlockSpec((1, tc), lambda s, j: (0, j))],
        out_specs=pl.BlockSpec((seq_len, tc), lambda s, j: (s, j)),
        out_shape=jax.ShapeDtypeStruct((n_seq * seq_len, c), F32),
        compiler_params=_params(("parallel", "parallel"), 40),
        name=f"shortconv_{seq_len}",
    )(p, w, b.reshape(1, c))


def _seqmm_kernel(a_ref, x_ref, o_ref):
    o_ref[...] = _dot(a_ref[...], x_ref[...].astype(BF16))


def _seqmm(a_bf16, x, n_seq, col_block0, n_cols, name):
    m, k = a_bf16.shape
    tm = min(m, 1024)
    tn = 512
    return pl.pallas_call(
        _seqmm_kernel,
        grid=(n_seq, n_cols // tn, m // tm),
        in_specs=[pl.BlockSpec((tm, k), lambda s, j, i: (i, 0)),
                  pl.BlockSpec((k, tn), lambda s, j, i: (s, col_block0 + j))],
        out_specs=pl.BlockSpec((tm, tn), lambda s, j, i: (s * (m // tm) + i, j)),
        out_shape=jax.ShapeDtypeStruct((n_seq * m, n_cols), F32),
        compiler_params=_params(("parallel", "parallel", "arbitrary"), 40),
        name=name,
    )(a_bf16, x)


def _hyena_inv_kernel(g_ref, y_ref, k_ref, yin_ref, gate_ref, bias_ref, o_ref, z_ref, *, seq_len):
    @pl.when(pl.program_id(2) == 0)
    def _():
        yc, ys = y_ref[0:seq_len], y_ref[seq_len:]
        kc, ks = k_ref[0:seq_len], k_ref[seq_len:]
        z_ref[0:seq_len] = (yc * kc - ys * ks).astype(BF16)
        z_ref[seq_len:] = (yc * ks + ys * kc).astype(BF16)

    conv = _dot(g_ref[...], z_ref[...]) * (1.0 / seq_len)
    o_ref[...] = gate_ref[...] * (conv + yin_ref[...] * bias_ref[...])


def _hyena_inv(g_bf16, yspec, kspec, order, yin, yin_col0, u, gate_col0, bias, n_seq, seq_len):
    tn = 256
    tt = min(seq_len, 512)
    nb = HY_DIM // tn
    kern = functools.partial(_hyena_inv_kernel, seq_len=seq_len)
    return pl.pallas_call(
        kern,
        grid=(n_seq, nb, seq_len // tt),
        in_specs=[pl.BlockSpec((tt, 2 * seq_len), lambda s, j, t: (t, 0)),
                  pl.BlockSpec((2 * seq_len, tn), lambda s, j, t: (s, j)),
                  pl.BlockSpec((2 * seq_len, tn), lambda s, j, t: (0, order * nb + j)),
                  pl.BlockSpec((tt, tn), lambda s, j, t: (s * (seq_len // tt) + t, yin_col0 // tn + j)),
                  pl.BlockSpec((tt, tn), lambda s, j, t: (s * (seq_len // tt) + t, gate_col0 // tn + j)),
                  pl.BlockSpec((1, tn), lambda s, j, t: (0, order * nb + j))],
        out_specs=pl.BlockSpec((tt, tn), lambda s, j, t: (s * (seq_len // tt) + t, j)),
        out_shape=jax.ShapeDtypeStruct((n_seq * seq_len, HY_DIM), F32),
        scratch_shapes=[pltpu.VMEM((2 * seq_len, tn), BF16)],
        compiler_params=_params(("parallel", "parallel", "arbitrary"), 48),
        name=f"hyena_inv_{seq_len}_{order}",
    )(g_bf16, yspec, kspec, yin, u, bias.reshape(1, 2 * HY_DIM))


def _dft_matrices(seq_len):
    f = lax.broadcasted_iota(jnp.int32, (seq_len, seq_len), 0)
    s = lax.broadcasted_iota(jnp.int32, (seq_len, seq_len), 1)
    ang = (((2 * f + 1) * s) % (4 * seq_len)).astype(F32) * (math.pi / (2 * seq_len))
    c, sn = jnp.cos(ang), jnp.sin(ang)
    fwd = jnp.concatenate([c, sn], axis=0).astype(BF16)
    inv = jnp.concatenate([c.T, sn.T], axis=1).astype(BF16)
    return fwd, inv


def _hyena_filter_time(seq_len, w1, b1, w2, b2, w3, freq):
    hp = lax.Precision.HIGHEST
    t = jnp.linspace(0.0, 1.0, seq_len, dtype=F32)[:, None]
    bands = (HY_EMB - 1) // 2
    ang = 2.0 * math.pi * jnp.arange(seq_len, dtype=F32)[:, None] / seq_len
    fb = jnp.linspace(1e-4, bands - 1, bands, dtype=F32)[None, :]
    z = jnp.concatenate([t, jnp.cos(fb * ang), -jnp.sin(fb * ang)], axis=-1)
    h = jnp.sin(freq * (jnp.dot(z, w1, precision=hp) + b1))
    h = jnp.sin(freq * (jnp.dot(h, w2, precision=hp) + b2))
    h = jnp.dot(h, w3, precision=hp).reshape(seq_len, 2, 2, HY_DIM)
    deltas = jnp.linspace(math.log(HY_DECAY_TARGET) / HY_SLOW_DECAY,
                          math.log(HY_DECAY_TARGET) / HY_FAST_DECAY, HY_DIM, dtype=F32)
    h = h * jnp.exp(-t * jnp.abs(deltas))[:, None, None, :]
    return h * lax.rsqrt(jnp.sum(h * h, axis=(0, 1), keepdims=True))


def _filter_spec_kernel(a_ref, hf_ref, hb_ref, o_ref, *, seq_len, tm):
    sign = jnp.where(pl.program_id(1) * tm < seq_len, 1.0, -1.0)
    hb = hb_ref[...]
    hb = jnp.where(lax.broadcasted_iota(jnp.int32, hb.shape, 0) == 0, 0.0, hb)
    hi, lo = _split_bf16(hf_ref[...] + sign * hb)
    o_ref[...] = _dot(a_ref[...], hi) + _dot(a_ref[...], lo)


def _hyena_filter_spec(h, fwd_bf16, seq_len):
    h2d = h.reshape(seq_len, 4 * HY_DIM)
    tm = min(seq_len, 1024)
    tn = 512
    nb = 2 * HY_DIM // tn
    kern = functools.partial(_filter_spec_kernel, seq_len=seq_len, tm=tm)
    return pl.pallas_call(
        kern,
        grid=(nb, 2 * seq_len // tm),
        in_specs=[pl.BlockSpec((tm, seq_len), lambda j, i: (i, 0)),
                  pl.BlockSpec((seq_len, tn), lambda j, i: (0, j)),
                  pl.BlockSpec((seq_len, tn), lambda j, i: (0, nb + j))],
        out_specs=pl.BlockSpec((tm, tn), lambda j, i: (i, j)),
        out_shape=jax.ShapeDtypeStruct((2 * seq_len, 2 * HY_DIM), F32),
        compiler_params=_params(("parallel", "arbitrary"), 40),
        name=f"filter_spec_{seq_len}",
    )(fwd_bf16, h2d, h2d)


def _hyena(p, short_w, short_b, bias, h_time, n_seq, seq_len, row_block0):
    fwd, inv = _dft_matrices(seq_len)
    kspec = _hyena_filter_spec(h_time, fwd, seq_len)
    u = _shortconv(p, short_w, short_b, n_seq, seq_len, row_block0)
    yspec = _seqmm(fwd, u, n_seq, 2 * HY_DIM // 512, HY_DIM, f"hyena_fwd_{seq_len}_0")
    y1 = _hyena_inv(inv, yspec, kspec, 0, u, 2 * HY_DIM, u, 0, bias, n_seq, seq_len)
    yspec = _seqmm(fwd, y1, n_seq, 0, HY_DIM, f"hyena_fwd_{seq_len}_1")
    return _hyena_inv(inv, yspec, kspec, 1, y1, 0, u, HY_DIM, bias, n_seq, seq_len)


def _nt_dot(a, b):
    return lax.dot_general(a, b, (((1,), (1,)), ((), ())), preferred_element_type=F32)


def _ctx_attn_kernel(q_ref, k_ref, v_ref, o_ref, ko_ref, vo_ref):
    scale = HEAD_DIM ** -0.5
    ko_ref[...] = k_ref[...]
    vo_ref[...] = v_ref[...]
    for h in range(N_HEADS):
        sl = slice(h * HEAD_DIM, (h + 1) * HEAD_DIM)
        s = _nt_dot(q_ref[:, sl].astype(BF16), k_ref[:, sl].astype(BF16)) * scale
        e = jnp.exp(s - jnp.max(s, axis=-1, keepdims=True))
        pr = e / jnp.sum(e, axis=-1, keepdims=True)
        o_ref[:, sl] = _dot(pr.astype(BF16), v_ref[:, sl].astype(BF16))


def _ctx_attn(p):
    qb = 3 * HY_DIM // NA_DIM
    spec = lambda c: pl.BlockSpec((L_PROMPT, NA_DIM), lambda s: (s, c))
    out = jax.ShapeDtypeStruct((T_PROMPT, NA_DIM), F32)
    return pl.pallas_call(
        _ctx_attn_kernel,
        grid=(N_PROMPT_SEQ,),
        in_specs=[spec(qb), spec(qb + 1), spec(qb + 2)],
        out_specs=[spec(0), spec(0), spec(0)],
        out_shape=[out, out, out],
        compiler_params=_params(("parallel",), 40),
        name="ctx_attn",
    )(p, p, p)


def _na_kernel(q_ref, k_ref, v_ref, kc_ref, vc_ref, b_ref, o_ref):
    r = pl.program_id(1)
    scale = HEAD_DIM ** -0.5
    row_start = jnp.clip(r - WIN_R // 2, 0, GRID_ROWS - WIN_R)
    k0 = pl.multiple_of(row_start * GRID_W, GRID_W)
    n_loc = WIN_R * GRID_W
    for h in range(N_HEADS):
        sl = slice(h * HEAD_DIM, (h + 1) * HEAD_DIM)
        q = q_ref[:, sl].astype(BF16)
        s_loc = _nt_dot(q, k_ref[pl.ds(k0, n_loc), sl].astype(BF16)) * scale + b_ref[0, h]
        s_ctx = _nt_dot(q, kc_ref[0, :, sl].astype(BF16)) * scale
        m = jnp.maximum(jnp.max(s_loc, axis=-1, keepdims=True), jnp.max(s_ctx, axis=-1, keepdims=True))
        e_loc = jnp.exp(s_loc - m)
        e_ctx = jnp.exp(s_ctx - m)
        den = jnp.sum(e_loc, axis=-1, keepdims=True) + jnp.sum(e_ctx, axis=-1, keepdims=True)
        o_ref[:, sl] = (_dot((e_loc / den).astype(BF16), v_ref[pl.ds(k0, n_loc), sl].astype(BF16))
                        + _dot((e_ctx / den).astype(BF16), vc_ref[0, :, sl].astype(BF16)))


def _na_bias_table(rpb):
    q = np.arange(GRID_W)[:, None]
    kc = np.arange(GRID_W)[None, :]
    cs = np.clip(q - WIN_C // 2, 0, GRID_W - WIN_C)
    valid = (kc >= cs) & (kc < cs + WIN_C)
    col_off = np.clip(kc - q, -(WIN_C - 1), WIN_C - 1) + WIN_C - 1
    b = rpb.astype(F32)[:, :, col_off]
    b = jnp.where(jnp.asarray(valid)[None, None], b, NEG_BIG)
    rows = np.arange(WIN_R)[:, None] + np.arange(WIN_R)[None, :]
    b = b[:, rows]
    return b.transpose(1, 0, 3, 2, 4).reshape(WIN_R, N_HEADS, GRID_W, WIN_R * GRID_W)


def _na_attn(p, cache_k, cache_v, rpb):
    qb = 3 * HY_DIM // NA_DIM
    q_row0 = T_PROMPT // GRID_W
    seq0 = T_PROMPT // L_LATENT
    bias = _na_bias_table(rpb)

    def bias_idx(b, r):
        row_start = jnp.clip(r - WIN_R // 2, 0, GRID_ROWS - WIN_R)
        return (row_start - r + WIN_R - 1, 0, 0, 0)

    ctx_spec = pl.BlockSpec((1, cache_k.shape[1], NA_DIM), lambda b, r: (b, 0, 0))
    return pl.pallas_call(
        _na_kernel,
        grid=(N_LATENT_SEQ, GRID_ROWS),
        in_specs=[pl.BlockSpec((GRID_W, NA_DIM), lambda b, r: (q_row0 + b * GRID_ROWS + r, qb)),
                  pl.BlockSpec((L_LATENT, NA_DIM), lambda b, r: (seq0 + b, qb + 1)),
                  pl.BlockSpec((L_LATENT, NA_DIM), lambda b, r: (seq0 + b, qb + 2)),
                  ctx_spec, ctx_spec,
                  pl.BlockSpec((1, N_HEADS, GRID_W, WIN_R * GRID_W), bias_idx)],
        out_specs=pl.BlockSpec((GRID_W, NA_DIM), lambda b, r: (b * GRID_ROWS + r, 0)),
        out_shape=jax.ShapeDtypeStruct((T_LATENT, NA_DIM), F32),
        compiler_params=_params(("parallel", "arbitrary"), 52),
        name="na_attn",
    )(p, p, p, cache_k, cache_v, bias)


def _pool_kernel(prev_ref, cur_ref, next_ref, gw_ref, sc_ref, o_ref, ext_ref, *, tm):
    i = pl.program_id(0)
    row0 = i * tm
    seq_len = jnp.where(row0 < T_PROMPT, L_PROMPT, L_LATENT)
    pos0 = jnp.where(row0 < T_PROMPT, row0 % L_PROMPT, (row0 - T_PROMPT) % L_LATENT)
    first = pos0 == 0
    last = pos0 + tm == seq_len
    h = POOL_HALO
    ext_ref[0:h] = jnp.where(first, 0.0, prev_ref[...])
    ext_ref[h:h + tm] = cur_ref[...]
    ext_ref[h + tm:] = jnp.where(last, 0.0, next_ref[...])
    pos = pos0 + lax.broadcasted_iota(jnp.int32, (tm, 1), 0)
    for g, w in enumerate(POOL_WINDOWS):
        cols = slice(g * POOL_GROUP, (g + 1) * POOL_GROUP)
        acc = ext_ref[h - w // 2:h - w // 2 + tm, cols]
        for j in range(1, w):
            acc = acc + ext_ref[h - w // 2 + j:h - w // 2 + j + tm, cols]
        cnt = jnp.minimum(pos + w // 2, seq_len) - jnp.maximum(pos - w // 2, 0)
        d = acc / cnt.astype(F32) - cur_ref[:, cols]
        o_ref[:, cols] = _dot(d.astype(BF16), gw_ref[g]) * sc_ref[:, cols]


def _pool(u, group_w_bf16, scale):
    tm = 256
    hb = tm // POOL_HALO
    n_halo = T_ALL // POOL_HALO
    kern = functools.partial(_pool_kernel, tm=tm)
    return pl.pallas_call(
        kern,
        grid=(T_ALL // tm,),
        in_specs=[pl.BlockSpec((POOL_HALO, D), lambda i: (jnp.maximum(i * hb - 1, 0), 0)),
                  pl.BlockSpec((tm, D), lambda i: (i, 0)),
                  pl.BlockSpec((POOL_HALO, D), lambda i: (jnp.minimum((i + 1) * hb, n_halo - 1), 0)),
                  pl.BlockSpec((len(POOL_WINDOWS), POOL_GROUP, POOL_GROUP), lambda i: (0, 0, 0)),
                  pl.BlockSpec((1, D), lambda i: (0, 0))],
        out_specs=pl.BlockSpec((tm, D), lambda i: (i, 0)),
        out_shape=jax.ShapeDtypeStruct((T_ALL, D), F32),
        scratch_shapes=[pltpu.VMEM((tm + 2 * POOL_HALO, D), F32)],
        compiler_params=_params(("parallel",), 40),
        name="pool",
    )(u, u, u, group_w_bf16, scale.reshape(1, D))


def _post_mixer_kernel(a1_ref, a2_ref, w_ref, x_ref, gpost_ref, gate_ref, gpre_ref, sh_ref, sc_ref,
                       rwh_ref, rwl_ref, rb_ref, xo_ref, h_ref, lg_ref):
    k1 = a1_ref.shape[1]
    m = _dot(a1_ref[...].astype(BF16), w_ref[0:k1]) + _dot(a2_ref[...].astype(BF16), w_ref[k1:])
    xn = x_ref[...] + gate_ref[0] * (_rms(m) * gpost_ref[...])
    xo_ref[...] = xn
    h = _rms(xn) * gpre_ref[...] * (1.0 + sc_ref[0]) + sh_ref[0]
    hh, hl = _split_bf16(h)
    h_ref[...] = _pack_bf16_pairs(hh)
    lg_ref[...] = _dot(hh, rwh_ref[...]) + _dot(hl, rwh_ref[...]) + _dot(hh, rwl_ref[...]) + rb_ref[...]


def _post_mixer(a1, a1_col, a2, a2_col, w_bf16, x, g_post, gate, g_pre, shift, scale, rw_hi, rw_lo, rb, name):
    tm = 256
    kh = w_bf16.shape[0] // 2
    row = lambda i: (i, 0)
    const = lambda i: (0, 0)
    mod_spec = pl.BlockSpec((1, 1, D), lambda i: (_mod_row(i * tm), 0, 0))
    vec = pl.BlockSpec((1, D), const)
    return pl.pallas_call(
        _post_mixer_kernel,
        grid=(T_ALL // tm,),
        in_specs=[pl.BlockSpec((tm, kh), lambda i: (i, a1_col)),
                  pl.BlockSpec((tm, kh), lambda i: (i, a2_col)),
                  pl.BlockSpec((2 * kh, D), const),
                  pl.BlockSpec((tm, D), row), vec, mod_spec, vec, mod_spec, mod_spec,
                  pl.BlockSpec((D, LANES), const), pl.BlockSpec((D, LANES), const),
                  pl.BlockSpec((1, LANES), const)],
        out_specs=[pl.BlockSpec((tm, D), row), pl.BlockSpec((tm, D // 2), row), pl.BlockSpec((tm, LANES), row)],
        out_shape=[jax.ShapeDtypeStruct((T_ALL, D), F32), jax.ShapeDtypeStruct((T_ALL, D // 2), jnp.uint32),
                   jax.ShapeDtypeStruct((T_ALL, LANES), F32)],
        compiler_params=_params(("parallel",), 48),
        name=name,
    )(a1, a2, w_bf16, x, g_post, gate, g_pre, shift, scale, rw_hi, rw_lo, rb)


def _route_kernel(lg_ref, tri_ref, eidx_ref, gate_ref, rank_ref, cnt_ref, carry_ref):
    @pl.when(pl.program_id(0) == 0)
    def _():
        carry_ref[...] = jnp.zeros_like(carry_ref)

    shape = lg_ref.shape
    lane = lax.broadcasted_iota(jnp.int32, shape, 1).astype(F32)
    lg = jnp.where(lane < N_EXPERTS, lg_ref[...], -jnp.inf)
    multi = jnp.zeros(shape, F32)
    vals, sels = [], []
    eidx = jnp.zeros(shape, F32)
    for k in range(TOP_K):
        m = jnp.max(lg, axis=-1, keepdims=True)
        idx = jnp.min(jnp.where(lg == m, lane, float(LANES)), axis=-1, keepdims=True)
        sel = lane == idx
        multi = jnp.where(sel, 1.0, multi)
        lg = jnp.where(sel, -jnp.inf, lg)
        eidx = jnp.where(lane == k, idx, eidx)
        vals.append(m)
        sels.append(sel)
    exps = [jnp.exp(v - vals[0]) for v in vals]
    den = exps[0] + exps[1] + exps[2] + exps[3]
    rank_all = _dot(tri_ref[...], multi.astype(BF16)) + carry_ref[0:1]
    gate = jnp.zeros(shape, F32)
    rank = jnp.zeros(shape, F32)
    for k in range(TOP_K):
        gate = jnp.where(lane == k, exps[k] / den, gate)
        rank = jnp.where(lane == k, jnp.sum(jnp.where(sels[k], rank_all, 0.0), axis=-1, keepdims=True), rank)
    eidx_ref[...] = eidx.astype(jnp.int32)
    gate_ref[...] = gate
    rank_ref[...] = rank.astype(jnp.int32)
    carry_ref[0:1] = carry_ref[0:1] + jnp.sum(multi, axis=0, keepdims=True)
    cnt_ref[...] = carry_ref[...]


def _route(logits):
    tm = 256
    tri = (np.arange(tm)[:, None] > np.arange(tm)[None, :]).astype(np.float32)
    row = lambda i: (i, 0)
    spec = pl.BlockSpec((tm, LANES), row)
    return pl.pallas_call(
        _route_kernel,
        grid=(T_ALL // tm,),
        in_specs=[spec, pl.BlockSpec((tm, tm), lambda i: (0, 0))],
        out_specs=[spec, spec, spec, pl.BlockSpec((8, LANES), lambda i: (0, 0))],
        out_shape=[jax.ShapeDtypeStruct((T_ALL, LANES), jnp.int32), jax.ShapeDtypeStruct((T_ALL, LANES), F32),
                   jax.ShapeDtypeStruct((T_ALL, LANES), jnp.int32), jax.ShapeDtypeStruct((8, LANES), F32)],
        scratch_shapes=[pltpu.VMEM((8, LANES), F32)],
        compiler_params=_params(("arbitrary",), 32),
        name="route",
    )(logits, jnp.asarray(tri, BF16))


def _dest_kernel(eidx_ref, rank_ref, start_ref, o_ref):
    shape = eidx_ref.shape
    lane = lax.broadcasted_iota(jnp.int32, shape, 1).astype(F32)
    eidx = eidx_ref[...].astype(F32)
    starts = start_ref[...].astype(F32)
    dest = jnp.zeros(shape, F32)
    for k in range(TOP_K):
        e_k = jnp.sum(jnp.where(lane == k, eidx, 0.0), axis=-1, keepdims=True)
        start = jnp.sum(jnp.where(lane == e_k, starts, 0.0), axis=-1, keepdims=True)
        dest = jnp.where(lane == k, start, dest)
    o_ref[...] = jnp.where(lane < TOP_K, dest.astype(jnp.int32) + rank_ref[...], 0)


def _dest(eidx, rank, group_start):
    tm = 1024
    spec = pl.BlockSpec((tm, LANES), lambda i: (i, 0))
    return pl.pallas_call(
        _dest_kernel,
        grid=(T_ALL // tm,),
        in_specs=[spec, spec, pl.BlockSpec((1, LANES), lambda i: (0, 0))],
        out_specs=spec,
        out_shape=jax.ShapeDtypeStruct((T_ALL, LANES), jnp.int32),
        compiler_params=_params(("parallel",), 32),
        name="dest",
    )(eidx, rank, group_start)


def _dispatch_kernel(dest_ref, lo_ref, hi_ref, nu_ref, h_ref, o_ref, zero_ref, sem, zsem, *, tm):
    i = pl.program_id(0)
    base = i * tm * TOP_K

    @pl.when(i == 0)
    def _():
        zero_ref[...] = jnp.zeros_like(zero_ref)

    def issue(t, carry):
        for k in range(TOP_K):
            d = dest_ref[base + t * TOP_K + k]
            pltpu.make_async_copy(h_ref.at[pl.ds(t, 1)], o_ref.at[pl.ds(d, 1)], sem).start()
        return carry

    lax.fori_loop(0, tm, issue, 0)

    @pl.when(i < N_EXPERTS)
    def _():
        zrow = lambda r: pltpu.make_async_copy(zero_ref.at[pl.ds(0, 1)], o_ref.at[pl.ds(r, 1)], zsem)
        lax.fori_loop(lo_ref[i], hi_ref[i], lambda r, c: (zrow(r).start(), c)[1], 0)
        lax.fori_loop(lo_ref[i], hi_ref[i], lambda r, c: (zrow(r).wait(), c)[1], 0)

    blk = nu_ref[0] + i - N_EXPERTS

    @pl.when((i >= N_EXPERTS) & (blk < MOE_BLOCKS))
    def _():
        r0 = pl.multiple_of(blk * MOE_TM, MOE_TM)
        cp = pltpu.make_async_copy(zero_ref, o_ref.at[pl.ds(r0, MOE_TM)], zsem)
        cp.start()
        cp.wait()

    for k in range(TOP_K):
        pltpu.make_async_copy(h_ref, o_ref.at[pl.ds(0, tm)], sem).wait()


def _dispatch(dest_flat, pad_lo, pad_hi, n_used, h):
    tm = 128
    assert T_ALL // tm >= 2 * N_EXPERTS
    kern = functools.partial(_dispatch_kernel, tm=tm)
    return pl.pallas_call(
        kern,
        grid_spec=pltpu.PrefetchScalarGridSpec(
            num_scalar_prefetch=4,
            grid=(T_ALL // tm,),
            in_specs=[pl.BlockSpec((tm, D // 2), lambda i, *_: (i, 0))],
            out_specs=pl.BlockSpec(memory_space=pl.ANY),
            scratch_shapes=[pltpu.VMEM((MOE_TM, D // 2), jnp.uint32), pltpu.SemaphoreType.DMA,
                            pltpu.SemaphoreType.DMA]),
        out_shape=jax.ShapeDtypeStruct((MOE_ROWS, D // 2), jnp.uint32),
        compiler_params=_params(("arbitrary",), 32),
        name="dispatch",
    )(dest_flat, pad_lo, pad_hi, n_used, h)


def _expert_up_kernel(be_ref, nu_ref, x_ref, w_ref, b_ref, o_ref, wbf_ref):
    i = pl.program_id(1)
    tn = o_ref.shape[1]

    @pl.when(i < nu_ref[0])
    def _():
        @pl.when((i == 0) | (be_ref[i] != be_ref[jnp.maximum(i - 1, 0)]))
        def _():
            wbf_ref[...] = w_ref[0, 0].astype(BF16)

        x_hi, x_lo = _unpack_bf16_pairs(x_ref[...])
        hb = _dot(x_hi, wbf_ref[0:D // 2]) + _dot(x_lo, wbf_ref[D // 2:]) + b_ref[0, 0]
        first, second = hb[:, :tn], hb[:, tn:]
        even = lax.broadcasted_iota(jnp.int32, first.shape, 1) % 2 == 0
        g = jnp.where(even, first, pltpu.roll(second, 1, 1))
        lin = jnp.where(even, pltpu.roll(first, tn - 1, 1), second)
        g = jnp.minimum(g, SWIGLU_LIMIT)
        lin = jnp.clip(lin, -SWIGLU_LIMIT, SWIGLU_LIMIT)
        o_ref[...] = (g * jax.nn.sigmoid(SWIGLU_ALPHA * g) * (lin + 1.0)).astype(BF16)

    @pl.when(i >= nu_ref[0])
    def _():
        o_ref[...] = jnp.zeros_like(o_ref)


def _expert_up(blk_e, n_used, xs, w_gu, b_gu, layer):
    tn = FF_TILE
    blk = lambda j, i, be, nu: (jnp.minimum(i, nu[0] - 1), 0)
    return pl.pallas_call(
        _expert_up_kernel,
        grid_spec=pltpu.PrefetchScalarGridSpec(
            num_scalar_prefetch=2,
            grid=(D_FF // tn, MOE_BLOCKS),
            in_specs=[pl.BlockSpec((MOE_TM, D // 2), blk),
                      pl.BlockSpec((1, 1, D, 2 * tn), lambda j, i, be, nu: (layer, be[i], 0, j)),
                      pl.BlockSpec((1, 1, 1, 2 * tn), lambda j, i, be, nu: (layer, be[i], 0, j))],
            out_specs=pl.BlockSpec((MOE_TM, tn), lambda j, i, be, nu: (i, j)),
            scratch_shapes=[pltpu.VMEM((D, 2 * tn), BF16)]),
        out_shape=jax.ShapeDtypeStruct((MOE_ROWS, D_FF), BF16),
        compiler_params=_params(("arbitrary", "arbitrary"), 56),
        name="expert_up",
    )(blk_e, n_used, xs, w_gu, b_gu.reshape(N_LAYERS, N_EXPERTS, 1, 2 * D_FF))


def _ff_order_rows(w_down):
    e, _, n = w_down.shape
    w = w_down.reshape(e, D_FF // FF_TILE, 2, FF_TILE // 2, n)
    return w.transpose(0, 1, 3, 2, 4).reshape(e, D_FF, n)


def _expert_down_kernel(be_ref, nu_ref, a_ref, w_ref, b_ref, o_ref):
    @pl.when(pl.program_id(0) < nu_ref[0])
    def _():
        o_ref[...] = _dot(a_ref[...], w_ref[0]) + b_ref[0]

    @pl.when(pl.program_id(0) >= nu_ref[0])
    def _():
        o_ref[...] = jnp.zeros_like(o_ref)


def _expert_down(blk_e, n_used, a, wd, bd):
    blk = lambda i, be, nu: (jnp.minimum(i, nu[0] - 1), 0)
    return pl.pallas_call(
        _expert_down_kernel,
        grid_spec=pltpu.PrefetchScalarGridSpec(
            num_scalar_prefetch=2,
            grid=(MOE_BLOCKS,),
            in_specs=[pl.BlockSpec((MOE_TM, D_FF), blk),
                      pl.BlockSpec((1, D_FF, D), lambda i, be, nu: (be[i], 0, 0)),
                      pl.BlockSpec((1, 1, D), lambda i, be, nu: (be[i], 0, 0))],
            out_specs=pl.BlockSpec((MOE_TM, D), lambda i, be, nu: (i, 0))),
        out_shape=jax.ShapeDtypeStruct((MOE_ROWS, D), F32),
        compiler_params=_params(("arbitrary",), 48),
        name="expert_down",
    )(blk_e, n_used, a, wd, bd)


def _combine_kernel(dest_ref, y_ref, gate_ref, x_ref, gpost_ref, g2_ref, o_ref, buf_ref, sem, *, tm):
    base = pl.program_id(0) * tm * TOP_K

    def issue(t, carry):
        for k in range(TOP_K):
            d = dest_ref[base + t * TOP_K + k]
            pltpu.make_async_copy(y_ref.at[pl.ds(d, 1)], buf_ref.at[k, pl.ds(t, 1)], sem).start()
        return carry

    lax.fori_loop(0, tm, issue, 0)
    for k in range(TOP_K):
        pltpu.make_async_copy(y_ref.at[pl.ds(0, tm)], buf_ref.at[k], sem).wait()
    f = gate_ref[:, 0:1] * buf_ref[0]
    for k in range(1, TOP_K):
        f = f + gate_ref[:, k:k + 1] * buf_ref[k]
    o_ref[...] = x_ref[...] + g2_ref[0] * (_rms(f) * gpost_ref[...])


def _combine(dest_flat, ys, gate, x, g_post, gate2, name):
    tm = 128
    kern = functools.partial(_combine_kernel, tm=tm)
    return pl.pallas_call(
        kern,
        grid_spec=pltpu.PrefetchScalarGridSpec(
            num_scalar_prefetch=1,
            grid=(T_ALL // tm,),
            in_specs=[pl.BlockSpec(memory_space=pl.ANY),
                      pl.BlockSpec((tm, LANES), lambda i, d: (i, 0)),
                      pl.BlockSpec((tm, D), lambda i, d: (i, 0)),
                      pl.BlockSpec((1, D), lambda i, d: (0, 0)),
                      pl.BlockSpec((1, 1, D), lambda i, d: (_mod_row(i * tm), 0, 0))],
            out_specs=pl.BlockSpec((tm, D), lambda i, d: (i, 0)),
            scratch_shapes=[pltpu.VMEM((TOP_K, tm, D), F32), pltpu.SemaphoreType.DMA]),
        out_shape=jax.ShapeDtypeStruct((T_ALL, D), F32),
        compiler_params=_params(("arbitrary",), 32),
        name=name,
    )(dest_flat, ys, gate, x, g_post, gate2)


def _moe(h, logits, x, g_post, gate2, w_gu, b_gu, wd, bd, layer):
    eidx, gate, rank, counts = _route(logits)
    cnt = counts[0, :N_EXPERTS].astype(jnp.int32)
    padded = (cnt + MOE_TM - 1) // MOE_TM * MOE_TM
    group_end = jnp.cumsum(padded)
    group_start = jnp.zeros((1, LANES), jnp.int32).at[0, :N_EXPERTS].set(group_end - padded)
    n_used = group_end[-1] // MOE_TM
    blk_row0 = jnp.arange(MOE_BLOCKS, dtype=jnp.int32) * MOE_TM
    blk_row0 = jnp.minimum(blk_row0, (n_used - 1) * MOE_TM)
    blk_e = jnp.sum(group_end[None, :] <= blk_row0[:, None], axis=1).astype(jnp.int32)
    n_used = n_used.astype(jnp.int32).reshape(1)
    dest = _dest(eidx, rank, group_start)[:, :TOP_K].reshape(N_ASSIGN)
    xs = _dispatch(dest, group_end - padded + cnt, group_end, n_used, h)
    a = _expert_up(blk_e, n_used, xs, w_gu, b_gu, layer)
    ys = _expert_down(blk_e, n_used, a, wd, bd)
    return _combine(dest, ys, gate, x, g_post, gate2, f"combine_{layer}")


def kernel(x_prompt, x_sample, cache_k, cache_v, c, c_ctx, ada_w, ada_b, norm_mix_pre, norm_mix_post,
           norm_ffn_pre, norm_ffn_post, ab_in_w, ab_out_w, hy_short_w, hy_short_b, hy_w1, hy_b1, hy_w2,
           hy_b2, hy_w3, hy_freq, hy_bias, na_rpb, pool_in_w, pool_group_w, pool_scale, pool_out_w,
           router_w, router_b, exp_w_gu, exp_b_gu, exp_w_down, exp_b_down):
    x = jnp.concatenate([x_prompt.reshape(T_PROMPT, D), x_sample.reshape(T_LATENT, D)], axis=0)
    cvec = jnp.concatenate([c_ctx[None], c, jnp.zeros((8 - 1 - N_LATENT_SEQ, D), F32)], axis=0)
    mods = _adaln(cvec, ada_w, ada_b).reshape(N_LAYERS, 8, 6, D)

    new_k = new_v = None
    for layer in range(N_LAYERS):
        sh1, sc1, g1, sh2, sc2, g2 = [mods[layer, :, k][:, None, :] for k in range(6)]
        row = lambda a: a[layer].reshape(1, D)
        if layer % 2 == 0:
            j = layer // 2
            p = _modmm(x, row(norm_mix_pre), sh1, sc1, ab_in_w[j].astype(BF16), "ab_in")
            filt = (hy_w1[j], hy_b1[j], hy_w2[j], hy_b2[j], hy_w3[j], hy_freq[j])
            hy = (hy_short_w[j], hy_short_b[j], hy_bias[j])
            yh_p = _hyena(p, *hy, _hyena_filter_time(L_PROMPT, *filt), N_PROMPT_SEQ, L_PROMPT, 0)
            yh_s = _hyena(p, *hy, _hyena_filter_time(L_LATENT, *filt), N_LATENT_SEQ, L_LATENT,
                          T_PROMPT // L_LATENT)
            ya_p, new_k, new_v = _ctx_attn(p)
            ck = cache_k[:, j].reshape(N_LATENT_SEQ, -1, NA_DIM)
            cv = cache_v[:, j].reshape(N_LATENT_SEQ, -1, NA_DIM)
            ya_s = _na_attn(p, ck, cv, na_rpb[j])
            a1 = jnp.concatenate([yh_p, yh_s], axis=0)
            a2 = jnp.concatenate([ya_p, ya_s], axis=0)
            a1_col = a2_col = 0
            w_out = ab_out_w[j].astype(BF16)
        else:
            j = layer // 2
            u = _modmm(x, row(norm_mix_pre), sh1, sc1, pool_in_w[j].astype(BF16), "pool_in")
            a1 = a2 = _pool(u, pool_group_w[j].astype(BF16), pool_scale[j])
            a1_col, a2_col = 0, 1
            w_out = pool_out_w[j].astype(BF16)
        rw = jnp.zeros((D, LANES), F32).at[:, :N_EXPERTS].set(router_w[layer])
        rw_hi, rw_lo = _split_bf16(rw)
        rb = jnp.zeros((1, LANES), F32).at[0, :N_EXPERTS].set(router_b[layer])
        x, h, logits = _post_mixer(a1, a1_col, a2, a2_col, w_out, x, row(norm_mix_post), g1,
                                   row(norm_ffn_pre), sh2, sc2, rw_hi, rw_lo, rb, f"post_mixer_{layer}")
        wd = _ff_order_rows(exp_w_down[layer]).astype(BF16)
        bd = exp_b_down[layer].reshape(N_EXPERTS, 1, D)
        x = _moe(h, logits, x, row(norm_ffn_post), g2, exp_w_gu, exp_b_gu, wd, bd, layer)

    y_prompt = x[:T_PROMPT].reshape(x_prompt.shape)
    y_sample = x[T_PROMPT:].reshape(x_sample.shape)
    kv_shape = (N_PROMPT_SEQ, 1, L_PROMPT, N_HEADS, HEAD_DIM)
    return y_prompt, y_sample, new_k.reshape(kv_shape), new_v.reshape(kv_shape)
```

```python
import functools
import math

import jax
import jax.numpy as jnp
import numpy as np
from jax import lax
from jax.experimental import pallas as pl
from jax.experimental.pallas import tpu as pltpu

F32 = jnp.float32
BF16 = jnp.bfloat16

D = 2048
N_PROMPT_SEQ = 32
L_PROMPT = 256
N_LATENT_SEQ = 4
L_LATENT = 2048
T_PROMPT = N_PROMPT_SEQ * L_PROMPT
T_LATENT = N_LATENT_SEQ * L_LATENT
T_ALL = T_PROMPT + T_LATENT
N_LAYERS = 2
GRID_W = 64
GRID_ROWS = L_LATENT // GRID_W
HY_DIM = 1024
HY_EMB = 33
HY_DECAY_TARGET = 1e-2
HY_FAST_DECAY = 0.3
HY_SLOW_DECAY = 1.5
N_HEADS = 16
HEAD_DIM = 64
NA_DIM = N_HEADS * HEAD_DIM
WIN_R = 8
WIN_C = 16
AB_IN = 3 * HY_DIM + 3 * NA_DIM
POOL_WINDOWS = (2, 4, 8, 16)
POOL_GROUP = D // len(POOL_WINDOWS)
POOL_HALO = 8
N_EXPERTS = 32
TOP_K = 4
D_FF = D
SWIGLU_LIMIT = 7.0
SWIGLU_ALPHA = 1.702
RMS_EPS = 1e-6
NEG_BIG = -1e30

LANES = 128
MOE_TM = 256
FF_TILE = 1024
N_ASSIGN = T_ALL * TOP_K
MOE_BLOCKS = -(-(N_ASSIGN + N_EXPERTS * (MOE_TM - 1)) // MOE_TM)
MOE_ROWS = MOE_BLOCKS * MOE_TM
MIB = 1 << 20


def _params(semantics, vmem_mib):
    return pltpu.CompilerParams(dimension_semantics=semantics, vmem_limit_bytes=vmem_mib * MIB)


def _mod_row(row0):
    return jnp.where(row0 < T_PROMPT, 0, 1 + (row0 - T_PROMPT) // L_LATENT)


def _rms(x):
    return x * lax.rsqrt(jnp.mean(x * x, axis=-1, keepdims=True) + RMS_EPS)


def _split_bf16(x):
    hi = x.astype(BF16)
    lo = (x - hi.astype(F32)).astype(BF16)
    return hi, lo


def _dot(a, b):
    return jnp.dot(a, b, preferred_element_type=F32)


def _pack_bf16_pairs(x_bf16):
    half = x_bf16.shape[1] // 2
    bits = lax.bitcast_convert_type(x_bf16.astype(F32), jnp.uint32)
    return (bits[:, :half] & jnp.uint32(0xFFFF0000)) | (bits[:, half:] >> 16)


def _unpack_bf16_pairs(packed):
    hi = lax.bitcast_convert_type(packed & jnp.uint32(0xFFFF0000), F32).astype(BF16)
    lo = lax.bitcast_convert_type(packed << 16, F32).astype(BF16)
    return hi, lo


def _adaln_kernel(cv_ref, w_ref, b_ref, o_ref):
    s = jax.nn.silu(cv_ref[...]).astype(BF16)
    o_ref[0] = _dot(s, w_ref[0].astype(BF16)) + b_ref[0]


def _adaln(cvec, ada_w, ada_b):
    n = ada_w.shape[-1]
    tn = 1024
    return pl.pallas_call(
        _adaln_kernel,
        grid=(N_LAYERS, n // tn),
        in_specs=[pl.BlockSpec((8, D), lambda l, j: (0, 0)),
                  pl.BlockSpec((1, D, tn), lambda l, j: (l, 0, j)),
                  pl.BlockSpec((1, 1, tn), lambda l, j: (l, 0, j))],
        out_specs=pl.BlockSpec((1, 8, tn), lambda l, j: (l, 0, j)),
        out_shape=jax.ShapeDtypeStruct((N_LAYERS, 8, n), F32),
        compiler_params=_params(("parallel", "arbitrary"), 40),
        name="adaln",
    )(cvec, ada_w, ada_b.reshape(N_LAYERS, 1, n))


def _modmm_kernel(x_ref, g_ref, sh_ref, sc_ref, w_ref, o_ref, h_ref):
    @pl.when(pl.program_id(1) == 0)
    def _():
        h = _rms(x_ref[...]) * g_ref[...] * (1.0 + sc_ref[0]) + sh_ref[0]
        h_ref[...] = h.astype(BF16)

    o_ref[...] = _dot(h_ref[...], w_ref[...])


def _modmm(x, g, shift, scale, w_bf16, name):
    n = w_bf16.shape[1]
    tm, tn = 512, 1024
    mod_spec = pl.BlockSpec((1, 1, D), lambda i, j: (_mod_row(i * tm), 0, 0))
    return pl.pallas_call(
        _modmm_kernel,
        grid=(T_ALL // tm, n // tn),
        in_specs=[pl.BlockSpec((tm, D), lambda i, j: (i, 0)),
                  pl.BlockSpec((1, D), lambda i, j: (0, 0)),
                  mod_spec, mod_spec,
                  pl.BlockSpec((D, tn), lambda i, j: (0, j))],
        out_specs=pl.BlockSpec((tm, tn), lambda i, j: (i, j)),
        out_shape=jax.ShapeDtypeStruct((T_ALL, n), F32),
        scratch_shapes=[pltpu.VMEM((tm, D), BF16)],
        compiler_params=_params(("parallel", "arbitrary"), 40),
        name=name,
    )(x, g, shift, scale, w_bf16)


def _shortconv_kernel(z_ref, w_ref, b_ref, o_ref):
    z = z_ref[...]
    n = z.shape[0]
    row = lax.broadcasted_iota(jnp.int32, z.shape, 0)
    prev = jnp.where(row == 0, 0.0, pltpu.roll(z, 1, 0))
    nxt = jnp.where(row == n - 1, 0.0, pltpu.roll(z, n - 1, 0))
    o_ref[...] = w_ref[0:1] * prev + w_ref[1:2] * z + w_ref[2:3] * nxt + b_ref[...]


def _shortconv(p, w, b, n_seq, seq_len, row_block0):
    c = 3 * HY_DIM
    tc = 512
    return pl.pallas_call(
        _shortconv_kernel,
        grid=(n_seq, c // tc),
        in_specs=[pl.BlockSpec((seq_len, tc), lambda s, j: (row_block0 + s, j)),
                  pl.BlockSpec((3, tc), lambda s, j: (0, j)),
                  pl.BlockSpec((1, tc), lambda s, j: (0, j))],
        out_specs=pl.BlockSpec((seq_len, tc), lambda s, j: (s, j)),
        out_shape=jax.ShapeDtypeStruct((n_seq * seq_len, c), F32),
        compiler_params=_params(("parallel", "parallel"), 40),
        name=f"shortconv_{seq_len}",
    )(p, w, b.reshape(1, c))


def _seqmm_kernel(a_ref, x_ref, o_ref):
    o_ref[...] = _dot(a_ref[...], x_ref[...].astype(BF16))


def _seqmm(a_bf16, x, n_seq, col_block0, n_cols, name):
    m, k = a_bf16.shape
    tm = min(m, 1024)
    tn = 512
    return pl.pallas_call(
        _seqmm_kernel,
        grid=(n_seq, n_cols // tn, m // tm),
        in_specs=[pl.BlockSpec((tm, k), lambda s, j, i: (i, 0)),
                  pl.BlockSpec((k, tn), lambda s, j, i: (s, col_block0 + j))],
        out_specs=pl.BlockSpec((tm, tn), lambda s, j, i: (s * (m // tm) + i, j)),
        out_shape=jax.ShapeDtypeStruct((n_seq * m, n_cols), F32),
        compiler_params=_params(("parallel", "parallel", "arbitrary"), 40),
        name=name,
    )(a_bf16, x)


def _hyena_inv_kernel(g_ref, y_ref, k_ref, yin_ref, gate_ref, bias_ref, o_ref, z_ref, *, seq_len):
    @pl.when(pl.program_id(2) == 0)
    def _():
        yc, ys = y_ref[0:seq_len], y_ref[seq_len:]
        kc, ks = k_ref[0:seq_len], k_ref[seq_len:]
        z_ref[0:seq_len] = (yc * kc - ys * ks).astype(BF16)
        z_ref[seq_len:] = (yc * ks + ys * kc).astype(BF16)

    conv = _dot(g_ref[...], z_ref[...]) * (1.0 / seq_len)
    o_ref[...] = gate_ref[...] * (conv + yin_ref[...] * bias_ref[...])


def _hyena_inv(g_bf16, yspec, kspec, order, yin, yin_col0, u, gate_col0, bias, n_seq, seq_len):
    tn = 256
    tt = min(seq_len, 512)
    nb = HY_DIM // tn
    kern = functools.partial(_hyena_inv_kernel, seq_len=seq_len)
    return pl.pallas_call(
        kern,
        grid=(n_seq, nb, seq_len // tt),
        in_specs=[pl.BlockSpec((tt, 2 * seq_len), lambda s, j, t: (t, 0)),
                  pl.BlockSpec((2 * seq_len, tn), lambda s, j, t: (s, j)),
                  pl.BlockSpec((2 * seq_len, tn), lambda s, j, t: (0, order * nb + j)),
                  pl.BlockSpec((tt, tn), lambda s, j, t: (s * (seq_len // tt) + t, yin_col0 // tn + j)),
                  pl.BlockSpec((tt, tn), lambda s, j, t: (s * (seq_len // tt) + t, gate_col0 // tn + j)),
                  pl.BlockSpec((1, tn), lambda s, j, t: (0, order * nb + j))],
        out_specs=pl.BlockSpec((tt, tn), lambda s, j, t: (s * (seq_len // tt) + t, j)),
        out_shape=jax.ShapeDtypeStruct((n_seq * seq_len, HY_DIM), F32),
        scratch_shapes=[pltpu.VMEM((2 * seq_len, tn), BF16)],
        compiler_params=_params(("parallel", "parallel", "arbitrary"), 48),
        name=f"hyena_inv_{seq_len}_{order}",
    )(g_bf16, yspec, kspec, yin, u, bias.reshape(1, 2 * HY_DIM))


def _dft_matrices(seq_len):
    f = lax.broadcasted_iota(jnp.int32, (seq_len, seq_len), 0)
    s = lax.broadcasted_iota(jnp.int32, (seq_len, seq_len), 1)
    ang = (((2 * f + 1) * s) % (4 * seq_len)).astype(F32) * (math.pi / (2 * seq_len))
    c, sn = jnp.cos(ang), jnp.sin(ang)
    fwd = jnp.concatenate([c, sn], axis=0).astype(BF16)
    inv = jnp.concatenate([c.T, sn.T], axis=1).astype(BF16)
    return fwd, inv


def _hyena_filter_time(seq_len, w1, b1, w2, b2, w3, freq):
    hp = lax.Precision.HIGHEST
    t = jnp.linspace(0.0, 1.0, seq_len, dtype=F32)[:, None]
    bands = (HY_EMB - 1) // 2
    ang = 2.0 * math.pi * jnp.arange(seq_len, dtype=F32)[:, None] / seq_len
    fb = jnp.linspace(1e-4, bands - 1, bands, dtype=F32)[None, :]
    z = jnp.concatenate([t, jnp.cos(fb * ang), -jnp.sin(fb * ang)], axis=-1)
    h = jnp.sin(freq * (jnp.dot(z, w1, precision=hp) + b1))
    h = jnp.sin(freq * (jnp.dot(h, w2, precision=hp) + b2))
    h = jnp.dot(h, w3, precision=hp).reshape(seq_len, 2, 2, HY_DIM)
    deltas = jnp.linspace(math.log(HY_DECAY_TARGET) / HY_SLOW_DECAY,
                          math.log(HY_DECAY_TARGET) / HY_FAST_DECAY, HY_DIM, dtype=F32)
    h = h * jnp.exp(-t * jnp.abs(deltas))[:, None, None, :]
    return h * lax.rsqrt(jnp.sum(h * h, axis=(0, 1), keepdims=True))


def _filter_spec_kernel(a_ref, hf_ref, hb_ref, o_ref, *, seq_len, tm):
    sign = jnp.where(pl.program_id(1) * tm < seq_len, 1.0, -1.0)
    hb = hb_ref[...]
    hb = jnp.where(lax.broadcasted_iota(jnp.int32, hb.shape, 0) == 0, 0.0, hb)
    hi, lo = _split_bf16(hf_ref[...] + sign * hb)
    o_ref[...] = _dot(a_ref[...], hi) + _dot(a_ref[...], lo)


def _hyena_filter_spec(h, fwd_bf16, seq_len):
    h2d = h.reshape(seq_len, 4 * HY_DIM)
    tm = min(seq_len, 1024)
    tn = 512
    nb = 2 * HY_DIM // tn
    kern = functools.partial(_filter_spec_kernel, seq_len=seq_len, tm=tm)
    return pl.pallas_call(
        kern,
        grid=(nb, 2 * seq_len // tm),
        in_specs=[pl.BlockSpec((tm, seq_len), lambda j, i: (i, 0)),
                  pl.BlockSpec((seq_len, tn), lambda j, i: (0, j)),
                  pl.BlockSpec((seq_len, tn), lambda j, i: (0, nb + j))],
        out_specs=pl.BlockSpec((tm, tn), lambda j, i: (i, j)),
        out_shape=jax.ShapeDtypeStruct((2 * seq_len, 2 * HY_DIM), F32),
        compiler_params=_params(("parallel", "arbitrary"), 40),
        name=f"filter_spec_{seq_len}",
    )(fwd_bf16, h2d, h2d)


def _hyena(p, short_w, short_b, bias, h_time, n_seq, seq_len, row_block0):
    fwd, inv = _dft_matrices(seq_len)
    kspec = _hyena_filter_spec(h_time, fwd, seq_len)
    u = _shortconv(p, short_w, short_b, n_seq, seq_len, row_block0)
    yspec = _seqmm(fwd, u, n_seq, 2 * HY_DIM // 512, HY_DIM, f"hyena_fwd_{seq_len}_0")
    y1 = _hyena_inv(inv, yspec, kspec, 0, u, 2 * HY_DIM, u, 0, bias, n_seq, seq_len)
    yspec = _seqmm(fwd, y1, n_seq, 0, HY_DIM, f"hyena_fwd_{seq_len}_1")
    return _hyena_inv(inv, yspec, kspec, 1, y1, 0, u, HY_DIM, bias, n_seq, seq_len)


def _nt_dot(a, b):
    return lax.dot_general(a, b, (((1,), (1,)), ((), ())), preferred_element_type=F32)


def _ctx_attn_kernel(q_ref, k_ref, v_ref, o_ref, ko_ref, vo_ref):
    scale = HEAD_DIM ** -0.5
    ko_ref[...] = k_ref[...]
    vo_ref[...] = v_ref[...]
    for h in range(N_HEADS):
        sl = slice(h * HEAD_DIM, (h + 1) * HEAD_DIM)
        s = _nt_dot(q_ref[:, sl].astype(BF16), k_ref[:, sl].astype(BF16)) * scale
        e = jnp.exp(s - jnp.max(s, axis=-1, keepdims=True))
        pr = e / jnp.sum(e, axis=-1, keepdims=True)
        o_ref[:, sl] = _dot(pr.astype(BF16), v_ref[:, sl].astype(BF16))


def _ctx_attn(p):
    qb = 3 * HY_DIM // NA_DIM
    spec = lambda c: pl.BlockSpec((L_PROMPT, NA_DIM), lambda s: (s, c))
    out = jax.ShapeDtypeStruct((T_PROMPT, NA_DIM), F32)
    return pl.pallas_call(
        _ctx_attn_kernel,
        grid=(N_PROMPT_SEQ,),
        in_specs=[spec(qb), spec(qb + 1), spec(qb + 2)],
        out_specs=[spec(0), spec(0), spec(0)],
        out_shape=[out, out, out],
        compiler_params=_params(("parallel",), 40),
        name="ctx_attn",
    )(p, p, p)


def _na_kernel(q_ref, k_ref, v_ref, kc_ref, vc_ref, b_ref, o_ref):
    r = pl.program_id(1)
    scale = HEAD_DIM ** -0.5
    row_start = jnp.clip(r - WIN_R // 2, 0, GRID_ROWS - WIN_R)
    k0 = pl.multiple_of(row_start * GRID_W, GRID_W)
    n_loc = WIN_R * GRID_W
    for h in range(N_HEADS):
        sl = slice(h * HEAD_DIM, (h + 1) * HEAD_DIM)
        q = q_ref[:, sl].astype(BF16)
        s_loc = _nt_dot(q, k_ref[pl.ds(k0, n_loc), sl].astype(BF16)) * scale + b_ref[0, h]
        s_ctx = _nt_dot(q, kc_ref[0, :, sl].astype(BF16)) * scale
        m = jnp.maximum(jnp.max(s_loc, axis=-1, keepdims=True), jnp.max(s_ctx, axis=-1, keepdims=True))
        e_loc = jnp.exp(s_loc - m)
        e_ctx = jnp.exp(s_ctx - m)
        den = jnp.sum(e_loc, axis=-1, keepdims=True) + jnp.sum(e_ctx, axis=-1, keepdims=True)
        o_ref[:, sl] = (_dot((e_loc / den).astype(BF16), v_ref[pl.ds(k0, n_loc), sl].astype(BF16))
                        + _dot((e_ctx / den).astype(BF16), vc_ref[0, :, sl].astype(BF16)))


def _na_bias_table(rpb):
    q = np.arange(GRID_W)[:, None]
    kc = np.arange(GRID_W)[None, :]
    cs = np.clip(q - WIN_C // 2, 0, GRID_W - WIN_C)
    valid = (kc >= cs) & (kc < cs + WIN_C)
    col_off = np.clip(kc - q, -(WIN_C - 1), WIN_C - 1) + WIN_C - 1
    b = rpb.astype(F32)[:, :, col_off]
    b = jnp.where(jnp.asarray(valid)[None, None], b, NEG_BIG)
    rows = np.arange(WIN_R)[:, None] + np.arange(WIN_R)[None, :]
    b = b[:, rows]
    return b.transpose(1, 0, 3, 2, 4).reshape(WIN_R, N_HEADS, GRID_W, WIN_R * GRID_W)


def _na_attn(p, cache_k, cache_v, rpb):
    qb = 3 * HY_DIM // NA_DIM
    q_row0 = T_PROMPT // GRID_W
    seq0 = T_PROMPT // L_LATENT
    bias = _na_bias_table(rpb)

    def bias_idx(b, r):
        row_start = jnp.clip(r - WIN_R // 2, 0, GRID_ROWS - WIN_R)
        return (row_start - r + WIN_R - 1, 0, 0, 0)

    ctx_spec = pl.BlockSpec((1, cache_k.shape[1], NA_DIM), lambda b, r: (b, 0, 0))
    return pl.pallas_call(
        _na_kernel,
        grid=(N_LATENT_SEQ, GRID_ROWS),
        in_specs=[pl.BlockSpec((GRID_W, NA_DIM), lambda b, r: (q_row0 + b * GRID_ROWS + r, qb)),
                  pl.BlockSpec((L_LATENT, NA_DIM), lambda b, r: (seq0 + b, qb + 1)),
                  pl.BlockSpec((L_LATENT, NA_DIM), lambda b, r: (seq0 + b, qb + 2)),
                  ctx_spec, ctx_spec,
                  pl.BlockSpec((1, N_HEADS, GRID_W, WIN_R * GRID_W), bias_idx)],
        out_specs=pl.BlockSpec((GRID_W, NA_DIM), lambda b, r: (b * GRID_ROWS + r, 0)),
        out_shape=jax.ShapeDtypeStruct((T_LATENT, NA_DIM), F32),
        compiler_params=_params(("parallel", "arbitrary"), 52),
        name="na_attn",
    )(p, p, p, cache_k, cache_v, bias)


def _pool_kernel(prev_ref, cur_ref, next_ref, gw_ref, sc_ref, o_ref, ext_ref, *, tm):
    i = pl.program_id(0)
    row0 = i * tm
    seq_len = jnp.where(row0 < T_PROMPT, L_PROMPT, L_LATENT)
    pos0 = jnp.where(row0 < T_PROMPT, row0 % L_PROMPT, (row0 - T_PROMPT) % L_LATENT)
    first = pos0 == 0
    last = pos0 + tm == seq_len
    h = POOL_HALO
    ext_ref[0:h] = jnp.where(first, 0.0, prev_ref[...])
    ext_ref[h:h + tm] = cur_ref[...]
    ext_ref[h + tm:] = jnp.where(last, 0.0, next_ref[...])
    pos = pos0 + lax.broadcasted_iota(jnp.int32, (tm, 1), 0)
    for g, w in enumerate(POOL_WINDOWS):
        cols = slice(g * POOL_GROUP, (g + 1) * POOL_GROUP)
        acc = ext_ref[h - w // 2:h - w // 2 + tm, cols]
        for j in range(1, w):
            acc = acc + ext_ref[h - w // 2 + j:h - w // 2 + j + tm, cols]
        cnt = jnp.minimum(pos + w // 2, seq_len) - jnp.maximum(pos - w // 2, 0)
        d = acc / cnt.astype(F32) - cur_ref[:, cols]
        o_ref[:, cols] = _dot(d.astype(BF16), gw_ref[g]) * sc_ref[:, cols]


def _pool(u, group_w_bf16, scale):
    tm = 256
    hb = tm // POOL_HALO
    n_halo = T_ALL // POOL_HALO
    kern = functools.partial(_pool_kernel, tm=tm)
    return pl.pallas_call(
        kern,
        grid=(T_ALL // tm,),
        in_specs=[pl.BlockSpec((POOL_HALO, D), lambda i: (jnp.maximum(i * hb - 1, 0), 0)),
                  pl.BlockSpec((tm, D), lambda i: (i, 0)),
                  pl.BlockSpec((POOL_HALO, D), lambda i: (jnp.minimum((i + 1) * hb, n_halo - 1), 0)),
                  pl.BlockSpec((len(POOL_WINDOWS), POOL_GROUP, POOL_GROUP), lambda i: (0, 0, 0)),
                  pl.BlockSpec((1, D), lambda i: (0, 0))],
        out_specs=pl.BlockSpec((tm, D), lambda i: (i, 0)),
        out_shape=jax.ShapeDtypeStruct((T_ALL, D), F32),
        scratch_shapes=[pltpu.VMEM((tm + 2 * POOL_HALO, D), F32)],
        compiler_params=_params(("parallel",), 40),
        name="pool",
    )(u, u, u, group_w_bf16, scale.reshape(1, D))


def _post_mixer_kernel(a1_ref, a2_ref, w_ref, x_ref, gpost_ref, gate_ref, gpre_ref, sh_ref, sc_ref,
                       rwh_ref, rwl_ref, rb_ref, xo_ref, h_ref, lg_ref):
    k1 = a1_ref.shape[1]
    m = _dot(a1_ref[...].astype(BF16), w_ref[0:k1]) + _dot(a2_ref[...].astype(BF16), w_ref[k1:])
    xn = x_ref[...] + gate_ref[0] * (_rms(m) * gpost_ref[...])
    xo_ref[...] = xn
    h = _rms(xn) * gpre_ref[...] * (1.0 + sc_ref[0]) + sh_ref[0]
    hh, hl = _split_bf16(h)
    h_ref[...] = _pack_bf16_pairs(hh)
    lg_ref[...] = _dot(hh, rwh_ref[...]) + _dot(hl, rwh_ref[...]) + _dot(hh, rwl_ref[...]) + rb_ref[...]


def _post_mixer(a1, a1_col, a2, a2_col, w_bf16, x, g_post, gate, g_pre, shift, scale, rw_hi, rw_lo, rb, name):
    tm = 256
    kh = w_bf16.shape[0] // 2
    row = lambda i: (i, 0)
    const = lambda i: (0, 0)
    mod_spec = pl.BlockSpec((1, 1, D), lambda i: (_mod_row(i * tm), 0, 0))
    vec = pl.BlockSpec((1, D), const)
    return pl.pallas_call(
        _post_mixer_kernel,
        grid=(T_ALL // tm,),
        in_specs=[pl.BlockSpec((tm, kh), lambda i: (i, a1_col)),
                  pl.BlockSpec((tm, kh), lambda i: (i, a2_col)),
                  pl.BlockSpec((2 * kh, D), const),
                  pl.BlockSpec((tm, D), row), vec, mod_spec, vec, mod_spec, mod_spec,
                  pl.BlockSpec((D, LANES), const), pl.BlockSpec((D, LANES), const),
                  pl.BlockSpec((1, LANES), const)],
        out_specs=[pl.BlockSpec((tm, D), row), pl.BlockSpec((tm, D // 2), row), pl.BlockSpec((tm, LANES), row)],
        out_shape=[jax.ShapeDtypeStruct((T_ALL, D), F32), jax.ShapeDtypeStruct((T_ALL, D // 2), jnp.uint32),
                   jax.ShapeDtypeStruct((T_ALL, LANES), F32)],
        compiler_params=_params(("parallel",), 48),
        name=name,
    )(a1, a2, w_bf16, x, g_post, gate, g_pre, shift, scale, rw_hi, rw_lo, rb)


def _route_kernel(lg_ref, tri_ref, eidx_ref, gate_ref, rank_ref, cnt_ref, carry_ref):
    @pl.when(pl.program_id(0) == 0)
    def _():
        carry_ref[...] = jnp.zeros_like(carry_ref)

    shape = lg_ref.shape
    lane = lax.broadcasted_iota(jnp.int32, shape, 1).astype(F32)
    lg = jnp.where(lane < N_EXPERTS, lg_ref[...], -jnp.inf)
    multi = jnp.zeros(shape, F32)
    vals, sels = [], []
    eidx = jnp.zeros(shape, F32)
    for k in range(TOP_K):
        m = jnp.max(lg, axis=-1, keepdims=True)
        idx = jnp.min(jnp.where(lg == m, lane, float(LANES)), axis=-1, keepdims=True)
        sel = lane == idx
        multi = jnp.where(sel, 1.0, multi)
        lg = jnp.where(sel, -jnp.inf, lg)
        eidx = jnp.where(lane == k, idx, eidx)
        vals.append(m)
        sels.append(sel)
    exps = [jnp.exp(v - vals[0]) for v in vals]
    den = exps[0] + exps[1] + exps[2] + exps[3]
    rank_all = _dot(tri_ref[...], multi.astype(BF16)) + carry_ref[0:1]
    gate = jnp.zeros(shape, F32)
    rank = jnp.zeros(shape, F32)
    for k in range(TOP_K):
        gate = jnp.where(lane == k, exps[k] / den, gate)
        rank = jnp.where(lane == k, jnp.sum(jnp.where(sels[k], rank_all, 0.0), axis=-1, keepdims=True), rank)
    eidx_ref[...] = eidx.astype(jnp.int32)
    gate_ref[...] = gate
    rank_ref[...] = rank.astype(jnp.int32)
    carry_ref[0:1] = carry_ref[0:1] + jnp.sum(multi, axis=0, keepdims=True)
    cnt_ref[...] = carry_ref[...]


def _route(logits):
    tm = 256
    tri = (np.arange(tm)[:, None] > np.arange(tm)[None, :]).astype(np.float32)
    row = lambda i: (i, 0)
    spec = pl.BlockSpec((tm, LANES), row)
    return pl.pallas_call(
        _route_kernel,
        grid=(T_ALL // tm,),
        in_specs=[spec, pl.BlockSpec((tm, tm), lambda i: (0, 0))],
        out_specs=[spec, spec, spec, pl.BlockSpec((8, LANES), lambda i: (0, 0))],
        out_shape=[jax.ShapeDtypeStruct((T_ALL, LANES), jnp.int32), jax.ShapeDtypeStruct((T_ALL, LANES), F32),
                   jax.ShapeDtypeStruct((T_ALL, LANES), jnp.int32), jax.ShapeDtypeStruct((8, LANES), F32)],
        scratch_shapes=[pltpu.VMEM((8, LANES), F32)],
        compiler_params=_params(("arbitrary",), 32),
        name="route",
    )(logits, jnp.asarray(tri, BF16))


def _dest_kernel(eidx_ref, rank_ref, start_ref, o_ref):
    shape = eidx_ref.shape
    lane = lax.broadcasted_iota(jnp.int32, shape, 1).astype(F32)
    eidx = eidx_ref[...].astype(F32)
    starts = start_ref[...].astype(F32)
    dest = jnp.zeros(shape, F32)
    for k in range(TOP_K):
        e_k = jnp.sum(jnp.where(lane == k, eidx, 0.0), axis=-1, keepdims=True)
        start = jnp.sum(jnp.where(lane == e_k, starts, 0.0), axis=-1, keepdims=True)
        dest = jnp.where(lane == k, start, dest)
    o_ref[...] = jnp.where(lane < TOP_K, dest.astype(jnp.int32) + rank_ref[...], 0)


def _dest(eidx, rank, group_start):
    tm = 1024
    spec = pl.BlockSpec((tm, LANES), lambda i: (i, 0))
    return pl.pallas_call(
        _dest_kernel,
        grid=(T_ALL // tm,),
        in_specs=[spec, spec, pl.BlockSpec((1, LANES), lambda i: (0, 0))],
        out_specs=spec,
        out_shape=jax.ShapeDtypeStruct((T_ALL, LANES), jnp.int32),
        compiler_params=_params(("parallel",), 32),
        name="dest",
    )(eidx, rank, group_start)


def _dispatch_kernel(dest_ref, lo_ref, hi_ref, nu_ref, h_ref, o_ref, zero_ref, sem, zsem, *, tm):
    i = pl.program_id(0)
    base = i * tm * TOP_K

    @pl.when(i == 0)
    def _():
        zero_ref[...] = jnp.zeros_like(zero_ref)

    def issue(t, carry):
        for k in range(TOP_K):
            d = dest_ref[base + t * TOP_K + k]
            pltpu.make_async_copy(h_ref.at[pl.ds(t, 1)], o_ref.at[pl.ds(d, 1)], sem).start()
        return carry

    lax.fori_loop(0, tm, issue, 0)

    @pl.when(i < N_EXPERTS)
    def _():
        zrow = lambda r: pltpu.make_async_copy(zero_ref.at[pl.ds(0, 1)], o_ref.at[pl.ds(r, 1)], zsem)
        lax.fori_loop(lo_ref[i], hi_ref[i], lambda r, c: (zrow(r).start(), c)[1], 0)
        lax.fori_loop(lo_ref[i], hi_ref[i], lambda r, c: (zrow(r).wait(), c)[1], 0)

    blk = nu_ref[0] + i - N_EXPERTS

    @pl.when((i >= N_EXPERTS) & (blk < MOE_BLOCKS))
    def _():
        r0 = pl.multiple_of(blk * MOE_TM, MOE_TM)
        cp = pltpu.make_async_copy(zero_ref, o_ref.at[pl.ds(r0, MOE_TM)], zsem)
        cp.start()
        cp.wait()

    for k in range(TOP_K):
        pltpu.make_async_copy(h_ref, o_ref.at[pl.ds(0, tm)], sem).wait()


def _dispatch(dest_flat, pad_lo, pad_hi, n_used, h):
    tm = 128
    assert T_ALL // tm >= 2 * N_EXPERTS
    kern = functools.partial(_dispatch_kernel, tm=tm)
    return pl.pallas_call(
        kern,
        grid_spec=pltpu.PrefetchScalarGridSpec(
            num_scalar_prefetch=4,
            grid=(T_ALL // tm,),
            in_specs=[pl.BlockSpec((tm, D // 2), lambda i, *_: (i, 0))],
            out_specs=pl.BlockSpec(memory_space=pl.ANY),
            scratch_shapes=[pltpu.VMEM((MOE_TM, D // 2), jnp.uint32), pltpu.SemaphoreType.DMA,
                            pltpu.SemaphoreType.DMA]),
        out_shape=jax.ShapeDtypeStruct((MOE_ROWS, D // 2), jnp.uint32),
        compiler_params=_params(("arbitrary",), 32),
        name="dispatch",
    )(dest_flat, pad_lo, pad_hi, n_used, h)


def _new_weights(be_ref, i):
    return (i == 0) | (be_ref[i] != be_ref[jnp.maximum(i - 1, 0)])


def _expert_up_kernel(be_ref, nu_ref, x_ref, w_ref, b_ref, o_ref, wbf_ref):
    i = pl.program_id(1)
    tn = o_ref.shape[1]

    @pl.when(i < nu_ref[0])
    def _():
        @pl.when(_new_weights(be_ref, i))
        def _():
            wbf_ref[...] = w_ref[0, 0].astype(BF16)

        x = jnp.concatenate(_unpack_bf16_pairs(x_ref[...]), axis=1)
        even = lax.broadcasted_iota(jnp.int32, (x.shape[0], LANES), 1) % 2 == 0
        for c in range(tn // LANES):
            cols = slice(2 * LANES * c, 2 * LANES * (c + 1))
            hb = _dot(x, wbf_ref[:, cols]) + b_ref[0, 0, :, cols]
            first, second = hb[:, :LANES], hb[:, LANES:]
            g = jnp.where(even, first, pltpu.roll(second, 1, 1))
            lin = jnp.where(even, pltpu.roll(first, LANES - 1, 1), second)
            g = jnp.minimum(g, SWIGLU_LIMIT)
            lin = jnp.clip(lin, -SWIGLU_LIMIT, SWIGLU_LIMIT)
            act = g * jax.nn.sigmoid(SWIGLU_ALPHA * g) * (lin + 1.0)
            o_ref[:, LANES * c:LANES * (c + 1)] = act.astype(BF16)

    @pl.when(i >= nu_ref[0])
    def _():
        o_ref[...] = jnp.zeros_like(o_ref)


def _expert_up(blk_e, n_used, xs, w_gu, b_gu, layer):
    tn = FF_TILE
    blk = lambda j, i, be, nu: (jnp.minimum(i, nu[0] - 1), 0)
    return pl.pallas_call(
        _expert_up_kernel,
        grid_spec=pltpu.PrefetchScalarGridSpec(
            num_scalar_prefetch=2,
            grid=(D_FF // tn, MOE_BLOCKS),
            in_specs=[pl.BlockSpec((MOE_TM, D // 2), blk),
                      pl.BlockSpec((1, 1, D, 2 * tn), lambda j, i, be, nu: (layer, be[i], 0, j)),
                      pl.BlockSpec((1, 1, 1, 2 * tn), lambda j, i, be, nu: (layer, be[i], 0, j))],
            out_specs=pl.BlockSpec((MOE_TM, tn), lambda j, i, be, nu: (i, j)),
            scratch_shapes=[pltpu.VMEM((D, 2 * tn), BF16)]),
        out_shape=jax.ShapeDtypeStruct((MOE_ROWS, D_FF), BF16),
        compiler_params=_params(("arbitrary", "arbitrary"), 56),
        name="expert_up",
    )(blk_e, n_used, xs, w_gu, b_gu.reshape(N_LAYERS, N_EXPERTS, 1, 2 * D_FF))


def _expert_down_kernel(be_ref, nu_ref, a_ref, w_ref, b_ref, perm_ref, o_ref, wbf_ref):
    i = pl.program_id(1)

    @pl.when(i < nu_ref[0])
    def _():
        @pl.when(_new_weights(be_ref, i))
        def _():
            for g in range(D_FF // LANES):
                rows = slice(LANES * g, LANES * (g + 1))
                wbf_ref[rows] = _dot(perm_ref[...], w_ref[0, 0, rows].astype(BF16)).astype(BF16)

        o_ref[...] = _dot(a_ref[...], wbf_ref[...]) + b_ref[0, 0]

    @pl.when(i >= nu_ref[0])
    def _():
        o_ref[...] = jnp.zeros_like(o_ref)


def _expert_down(blk_e, n_used, a, w_down, b_down, layer):
    tn = D // 2
    lane = np.arange(LANES)
    src = np.where(lane % 2 == 0, lane // 2, LANES // 2 + lane // 2)
    perm = jnp.asarray(src[:, None] == lane[None, :], BF16)
    blk = lambda j, i, be, nu: (jnp.minimum(i, nu[0] - 1), 0)
    return pl.pallas_call(
        _expert_down_kernel,
        grid_spec=pltpu.PrefetchScalarGridSpec(
            num_scalar_prefetch=2,
            grid=(D // tn, MOE_BLOCKS),
            in_specs=[pl.BlockSpec((MOE_TM, D_FF), blk),
                      pl.BlockSpec((1, 1, D_FF, tn), lambda j, i, be, nu: (layer, be[i], 0, j)),
                      pl.BlockSpec((1, 1, 1, tn), lambda j, i, be, nu: (layer, be[i], 0, j)),
                      pl.BlockSpec((LANES, LANES), lambda j, i, be, nu: (0, 0))],
            out_specs=pl.BlockSpec((MOE_TM, tn), lambda j, i, be, nu: (i, j)),
            scratch_shapes=[pltpu.VMEM((D_FF, tn), BF16)]),
        out_shape=jax.ShapeDtypeStruct((MOE_ROWS, D), F32),
        compiler_params=_params(("arbitrary", "arbitrary"), 48),
        name="expert_down",
    )(blk_e, n_used, a, w_down, b_down.reshape(N_LAYERS, N_EXPERTS, 1, D), perm)


def _combine_kernel(dest_ref, y_ref, gate_ref, x_ref, gpost_ref, g2_ref, o_ref, buf_ref, sem, *, tm):
    base = pl.program_id(0) * tm * TOP_K

    def issue(t, carry):
        for k in range(TOP_K):
            d = dest_ref[base + t * TOP_K + k]
            pltpu.make_async_copy(y_ref.at[pl.ds(d, 1)], buf_ref.at[k, pl.ds(t, 1)], sem).start()
        return carry

    lax.fori_loop(0, tm, issue, 0)
    for k in range(TOP_K):
        pltpu.make_async_copy(y_ref.at[pl.ds(0, tm)], buf_ref.at[k], sem).wait()
    f = gate_ref[:, 0:1] * buf_ref[0]
    for k in range(1, TOP_K):
        f = f + gate_ref[:, k:k + 1] * buf_ref[k]
    o_ref[...] = x_ref[...] + g2_ref[0] * (_rms(f) * gpost_ref[...])


def _combine(dest_flat, ys, gate, x, g_post, gate2, name):
    tm = 128
    kern = functools.partial(_combine_kernel, tm=tm)
    return pl.pallas_call(
        kern,
        grid_spec=pltpu.PrefetchScalarGridSpec(
            num_scalar_prefetch=1,
            grid=(T_ALL // tm,),
            in_specs=[pl.BlockSpec(memory_space=pl.ANY),
                      pl.BlockSpec((tm, LANES), lambda i, d: (i, 0)),
                      pl.BlockSpec((tm, D), lambda i, d: (i, 0)),
                      pl.BlockSpec((1, D), lambda i, d: (0, 0)),
                      pl.BlockSpec((1, 1, D), lambda i, d: (_mod_row(i * tm), 0, 0))],
            out_specs=pl.BlockSpec((tm, D), lambda i, d: (i, 0)),
            scratch_shapes=[pltpu.VMEM((TOP_K, tm, D), F32), pltpu.SemaphoreType.DMA]),
        out_shape=jax.ShapeDtypeStruct((T_ALL, D), F32),
        compiler_params=_params(("arbitrary",), 32),
        name=name,
    )(dest_flat, ys, gate, x, g_post, gate2)


def _moe(h, logits, x, g_post, gate2, w_gu, b_gu, w_down, b_down, layer):
    eidx, gate, rank, counts = _route(logits)
    cnt = counts[0, :N_EXPERTS].astype(jnp.int32)
    padded = (cnt + MOE_TM - 1) // MOE_TM * MOE_TM
    group_end = jnp.cumsum(padded)
    group_start = jnp.zeros((1, LANES), jnp.int32).at[0, :N_EXPERTS].set(group_end - padded)
    n_used = group_end[-1] // MOE_TM
    blk_row0 = jnp.arange(MOE_BLOCKS, dtype=jnp.int32) * MOE_TM
    blk_row0 = jnp.minimum(blk_row0, (n_used - 1) * MOE_TM)
    blk_e = jnp.sum(group_end[None, :] <= blk_row0[:, None], axis=1).astype(jnp.int32)
    n_used = n_used.astype(jnp.int32).reshape(1)
    dest = _dest(eidx, rank, group_start)[:, :TOP_K].reshape(N_ASSIGN)
    xs = _dispatch(dest, group_end - padded + cnt, group_end, n_used, h)
    a = _expert_up(blk_e, n_used, xs, w_gu, b_gu, layer)
    ys = _expert_down(blk_e, n_used, a, w_down, b_down, layer)
    return _combine(dest, ys, gate, x, g_post, gate2, f"combine_{layer}")


def kernel(x_prompt, x_sample, cache_k, cache_v, c, c_ctx, ada_w, ada_b, norm_mix_pre, norm_mix_post,
           norm_ffn_pre, norm_ffn_post, ab_in_w, ab_out_w, hy_short_w, hy_short_b, hy_w1, hy_b1, hy_w2,
           hy_b2, hy_w3, hy_freq, hy_bias, na_rpb, pool_in_w, pool_group_w, pool_scale, pool_out_w,
           router_w, router_b, exp_w_gu, exp_b_gu, exp_w_down, exp_b_down):
    x = jnp.concatenate([x_prompt.reshape(T_PROMPT, D), x_sample.reshape(T_LATENT, D)], axis=0)
    cvec = jnp.concatenate([c_ctx[None], c, jnp.zeros((8 - 1 - N_LATENT_SEQ, D), F32)], axis=0)
    mods = _adaln(cvec, ada_w, ada_b).reshape(N_LAYERS, 8, 6, D)

    new_k = new_v = None
    for layer in range(N_LAYERS):
        sh1, sc1, g1, sh2, sc2, g2 = [mods[layer, :, k][:, None, :] for k in range(6)]
        row = lambda a: a[layer].reshape(1, D)
        if layer % 2 == 0:
            j = layer // 2
            p = _modmm(x, row(norm_mix_pre), sh1, sc1, ab_in_w[j].astype(BF16), "ab_in")
            filt = (hy_w1[j], hy_b1[j], hy_w2[j], hy_b2[j], hy_w3[j], hy_freq[j])
            hy = (hy_short_w[j], hy_short_b[j], hy_bias[j])
            yh_p = _hyena(p, *hy, _hyena_filter_time(L_PROMPT, *filt), N_PROMPT_SEQ, L_PROMPT, 0)
            yh_s = _hyena(p, *hy, _hyena_filter_time(L_LATENT, *filt), N_LATENT_SEQ, L_LATENT,
                          T_PROMPT // L_LATENT)
            ya_p, new_k, new_v = _ctx_attn(p)
            ck = cache_k[:, j].reshape(N_LATENT_SEQ, -1, NA_DIM)
            cv = cache_v[:, j].reshape(N_LATENT_SEQ, -1, NA_DIM)
            ya_s = _na_attn(p, ck, cv, na_rpb[j])
            a1 = jnp.concatenate([yh_p, yh_s], axis=0)
            a2 = jnp.concatenate([ya_p, ya_s], axis=0)
            a1_col = a2_col = 0
            w_out = ab_out_w[j].astype(BF16)
        else:
            j = layer // 2
            u = _modmm(x, row(norm_mix_pre), sh1, sc1, pool_in_w[j].astype(BF16), "pool_in")
            a1 = a2 = _pool(u, pool_group_w[j].astype(BF16), pool_scale[j])
            a1_col, a2_col = 0, 1
            w_out = pool_out_w[j].astype(BF16)
        rw = jnp.zeros((D, LANES), F32).at[:, :N_EXPERTS].set(router_w[layer])
        rw_hi, rw_lo = _split_bf16(rw)
        rb = jnp.zeros((1, LANES), F32).at[0, :N_EXPERTS].set(router_b[layer])
        x, h, logits = _post_mixer(a1, a1_col, a2, a2_col, w_out, x, row(norm_mix_post), g1,
                                   row(norm_ffn_pre), sh2, sc2, rw_hi, rw_lo, rb, f"post_mixer_{layer}")
        x = _moe(h, logits, x, row(norm_ffn_post), g2, exp_w_gu, exp_b_gu, exp_w_down, exp_b_down, layer)

    y_prompt = x[:T_PROMPT].reshape(x_prompt.shape)
    y_sample = x[T_PROMPT:].reshape(x_sample.shape)
    kv_shape = (N_PROMPT_SEQ, 1, L_PROMPT, N_HEADS, HEAD_DIM)
    return y_prompt, y_sample, new_k.reshape(kv_shape), new_v.reshape(kv_shape)
```

```python
import functools
import math

import jax
import jax.numpy as jnp
import numpy as np
from jax import lax
from jax.experimental import pallas as pl
from jax.experimental.pallas import tpu as pltpu

F32 = jnp.float32
BF16 = jnp.bfloat16

D = 2048
N_PROMPT_SEQ = 32
L_PROMPT = 256
N_LATENT_SEQ = 4
L_LATENT = 2048
T_PROMPT = N_PROMPT_SEQ * L_PROMPT
T_LATENT = N_LATENT_SEQ * L_LATENT
T_ALL = T_PROMPT + T_LATENT
N_LAYERS = 2
GRID_W = 64
GRID_ROWS = L_LATENT // GRID_W
HY_DIM = 1024
HY_EMB = 33
HY_DECAY_TARGET = 1e-2
HY_FAST_DECAY = 0.3
HY_SLOW_DECAY = 1.5
N_HEADS = 16
HEAD_DIM = 64
NA_DIM = N_HEADS * HEAD_DIM
WIN_R = 8
WIN_C = 16
AB_IN = 3 * HY_DIM + 3 * NA_DIM
POOL_WINDOWS = (2, 4, 8, 16)
POOL_GROUP = D // len(POOL_WINDOWS)
POOL_HALO = 8
N_EXPERTS = 32
TOP_K = 4
D_FF = D
SWIGLU_LIMIT = 7.0
SWIGLU_ALPHA = 1.702
RMS_EPS = 1e-6
NEG_BIG = -1e30

LANES = 128
MOE_TM = 512
FF_TILE = 1024
N_ASSIGN = T_ALL * TOP_K
MOE_BLOCKS = -(-(N_ASSIGN + N_EXPERTS * (MOE_TM - 1)) // MOE_TM)
MOE_ROWS = MOE_BLOCKS * MOE_TM
MIB = 1 << 20


def _params(semantics, vmem_mib):
    return pltpu.CompilerParams(dimension_semantics=semantics, vmem_limit_bytes=vmem_mib * MIB)


def _mod_row(row0):
    return jnp.where(row0 < T_PROMPT, 0, 1 + (row0 - T_PROMPT) // L_LATENT)


def _rms(x):
    return x * lax.rsqrt(jnp.mean(x * x, axis=-1, keepdims=True) + RMS_EPS)


def _split_bf16(x):
    hi = x.astype(BF16)
    lo = (x - hi.astype(F32)).astype(BF16)
    return hi, lo


def _dot(a, b):
    return jnp.dot(a, b, preferred_element_type=F32)


def _pack_bf16_pairs(x_bf16):
    half = x_bf16.shape[1] // 2
    bits = lax.bitcast_convert_type(x_bf16.astype(F32), jnp.uint32)
    return (bits[:, :half] & jnp.uint32(0xFFFF0000)) | (bits[:, half:] >> 16)


def _unpack_bf16_pairs(packed):
    hi = lax.bitcast_convert_type(packed & jnp.uint32(0xFFFF0000), F32).astype(BF16)
    lo = lax.bitcast_convert_type(packed << 16, F32).astype(BF16)
    return hi, lo


def _adaln_kernel(cv_ref, w_ref, b_ref, o_ref):
    s = jax.nn.silu(cv_ref[...]).astype(BF16)
    o_ref[0] = _dot(s, w_ref[0].astype(BF16)) + b_ref[0]


def _adaln(cvec, ada_w, ada_b):
    n = ada_w.shape[-1]
    tn = 1024
    return pl.pallas_call(
        _adaln_kernel,
        grid=(N_LAYERS, n // tn),
        in_specs=[pl.BlockSpec((8, D), lambda l, j: (0, 0)),
                  pl.BlockSpec((1, D, tn), lambda l, j: (l, 0, j)),
                  pl.BlockSpec((1, 1, tn), lambda l, j: (l, 0, j))],
        out_specs=pl.BlockSpec((1, 8, tn), lambda l, j: (l, 0, j)),
        out_shape=jax.ShapeDtypeStruct((N_LAYERS, 8, n), F32),
        compiler_params=_params(("parallel", "arbitrary"), 40),
        name="adaln",
    )(cvec, ada_w, ada_b.reshape(N_LAYERS, 1, n))


def _modmm_kernel(x_ref, g_ref, sh_ref, sc_ref, w_ref, o_ref, h_ref):
    @pl.when(pl.program_id(1) == 0)
    def _():
        h = _rms(x_ref[...]) * g_ref[...] * (1.0 + sc_ref[0]) + sh_ref[0]
        h_ref[...] = h.astype(BF16)

    o_ref[...] = _dot(h_ref[...], w_ref[...])


def _modmm(x, g, shift, scale, w_bf16, name):
    n = w_bf16.shape[1]
    tm, tn = 512, 1024
    mod_spec = pl.BlockSpec((1, 1, D), lambda i, j: (_mod_row(i * tm), 0, 0))
    return pl.pallas_call(
        _modmm_kernel,
        grid=(T_ALL // tm, n // tn),
        in_specs=[pl.BlockSpec((tm, D), lambda i, j: (i, 0)),
                  pl.BlockSpec((1, D), lambda i, j: (0, 0)),
                  mod_spec, mod_spec,
                  pl.BlockSpec((D, tn), lambda i, j: (0, j))],
        out_specs=pl.BlockSpec((tm, tn), lambda i, j: (i, j)),
        out_shape=jax.ShapeDtypeStruct((T_ALL, n), F32),
        scratch_shapes=[pltpu.VMEM((tm, D), BF16)],
        compiler_params=_params(("parallel", "arbitrary"), 40),
        name=name,
    )(x, g, shift, scale, w_bf16)


def _shortconv_kernel(z_ref, w_ref, b_ref, o_ref):
    z = z_ref[...]
    n = z.shape[0]
    row = lax.broadcasted_iota(jnp.int32, z.shape, 0)
    prev = jnp.where(row == 0, 0.0, pltpu.roll(z, 1, 0))
    nxt = jnp.where(row == n - 1, 0.0, pltpu.roll(z, n - 1, 0))
    o_ref[...] = w_ref[0:1] * prev + w_ref[1:2] * z + w_ref[2:3] * nxt + b_ref[...]


def _shortconv(p, w, b, n_seq, seq_len, row_block0):
    c = 3 * HY_DIM
    tc = 512
    return pl.pallas_call(
        _shortconv_kernel,
        grid=(n_seq, c // tc),
        in_specs=[pl.BlockSpec((seq_len, tc), lambda s, j: (row_block0 + s, j)),
                  pl.BlockSpec((3, tc), lambda s, j: (0, j)),
                  pl.BlockSpec((1, tc), lambda s, j: (0, j))],
        out_specs=pl.BlockSpec((seq_len, tc), lambda s, j: (s, j)),
        out_shape=jax.ShapeDtypeStruct((n_seq * seq_len, c), F32),
        compiler_params=_params(("parallel", "parallel"), 40),
        name=f"shortconv_{seq_len}",
    )(p, w, b.reshape(1, c))


def _seqmm_kernel(a_ref, x_ref, o_ref):
    o_ref[...] = _dot(a_ref[...], x_ref[...].astype(BF16))


def _seqmm(a_bf16, x, n_seq, col_block0, n_cols, name):
    m, k = a_bf16.shape
    tm = min(m, 1024)
    tn = 512
    return pl.pallas_call(
        _seqmm_kernel,
        grid=(n_seq, n_cols // tn, m // tm),
        in_specs=[pl.BlockSpec((tm, k), lambda s, j, i: (i, 0)),
                  pl.BlockSpec((k, tn), lambda s, j, i: (s, col_block0 + j))],
        out_specs=pl.BlockSpec((tm, tn), lambda s, j, i: (s * (m // tm) + i, j)),
        out_shape=jax.ShapeDtypeStruct((n_seq * m, n_cols), F32),
        compiler_params=_params(("parallel", "parallel", "arbitrary"), 40),
        name=name,
    )(a_bf16, x)


def _hyena_inv_kernel(g_ref, y_ref, k_ref, yin_ref, gate_ref, bias_ref, o_ref, z_ref, *, seq_len):
    @pl.when(pl.program_id(2) == 0)
    def _():
        yc, ys = y_ref[0:seq_len], y_ref[seq_len:]
        kc, ks = k_ref[0:seq_len], k_ref[seq_len:]
        z_ref[0:seq_len] = (yc * kc - ys * ks).astype(BF16)
        z_ref[seq_len:] = (yc * ks + ys * kc).astype(BF16)

    conv = _dot(g_ref[...], z_ref[...]) * (1.0 / seq_len)
    o_ref[...] = gate_ref[...] * (conv + yin_ref[...] * bias_ref[...])


def _hyena_inv(g_bf16, yspec, kspec, order, yin, yin_col0, u, gate_col0, bias, n_seq, seq_len):
    tn = 256
    tt = min(seq_len, 512)
    nb = HY_DIM // tn
    kern = functools.partial(_hyena_inv_kernel, seq_len=seq_len)
    return pl.pallas_call(
        kern,
        grid=(n_seq, nb, seq_len // tt),
        in_specs=[pl.BlockSpec((tt, 2 * seq_len), lambda s, j, t: (t, 0)),
                  pl.BlockSpec((2 * seq_len, tn), lambda s, j, t: (s, j)),
                  pl.BlockSpec((2 * seq_len, tn), lambda s, j, t: (0, order * nb + j)),
                  pl.BlockSpec((tt, tn), lambda s, j, t: (s * (seq_len // tt) + t, yin_col0 // tn + j)),
                  pl.BlockSpec((tt, tn), lambda s, j, t: (s * (seq_len // tt) + t, gate_col0 // tn + j)),
                  pl.BlockSpec((1, tn), lambda s, j, t: (0, order * nb + j))],
        out_specs=pl.BlockSpec((tt, tn), lambda s, j, t: (s * (seq_len // tt) + t, j)),
        out_shape=jax.ShapeDtypeStruct((n_seq * seq_len, HY_DIM), F32),
        scratch_shapes=[pltpu.VMEM((2 * seq_len, tn), BF16)],
        compiler_params=_params(("parallel", "parallel", "arbitrary"), 48),
        name=f"hyena_inv_{seq_len}_{order}",
    )(g_bf16, yspec, kspec, yin, u, bias.reshape(1, 2 * HY_DIM))


def _dft_matrices(seq_len):
    f = lax.broadcasted_iota(jnp.int32, (seq_len, seq_len), 0)
    s = lax.broadcasted_iota(jnp.int32, (seq_len, seq_len), 1)
    ang = (((2 * f + 1) * s) % (4 * seq_len)).astype(F32) * (math.pi / (2 * seq_len))
    c, sn = jnp.cos(ang), jnp.sin(ang)
    fwd = jnp.concatenate([c, sn], axis=0).astype(BF16)
    inv = jnp.concatenate([c.T, sn.T], axis=1).astype(BF16)
    return fwd, inv


def _hyena_filter_time(seq_len, w1, b1, w2, b2, w3, freq):
    hp = lax.Precision.HIGHEST
    t = jnp.linspace(0.0, 1.0, seq_len, dtype=F32)[:, None]
    bands = (HY_EMB - 1) // 2
    ang = 2.0 * math.pi * jnp.arange(seq_len, dtype=F32)[:, None] / seq_len
    fb = jnp.linspace(1e-4, bands - 1, bands, dtype=F32)[None, :]
    z = jnp.concatenate([t, jnp.cos(fb * ang), -jnp.sin(fb * ang)], axis=-1)
    h = jnp.sin(freq * (jnp.dot(z, w1, precision=hp) + b1))
    h = jnp.sin(freq * (jnp.dot(h, w2, precision=hp) + b2))
    h = jnp.dot(h, w3, precision=hp).reshape(seq_len, 2, 2, HY_DIM)
    deltas = jnp.linspace(math.log(HY_DECAY_TARGET) / HY_SLOW_DECAY,
                          math.log(HY_DECAY_TARGET) / HY_FAST_DECAY, HY_DIM, dtype=F32)
    h = h * jnp.exp(-t * jnp.abs(deltas))[:, None, None, :]
    return h * lax.rsqrt(jnp.sum(h * h, axis=(0, 1), keepdims=True))


def _filter_spec_kernel(a_ref, hf_ref, hb_ref, o_ref, *, seq_len, tm):
    sign = jnp.where(pl.program_id(1) * tm < seq_len, 1.0, -1.0)
    hb = hb_ref[...]
    hb = jnp.where(lax.broadcasted_iota(jnp.int32, hb.shape, 0) == 0, 0.0, hb)
    hi, lo = _split_bf16(hf_ref[...] + sign * hb)
    o_ref[...] = _dot(a_ref[...], hi) + _dot(a_ref[...], lo)


def _hyena_filter_spec(h, fwd_bf16, seq_len):
    h2d = h.reshape(seq_len, 4 * HY_DIM)
    tm = min(seq_len, 1024)
    tn = 512
    nb = 2 * HY_DIM // tn
    kern = functools.partial(_filter_spec_kernel, seq_len=seq_len, tm=tm)
    return pl.pallas_call(
        kern,
        grid=(nb, 2 * seq_len // tm),
        in_specs=[pl.BlockSpec((tm, seq_len), lambda j, i: (i, 0)),
                  pl.BlockSpec((seq_len, tn), lambda j, i: (0, j)),
                  pl.BlockSpec((seq_len, tn), lambda j, i: (0, nb + j))],
        out_specs=pl.BlockSpec((tm, tn), lambda j, i: (i, j)),
        out_shape=jax.ShapeDtypeStruct((2 * seq_len, 2 * HY_DIM), F32),
        compiler_params=_params(("parallel", "arbitrary"), 40),
        name=f"filter_spec_{seq_len}",
    )(fwd_bf16, h2d, h2d)


def _hyena(p, short_w, short_b, bias, h_time, n_seq, seq_len, row_block0):
    fwd, inv = _dft_matrices(seq_len)
    kspec = _hyena_filter_spec(h_time, fwd, seq_len)
    u = _shortconv(p, short_w, short_b, n_seq, seq_len, row_block0)
    yspec = _seqmm(fwd, u, n_seq, 2 * HY_DIM // 512, HY_DIM, f"hyena_fwd_{seq_len}_0")
    y1 = _hyena_inv(inv, yspec, kspec, 0, u, 2 * HY_DIM, u, 0, bias, n_seq, seq_len)
    yspec = _seqmm(fwd, y1, n_seq, 0, HY_DIM, f"hyena_fwd_{seq_len}_1")
    return _hyena_inv(inv, yspec, kspec, 1, y1, 0, u, HY_DIM, bias, n_seq, seq_len)


def _nt_dot(a, b):
    return lax.dot_general(a, b, (((1,), (1,)), ((), ())), preferred_element_type=F32)


def _ctx_attn_kernel(q_ref, k_ref, v_ref, o_ref, ko_ref, vo_ref):
    scale = HEAD_DIM ** -0.5
    ko_ref[...] = k_ref[...]
    vo_ref[...] = v_ref[...]
    for h in range(N_HEADS):
        sl = slice(h * HEAD_DIM, (h + 1) * HEAD_DIM)
        s = _nt_dot(q_ref[:, sl].astype(BF16), k_ref[:, sl].astype(BF16)) * scale
        e = jnp.exp(s - jnp.max(s, axis=-1, keepdims=True))
        pr = e / jnp.sum(e, axis=-1, keepdims=True)
        o_ref[:, sl] = _dot(pr.astype(BF16), v_ref[:, sl].astype(BF16))


def _ctx_attn(p):
    qb = 3 * HY_DIM // NA_DIM
    spec = lambda c: pl.BlockSpec((L_PROMPT, NA_DIM), lambda s: (s, c))
    out = jax.ShapeDtypeStruct((T_PROMPT, NA_DIM), F32)
    return pl.pallas_call(
        _ctx_attn_kernel,
        grid=(N_PROMPT_SEQ,),
        in_specs=[spec(qb), spec(qb + 1), spec(qb + 2)],
        out_specs=[spec(0), spec(0), spec(0)],
        out_shape=[out, out, out],
        compiler_params=_params(("parallel",), 40),
        name="ctx_attn",
    )(p, p, p)


def _na_kernel(q_ref, k_ref, v_ref, kc_ref, vc_ref, b_ref, o_ref):
    r = pl.program_id(1)
    scale = HEAD_DIM ** -0.5
    row_start = jnp.clip(r - WIN_R // 2, 0, GRID_ROWS - WIN_R)
    k0 = pl.multiple_of(row_start * GRID_W, GRID_W)
    n_loc = WIN_R * GRID_W
    for h in range(N_HEADS):
        sl = slice(h * HEAD_DIM, (h + 1) * HEAD_DIM)
        q = q_ref[:, sl].astype(BF16)
        s_loc = _nt_dot(q, k_ref[pl.ds(k0, n_loc), sl].astype(BF16)) * scale + b_ref[0, h]
        s_ctx = _nt_dot(q, kc_ref[0, :, sl].astype(BF16)) * scale
        m = jnp.maximum(jnp.max(s_loc, axis=-1, keepdims=True), jnp.max(s_ctx, axis=-1, keepdims=True))
        e_loc = jnp.exp(s_loc - m)
        e_ctx = jnp.exp(s_ctx - m)
        den = jnp.sum(e_loc, axis=-1, keepdims=True) + jnp.sum(e_ctx, axis=-1, keepdims=True)
        o_ref[:, sl] = (_dot((e_loc / den).astype(BF16), v_ref[pl.ds(k0, n_loc), sl].astype(BF16))
                        + _dot((e_ctx / den).astype(BF16), vc_ref[0, :, sl].astype(BF16)))


def _na_bias_table(rpb):
    q = np.arange(GRID_W)[:, None]
    kc = np.arange(GRID_W)[None, :]
    cs = np.clip(q - WIN_C // 2, 0, GRID_W - WIN_C)
    valid = (kc >= cs) & (kc < cs + WIN_C)
    col_off = np.clip(kc - q, -(WIN_C - 1), WIN_C - 1) + WIN_C - 1
    b = rpb.astype(F32)[:, :, col_off]
    b = jnp.where(jnp.asarray(valid)[None, None], b, NEG_BIG)
    rows = np.arange(WIN_R)[:, None] + np.arange(WIN_R)[None, :]
    b = b[:, rows]
    return b.transpose(1, 0, 3, 2, 4).reshape(WIN_R, N_HEADS, GRID_W, WIN_R * GRID_W)


def _na_attn(p, cache_k, cache_v, rpb):
    qb = 3 * HY_DIM // NA_DIM
    q_row0 = T_PROMPT // GRID_W
    seq0 = T_PROMPT // L_LATENT
    bias = _na_bias_table(rpb)

    def bias_idx(b, r):
        row_start = jnp.clip(r - WIN_R // 2, 0, GRID_ROWS - WIN_R)
        return (row_start - r + WIN_R - 1, 0, 0, 0)

    ctx_spec = pl.BlockSpec((1, cache_k.shape[1], NA_DIM), lambda b, r: (b, 0, 0))
    return pl.pallas_call(
        _na_kernel,
        grid=(N_LATENT_SEQ, GRID_ROWS),
        in_specs=[pl.BlockSpec((GRID_W, NA_DIM), lambda b, r: (q_row0 + b * GRID_ROWS + r, qb)),
                  pl.BlockSpec((L_LATENT, NA_DIM), lambda b, r: (seq0 + b, qb + 1)),
                  pl.BlockSpec((L_LATENT, NA_DIM), lambda b, r: (seq0 + b, qb + 2)),
                  ctx_spec, ctx_spec,
                  pl.BlockSpec((1, N_HEADS, GRID_W, WIN_R * GRID_W), bias_idx)],
        out_specs=pl.BlockSpec((GRID_W, NA_DIM), lambda b, r: (b * GRID_ROWS + r, 0)),
        out_shape=jax.ShapeDtypeStruct((T_LATENT, NA_DIM), F32),
        compiler_params=_params(("parallel", "arbitrary"), 52),
        name="na_attn",
    )(p, p, p, cache_k, cache_v, bias)


def _pool_kernel(prev_ref, cur_ref, next_ref, gw_ref, sc_ref, o_ref, ext_ref, *, tm):
    i = pl.program_id(0)
    row0 = i * tm
    seq_len = jnp.where(row0 < T_PROMPT, L_PROMPT, L_LATENT)
    pos0 = jnp.where(row0 < T_PROMPT, row0 % L_PROMPT, (row0 - T_PROMPT) % L_LATENT)
    first = pos0 == 0
    last = pos0 + tm == seq_len
    h = POOL_HALO
    ext_ref[0:h] = jnp.where(first, 0.0, prev_ref[...])
    ext_ref[h:h + tm] = cur_ref[...]
    ext_ref[h + tm:] = jnp.where(last, 0.0, next_ref[...])
    pos = pos0 + lax.broadcasted_iota(jnp.int32, (tm, 1), 0)
    for g, w in enumerate(POOL_WINDOWS):
        cols = slice(g * POOL_GROUP, (g + 1) * POOL_GROUP)
        acc = ext_ref[h - w // 2:h - w // 2 + tm, cols]
        for j in range(1, w):
            acc = acc + ext_ref[h - w // 2 + j:h - w // 2 + j + tm, cols]
        cnt = jnp.minimum(pos + w // 2, seq_len) - jnp.maximum(pos - w // 2, 0)
        d = acc / cnt.astype(F32) - cur_ref[:, cols]
        o_ref[:, cols] = _dot(d.astype(BF16), gw_ref[g]) * sc_ref[:, cols]


def _pool(u, group_w_bf16, scale):
    tm = 256
    hb = tm // POOL_HALO
    n_halo = T_ALL // POOL_HALO
    kern = functools.partial(_pool_kernel, tm=tm)
    return pl.pallas_call(
        kern,
        grid=(T_ALL // tm,),
        in_specs=[pl.BlockSpec((POOL_HALO, D), lambda i: (jnp.maximum(i * hb - 1, 0), 0)),
                  pl.BlockSpec((tm, D), lambda i: (i, 0)),
                  pl.BlockSpec((POOL_HALO, D), lambda i: (jnp.minimum((i + 1) * hb, n_halo - 1), 0)),
                  pl.BlockSpec((len(POOL_WINDOWS), POOL_GROUP, POOL_GROUP), lambda i: (0, 0, 0)),
                  pl.BlockSpec((1, D), lambda i: (0, 0))],
        out_specs=pl.BlockSpec((tm, D), lambda i: (i, 0)),
        out_shape=jax.ShapeDtypeStruct((T_ALL, D), F32),
        scratch_shapes=[pltpu.VMEM((tm + 2 * POOL_HALO, D), F32)],
        compiler_params=_params(("parallel",), 40),
        name="pool",
    )(u, u, u, group_w_bf16, scale.reshape(1, D))


def _post_mixer_kernel(*refs, n1, n2, tm):
    a1_refs, a2_refs = refs[:n1], refs[n1:n1 + n2]
    (w_ref, x_ref, gpost_ref, gate_ref, gpre_ref, sh_ref, sc_ref, rwh_ref, rwl_ref, rb_ref,
     xo_ref, h_ref, lg_ref) = refs[n1 + n2:]
    in_prompt = pl.program_id(0) * tm < T_PROMPT

    def rows(part_refs):
        if len(part_refs) == 1:
            return part_refs[0][...].astype(BF16)
        return jnp.where(in_prompt, part_refs[0][...], part_refs[1][...]).astype(BF16)

    a1, a2 = rows(a1_refs), rows(a2_refs)
    k1 = a1.shape[1]
    m = _dot(a1, w_ref[0:k1]) + _dot(a2, w_ref[k1:])
    xn = x_ref[...] + gate_ref[0] * (_rms(m) * gpost_ref[...])
    xo_ref[...] = xn
    h = _rms(xn) * gpre_ref[...] * (1.0 + sc_ref[0]) + sh_ref[0]
    hh, hl = _split_bf16(h)
    h_ref[...] = _pack_bf16_pairs(hh)
    lg_ref[...] = _dot(hh, rwh_ref[...]) + _dot(hl, rwh_ref[...]) + _dot(hh, rwl_ref[...]) + rb_ref[...]


def _post_mixer(a1_parts, a1_col, a2_parts, a2_col, w_bf16, x, g_post, gate, g_pre, shift, scale,
                rw_hi, rw_lo, rb, name):
    tm = 256
    kh = w_bf16.shape[0] // 2
    row = lambda i: (i, 0)
    const = lambda i: (0, 0)
    mod_spec = pl.BlockSpec((1, 1, D), lambda i: (_mod_row(i * tm), 0, 0))
    vec = pl.BlockSpec((1, D), const)
    n_prompt_tiles = T_PROMPT // tm

    def part_specs(parts, col):
        if len(parts) == 1:
            return [pl.BlockSpec((tm, kh), lambda i: (i, col))]
        return [pl.BlockSpec((tm, kh), lambda i: (jnp.minimum(i, n_prompt_tiles - 1), col)),
                pl.BlockSpec((tm, kh), lambda i: (jnp.maximum(i - n_prompt_tiles, 0), col))]

    kern = functools.partial(_post_mixer_kernel, n1=len(a1_parts), n2=len(a2_parts), tm=tm)
    return pl.pallas_call(
        kern,
        grid=(T_ALL // tm,),
        in_specs=part_specs(a1_parts, a1_col) + part_specs(a2_parts, a2_col) + [
                  pl.BlockSpec((2 * kh, D), const),
                  pl.BlockSpec((tm, D), row), vec, mod_spec, vec, mod_spec, mod_spec,
                  pl.BlockSpec((D, LANES), const), pl.BlockSpec((D, LANES), const),
                  pl.BlockSpec((1, LANES), const)],
        out_specs=[pl.BlockSpec((tm, D), row), pl.BlockSpec((tm, D // 2), row), pl.BlockSpec((tm, LANES), row)],
        out_shape=[jax.ShapeDtypeStruct((T_ALL, D), F32), jax.ShapeDtypeStruct((T_ALL, D // 2), jnp.uint32),
                   jax.ShapeDtypeStruct((T_ALL, LANES), F32)],
        compiler_params=_params(("parallel",), 48),
        name=name,
    )(*a1_parts, *a2_parts, w_bf16, x, g_post, gate, g_pre, shift, scale, rw_hi, rw_lo, rb)


def _route_kernel(lg_ref, tri_ref, eidx_ref, gate_ref, rank_ref, cnt_ref, carry_ref):
    @pl.when(pl.program_id(0) == 0)
    def _():
        carry_ref[...] = jnp.zeros_like(carry_ref)

    shape = lg_ref.shape
    lane = lax.broadcasted_iota(jnp.int32, shape, 1).astype(F32)
    lg = jnp.where(lane < N_EXPERTS, lg_ref[...], -jnp.inf)
    multi = jnp.zeros(shape, F32)
    vals, sels = [], []
    eidx = jnp.zeros(shape, F32)
    for k in range(TOP_K):
        m = jnp.max(lg, axis=-1, keepdims=True)
        idx = jnp.min(jnp.where(lg == m, lane, float(LANES)), axis=-1, keepdims=True)
        sel = lane == idx
        multi = jnp.where(sel, 1.0, multi)
        lg = jnp.where(sel, -jnp.inf, lg)
        eidx = jnp.where(lane == k, idx, eidx)
        vals.append(m)
        sels.append(sel)
    exps = [jnp.exp(v - vals[0]) for v in vals]
    den = exps[0] + exps[1] + exps[2] + exps[3]
    rank_all = _dot(tri_ref[...], multi.astype(BF16)) + carry_ref[0:1]
    gate = jnp.zeros(shape, F32)
    rank = jnp.zeros(shape, F32)
    for k in range(TOP_K):
        gate = jnp.where(lane == k, exps[k] / den, gate)
        rank = jnp.where(lane == k, jnp.sum(jnp.where(sels[k], rank_all, 0.0), axis=-1, keepdims=True), rank)
    eidx_ref[...] = eidx.astype(jnp.int32)
    gate_ref[...] = gate
    rank_ref[...] = rank.astype(jnp.int32)
    carry_ref[0:1] = carry_ref[0:1] + jnp.sum(multi, axis=0, keepdims=True)
    cnt_ref[...] = carry_ref[...]


def _route(logits):
    tm = 256
    tri = (np.arange(tm)[:, None] > np.arange(tm)[None, :]).astype(np.float32)
    row = lambda i: (i, 0)
    spec = pl.BlockSpec((tm, LANES), row)
    return pl.pallas_call(
        _route_kernel,
        grid=(T_ALL // tm,),
        in_specs=[spec, pl.BlockSpec((tm, tm), lambda i: (0, 0))],
        out_specs=[spec, spec, spec, pl.BlockSpec((8, LANES), lambda i: (0, 0))],
        out_shape=[jax.ShapeDtypeStruct((T_ALL, LANES), jnp.int32), jax.ShapeDtypeStruct((T_ALL, LANES), F32),
                   jax.ShapeDtypeStruct((T_ALL, LANES), jnp.int32), jax.ShapeDtypeStruct((8, LANES), F32)],
        scratch_shapes=[pltpu.VMEM((8, LANES), F32)],
        compiler_params=_params(("arbitrary",), 32),
        name="route",
    )(logits, jnp.asarray(tri, BF16))


def _dest_kernel(eidx_ref, rank_ref, start_ref, o_ref):
    shape = eidx_ref.shape
    lane = lax.broadcasted_iota(jnp.int32, shape, 1).astype(F32)
    eidx = eidx_ref[...].astype(F32)
    starts = start_ref[...].astype(F32)
    dest = jnp.zeros(shape, F32)
    for k in range(TOP_K):
        e_k = jnp.sum(jnp.where(lane == k, eidx, 0.0), axis=-1, keepdims=True)
        start = jnp.sum(jnp.where(lane == e_k, starts, 0.0), axis=-1, keepdims=True)
        dest = jnp.where(lane == k, start, dest)
    o_ref[...] = jnp.where(lane < TOP_K, dest.astype(jnp.int32) + rank_ref[...], 0)


def _dest(eidx, rank, group_start):
    tm = 1024
    spec = pl.BlockSpec((tm, LANES), lambda i: (i, 0))
    return pl.pallas_call(
        _dest_kernel,
        grid=(T_ALL // tm,),
        in_specs=[spec, spec, pl.BlockSpec((1, LANES), lambda i: (0, 0))],
        out_specs=spec,
        out_shape=jax.ShapeDtypeStruct((T_ALL, LANES), jnp.int32),
        compiler_params=_params(("parallel",), 32),
        name="dest",
    )(eidx, rank, group_start)


def _dispatch_kernel(dest_ref, lo_ref, hi_ref, nu_ref, h_ref, o_ref, zero_ref, sem, zsem, *, tm):
    i = pl.program_id(0)
    base = i * tm * TOP_K

    @pl.when(i == 0)
    def _():
        zero_ref[...] = jnp.zeros_like(zero_ref)

    def issue(t, carry):
        for k in range(TOP_K):
            d = dest_ref[base + t * TOP_K + k]
            pltpu.make_async_copy(h_ref.at[pl.ds(t, 1)], o_ref.at[pl.ds(d, 1)], sem).start()
        return carry

    lax.fori_loop(0, tm, issue, 0, unroll=2)

    @pl.when(i < N_EXPERTS)
    def _():
        zrow = lambda r: pltpu.make_async_copy(zero_ref.at[pl.ds(0, 1)], o_ref.at[pl.ds(r, 1)], zsem)
        lax.fori_loop(lo_ref[i], hi_ref[i], lambda r, c: (zrow(r).start(), c)[1], 0)
        lax.fori_loop(lo_ref[i], hi_ref[i], lambda r, c: (zrow(r).wait(), c)[1], 0)

    blk = nu_ref[0] + i - N_EXPERTS

    @pl.when((i >= N_EXPERTS) & (blk < MOE_BLOCKS))
    def _():
        r0 = pl.multiple_of(blk * MOE_TM, MOE_TM)
        cp = pltpu.make_async_copy(zero_ref, o_ref.at[pl.ds(r0, MOE_TM)], zsem)
        cp.start()
        cp.wait()

    for k in range(TOP_K):
        pltpu.make_async_copy(h_ref, o_ref.at[pl.ds(0, tm)], sem).wait()


def _dispatch(dest_flat, pad_lo, pad_hi, n_used, h):
    tm = 128
    assert T_ALL // tm >= 2 * N_EXPERTS
    kern = functools.partial(_dispatch_kernel, tm=tm)
    return pl.pallas_call(
        kern,
        grid_spec=pltpu.PrefetchScalarGridSpec(
            num_scalar_prefetch=4,
            grid=(T_ALL // tm,),
            in_specs=[pl.BlockSpec((tm, D // 2), lambda i, *_: (i, 0))],
            out_specs=pl.BlockSpec(memory_space=pl.ANY),
            scratch_shapes=[pltpu.VMEM((MOE_TM, D // 2), jnp.uint32), pltpu.SemaphoreType.DMA,
                            pltpu.SemaphoreType.DMA]),
        out_shape=jax.ShapeDtypeStruct((MOE_ROWS, D // 2), jnp.uint32),
        compiler_params=_params(("arbitrary",), 32),
        name="dispatch",
    )(dest_flat, pad_lo, pad_hi, n_used, h)


def _new_weights(be_ref, i):
    return (i == 0) | (be_ref[i] != be_ref[jnp.maximum(i - 1, 0)])


def _expert_up_kernel(be_ref, nu_ref, x_ref, w_ref, b_ref, o_ref, wbf_ref):
    i = pl.program_id(1)
    tn = o_ref.shape[1]

    @pl.when(i < nu_ref[0])
    def _():
        @pl.when(_new_weights(be_ref, i))
        def _():
            wbf_ref[...] = w_ref[0, 0].astype(BF16)

        x = jnp.concatenate(_unpack_bf16_pairs(x_ref[...]), axis=1)
        even = lax.broadcasted_iota(jnp.int32, (x.shape[0], LANES), 1) % 2 == 0
        for c in range(tn // LANES):
            cols = slice(2 * LANES * c, 2 * LANES * (c + 1))
            hb = _dot(x, wbf_ref[:, cols]) + b_ref[0, 0, :, cols]
            first, second = hb[:, :LANES], hb[:, LANES:]
            g = jnp.where(even, first, pltpu.roll(second, 1, 1))
            lin = jnp.where(even, pltpu.roll(first, LANES - 1, 1), second)
            g = jnp.minimum(g, SWIGLU_LIMIT)
            lin = jnp.clip(lin, -SWIGLU_LIMIT, SWIGLU_LIMIT)
            act = g * jax.nn.sigmoid(SWIGLU_ALPHA * g) * (lin + 1.0)
            o_ref[:, LANES * c:LANES * (c + 1)] = act.astype(BF16)

    @pl.when(i >= nu_ref[0])
    def _():
        o_ref[...] = jnp.zeros_like(o_ref)


def _expert_up(blk_e, n_used, xs, w_gu, b_gu, layer):
    tn = FF_TILE
    blk = lambda j, i, be, nu: (jnp.minimum(i, nu[0] - 1), 0)
    return pl.pallas_call(
        _expert_up_kernel,
        grid_spec=pltpu.PrefetchScalarGridSpec(
            num_scalar_prefetch=2,
            grid=(D_FF // tn, MOE_BLOCKS),
            in_specs=[pl.BlockSpec((MOE_TM, D // 2), blk),
                      pl.BlockSpec((1, 1, D, 2 * tn), lambda j, i, be, nu: (layer, be[i], 0, j)),
                      pl.BlockSpec((1, 1, 1, 2 * tn), lambda j, i, be, nu: (layer, be[i], 0, j))],
            out_specs=pl.BlockSpec((MOE_TM, tn), lambda j, i, be, nu: (i, j)),
            scratch_shapes=[pltpu.VMEM((D, 2 * tn), BF16)]),
        out_shape=jax.ShapeDtypeStruct((MOE_ROWS, D_FF), BF16),
        compiler_params=_params(("arbitrary", "arbitrary"), 56),
        name="expert_up",
    )(blk_e, n_used, xs, w_gu, b_gu.reshape(N_LAYERS, N_EXPERTS, 1, 2 * D_FF))


def _expert_down_kernel(be_ref, nu_ref, a_ref, w_ref, b_ref, perm_ref, o_ref, wbf_ref):
    i = pl.program_id(1)

    @pl.when(i < nu_ref[0])
    def _():
        @pl.when(_new_weights(be_ref, i))
        def _():
            for g in range(D_FF // LANES):
                rows = slice(LANES * g, LANES * (g + 1))
                wbf_ref[rows] = _dot(perm_ref[...], w_ref[0, 0, rows].astype(BF16)).astype(BF16)

        o_ref[...] = _dot(a_ref[...], wbf_ref[...]) + b_ref[0, 0]

    @pl.when(i >= nu_ref[0])
    def _():
        o_ref[...] = jnp.zeros_like(o_ref)


def _expert_down(blk_e, n_used, a, w_down, b_down, layer):
    tn = D // 2
    lane = np.arange(LANES)
    src = np.where(lane % 2 == 0, lane // 2, LANES // 2 + lane // 2)
    perm = jnp.asarray(src[:, None] == lane[None, :], BF16)
    blk = lambda j, i, be, nu: (jnp.minimum(i, nu[0] - 1), 0)
    return pl.pallas_call(
        _expert_down_kernel,
        grid_spec=pltpu.PrefetchScalarGridSpec(
            num_scalar_prefetch=2,
            grid=(D // tn, MOE_BLOCKS),
            in_specs=[pl.BlockSpec((MOE_TM, D_FF), blk),
                      pl.BlockSpec((1, 1, D_FF, tn), lambda j, i, be, nu: (layer, be[i], 0, j)),
                      pl.BlockSpec((1, 1, 1, tn), lambda j, i, be, nu: (layer, be[i], 0, j)),
                      pl.BlockSpec((LANES, LANES), lambda j, i, be, nu: (0, 0))],
            out_specs=pl.BlockSpec((MOE_TM, tn), lambda j, i, be, nu: (i, j)),
            scratch_shapes=[pltpu.VMEM((D_FF, tn), BF16)]),
        out_shape=jax.ShapeDtypeStruct((MOE_ROWS, D), F32),
        compiler_params=_params(("arbitrary", "arbitrary"), 48),
        name="expert_down",
    )(blk_e, n_used, a, w_down, b_down.reshape(N_LAYERS, N_EXPERTS, 1, D), perm)


def _combine_kernel(dest_ref, y_ref, gate_ref, x_ref, gpost_ref, g2_ref, o_ref, buf_ref, sem, *, tm):
    i = pl.program_id(0)

    def gather(step, slot):
        base = step * tm * TOP_K

        def issue(t, carry):
            for k in range(TOP_K):
                d = dest_ref[base + t * TOP_K + k]
                pltpu.make_async_copy(y_ref.at[pl.ds(d, 1)], buf_ref.at[slot, k, pl.ds(t, 1)],
                                      sem.at[slot]).start()
            return carry

        lax.fori_loop(0, tm, issue, 0, unroll=2)

    @pl.when(i == 0)
    def _():
        gather(0, 0)

    @pl.when(i + 1 < pl.num_programs(0))
    def _():
        gather(i + 1, (i + 1) % 2)

    slot = i % 2
    for k in range(TOP_K):
        pltpu.make_async_copy(y_ref.at[pl.ds(0, tm)], buf_ref.at[slot, k], sem.at[slot]).wait()
    f = gate_ref[:, 0:1] * buf_ref[slot, 0]
    for k in range(1, TOP_K):
        f = f + gate_ref[:, k:k + 1] * buf_ref[slot, k]
    o_ref[...] = x_ref[...] + g2_ref[0] * (_rms(f) * gpost_ref[...])


def _combine(dest_flat, ys, gate, x, g_post, gate2, name):
    tm = 128
    kern = functools.partial(_combine_kernel, tm=tm)
    return pl.pallas_call(
        kern,
        grid_spec=pltpu.PrefetchScalarGridSpec(
            num_scalar_prefetch=1,
            grid=(T_ALL // tm,),
            in_specs=[pl.BlockSpec(memory_space=pl.ANY),
                      pl.BlockSpec((tm, LANES), lambda i, d: (i, 0)),
                      pl.BlockSpec((tm, D), lambda i, d: (i, 0)),
                      pl.BlockSpec((1, D), lambda i, d: (0, 0)),
                      pl.BlockSpec((1, 1, D), lambda i, d: (_mod_row(i * tm), 0, 0))],
            out_specs=pl.BlockSpec((tm, D), lambda i, d: (i, 0)),
            scratch_shapes=[pltpu.VMEM((2, TOP_K, tm, D), F32), pltpu.SemaphoreType.DMA((2,))]),
        out_shape=jax.ShapeDtypeStruct((T_ALL, D), F32),
        compiler_params=_params(("arbitrary",), 32),
        name=name,
    )(dest_flat, ys, gate, x, g_post, gate2)


def _moe(h, logits, x, g_post, gate2, w_gu, b_gu, w_down, b_down, layer):
    eidx, gate, rank, counts = _route(logits)
    cnt = counts[0, :N_EXPERTS].astype(jnp.int32)
    padded = (cnt + MOE_TM - 1) // MOE_TM * MOE_TM
    group_end = jnp.cumsum(padded)
    group_start = jnp.zeros((1, LANES), jnp.int32).at[0, :N_EXPERTS].set(group_end - padded)
    n_used = group_end[-1] // MOE_TM
    blk_row0 = jnp.arange(MOE_BLOCKS, dtype=jnp.int32) * MOE_TM
    blk_row0 = jnp.minimum(blk_row0, (n_used - 1) * MOE_TM)
    blk_e = jnp.sum(group_end[None, :] <= blk_row0[:, None], axis=1).astype(jnp.int32)
    n_used = n_used.astype(jnp.int32).reshape(1)
    dest = _dest(eidx, rank, group_start)[:, :TOP_K].reshape(N_ASSIGN)
    xs = _dispatch(dest, group_end - padded + cnt, group_end, n_used, h)
    a = _expert_up(blk_e, n_used, xs, w_gu, b_gu, layer)
    ys = _expert_down(blk_e, n_used, a, w_down, b_down, layer)
    return _combine(dest, ys, gate, x, g_post, gate2, f"combine_{layer}")


def kernel(x_prompt, x_sample, cache_k, cache_v, c, c_ctx, ada_w, ada_b, norm_mix_pre, norm_mix_post,
           norm_ffn_pre, norm_ffn_post, ab_in_w, ab_out_w, hy_short_w, hy_short_b, hy_w1, hy_b1, hy_w2,
           hy_b2, hy_w3, hy_freq, hy_bias, na_rpb, pool_in_w, pool_group_w, pool_scale, pool_out_w,
           router_w, router_b, exp_w_gu, exp_b_gu, exp_w_down, exp_b_down):
    x = jnp.concatenate([x_prompt.reshape(T_PROMPT, D), x_sample.reshape(T_LATENT, D)], axis=0)
    cvec = jnp.concatenate([c_ctx[None], c, jnp.zeros((8 - 1 - N_LATENT_SEQ, D), F32)], axis=0)
    mods = _adaln(cvec, ada_w, ada_b).reshape(N_LAYERS, 8, 6, D)

    new_k = new_v = None
    for layer in range(N_LAYERS):
        sh1, sc1, g1, sh2, sc2, g2 = [mods[layer, :, k][:, None, :] for k in range(6)]
        row = lambda a: a[layer].reshape(1, D)
        if layer % 2 == 0:
            j = layer // 2
            p = _modmm(x, row(norm_mix_pre), sh1, sc1, ab_in_w[j].astype(BF16), "ab_in")
            filt = (hy_w1[j], hy_b1[j], hy_w2[j], hy_b2[j], hy_w3[j], hy_freq[j])
            hy = (hy_short_w[j], hy_short_b[j], hy_bias[j])
            yh_p = _hyena(p, *hy, _hyena_filter_time(L_PROMPT, *filt), N_PROMPT_SEQ, L_PROMPT, 0)
            yh_s = _hyena(p, *hy, _hyena_filter_time(L_LATENT, *filt), N_LATENT_SEQ, L_LATENT,
                          T_PROMPT // L_LATENT)
            ya_p, new_k, new_v = _ctx_attn(p)
            ck = cache_k[:, j].reshape(N_LATENT_SEQ, -1, NA_DIM)
            cv = cache_v[:, j].reshape(N_LATENT_SEQ, -1, NA_DIM)
            ya_s = _na_attn(p, ck, cv, na_rpb[j])
            a1 = [yh_p, yh_s]
            a2 = [ya_p, ya_s]
            a1_col = a2_col = 0
            w_out = ab_out_w[j].astype(BF16)
        else:
            j = layer // 2
            u = _modmm(x, row(norm_mix_pre), sh1, sc1, pool_in_w[j].astype(BF16), "pool_in")
            a1 = a2 = [_pool(u, pool_group_w[j].astype(BF16), pool_scale[j])]
            a1_col, a2_col = 0, 1
            w_out = pool_out_w[j].astype(BF16)
        rw = jnp.zeros((D, LANES), F32).at[:, :N_EXPERTS].set(router_w[layer])
        rw_hi, rw_lo = _split_bf16(rw)
        rb = jnp.zeros((1, LANES), F32).at[0, :N_EXPERTS].set(router_b[layer])
        x, h, logits = _post_mixer(a1, a1_col, a2, a2_col, w_out, x, row(norm_mix_post), g1,
                                   row(norm_ffn_pre), sh2, sc2, rw_hi, rw_lo, rb, f"post_mixer_{layer}")
        x = _moe(h, logits, x, row(norm_ffn_post), g2, exp_w_gu, exp_b_gu, exp_w_down, exp_b_down, layer)

    y_prompt = x[:T_PROMPT].reshape(x_prompt.shape)
    y_sample = x[T_PROMPT:].reshape(x_sample.shape)
    kv_shape = (N_PROMPT_SEQ, 1, L_PROMPT, N_HEADS, HEAD_DIM)
    return y_prompt, y_sample, new_k.reshape(kv_shape), new_v.reshape(kv_shape)
```

```python
import functools
import math

import jax
import jax.numpy as jnp
import numpy as np
from jax import lax
from jax.experimental import pallas as pl
from jax.experimental.pallas import tpu as pltpu

F32 = jnp.float32
BF16 = jnp.bfloat16

D = 2048
N_PROMPT_SEQ = 32
L_PROMPT = 256
N_LATENT_SEQ = 4
L_LATENT = 2048
T_PROMPT = N_PROMPT_SEQ * L_PROMPT
T_LATENT = N_LATENT_SEQ * L_LATENT
T_ALL = T_PROMPT + T_LATENT
N_LAYERS = 2
GRID_W = 64
GRID_ROWS = L_LATENT // GRID_W
HY_DIM = 1024
HY_EMB = 33
HY_DECAY_TARGET = 1e-2
HY_FAST_DECAY = 0.3
HY_SLOW_DECAY = 1.5
N_HEADS = 16
HEAD_DIM = 64
NA_DIM = N_HEADS * HEAD_DIM
WIN_R = 8
WIN_C = 16
AB_IN = 3 * HY_DIM + 3 * NA_DIM
POOL_WINDOWS = (2, 4, 8, 16)
POOL_GROUP = D // len(POOL_WINDOWS)
POOL_HALO = 8
N_EXPERTS = 32
TOP_K = 4
D_FF = D
SWIGLU_LIMIT = 7.0
SWIGLU_ALPHA = 1.702
RMS_EPS = 1e-6
NEG_BIG = -1e30

LANES = 128
MOE_TM = 512
FF_TILE = 1024
N_ASSIGN = T_ALL * TOP_K
MOE_BLOCKS = -(-(N_ASSIGN + N_EXPERTS * (MOE_TM - 1)) // MOE_TM)
MOE_ROWS = MOE_BLOCKS * MOE_TM
MIB = 1 << 20


def _params(semantics, vmem_mib):
    return pltpu.CompilerParams(dimension_semantics=semantics, vmem_limit_bytes=vmem_mib * MIB)


def _mod_row(row0):
    return jnp.where(row0 < T_PROMPT, 0, 1 + (row0 - T_PROMPT) // L_LATENT)


def _rms(x):
    return x * lax.rsqrt(jnp.mean(x * x, axis=-1, keepdims=True) + RMS_EPS)


def _split_bf16(x):
    hi = x.astype(BF16)
    lo = (x - hi.astype(F32)).astype(BF16)
    return hi, lo


def _dot(a, b):
    return jnp.dot(a, b, preferred_element_type=F32)


def _pack_bf16_pairs(x_bf16):
    half = x_bf16.shape[1] // 2
    bits = lax.bitcast_convert_type(x_bf16.astype(F32), jnp.uint32)
    return (bits[:, :half] & jnp.uint32(0xFFFF0000)) | (bits[:, half:] >> 16)


def _unpack_bf16_pairs(packed):
    hi = lax.bitcast_convert_type(packed & jnp.uint32(0xFFFF0000), F32).astype(BF16)
    lo = lax.bitcast_convert_type(packed << 16, F32).astype(BF16)
    return hi, lo


def _adaln_kernel(cv_ref, w_ref, b_ref, o_ref):
    s = jax.nn.silu(cv_ref[...]).astype(BF16)
    o_ref[0] = _dot(s, w_ref[0].astype(BF16)) + b_ref[0]


def _adaln(cvec, ada_w, ada_b):
    n = ada_w.shape[-1]
    tn = 1024
    return pl.pallas_call(
        _adaln_kernel,
        grid=(N_LAYERS, n // tn),
        in_specs=[pl.BlockSpec((8, D), lambda l, j: (0, 0)),
                  pl.BlockSpec((1, D, tn), lambda l, j: (l, 0, j)),
                  pl.BlockSpec((1, 1, tn), lambda l, j: (l, 0, j))],
        out_specs=pl.BlockSpec((1, 8, tn), lambda l, j: (l, 0, j)),
        out_shape=jax.ShapeDtypeStruct((N_LAYERS, 8, n), F32),
        compiler_params=_params(("parallel", "arbitrary"), 40),
        name="adaln",
    )(cvec, ada_w, ada_b.reshape(N_LAYERS, 1, n))


def _modmm_kernel(x_ref, g_ref, sh_ref, sc_ref, w_ref, o_ref, h_ref):
    @pl.when(pl.program_id(1) == 0)
    def _():
        h = _rms(x_ref[...]) * g_ref[...] * (1.0 + sc_ref[0]) + sh_ref[0]
        h_ref[...] = h.astype(BF16)

    o_ref[...] = _dot(h_ref[...], w_ref[...])


def _modmm(x, g, shift, scale, w_bf16, name):
    n = w_bf16.shape[1]
    tm, tn = 1024, 1024
    mod_spec = pl.BlockSpec((1, 1, D), lambda i, j: (_mod_row(i * tm), 0, 0))
    return pl.pallas_call(
        _modmm_kernel,
        grid=(T_ALL // tm, n // tn),
        in_specs=[pl.BlockSpec((tm, D), lambda i, j: (i, 0)),
                  pl.BlockSpec((1, D), lambda i, j: (0, 0)),
                  mod_spec, mod_spec,
                  pl.BlockSpec((D, tn), lambda i, j: (0, j))],
        out_specs=pl.BlockSpec((tm, tn), lambda i, j: (i, j)),
        out_shape=jax.ShapeDtypeStruct((T_ALL, n), F32),
        scratch_shapes=[pltpu.VMEM((tm, D), BF16)],
        compiler_params=_params(("parallel", "arbitrary"), 48),
        name=name,
    )(x, g, shift, scale, w_bf16)


def _shortconv_kernel(z_ref, w_ref, b_ref, o_ref):
    z = z_ref[...]
    n = z.shape[0]
    row = lax.broadcasted_iota(jnp.int32, z.shape, 0)
    prev = jnp.where(row == 0, 0.0, pltpu.roll(z, 1, 0))
    nxt = jnp.where(row == n - 1, 0.0, pltpu.roll(z, n - 1, 0))
    o_ref[...] = w_ref[0:1] * prev + w_ref[1:2] * z + w_ref[2:3] * nxt + b_ref[...]


def _shortconv(p, w, b, n_seq, seq_len, row_block0):
    c = 3 * HY_DIM
    tc = 512 if seq_len > 512 else c
    return pl.pallas_call(
        _shortconv_kernel,
        grid=(n_seq, c // tc),
        in_specs=[pl.BlockSpec((seq_len, tc), lambda s, j: (row_block0 + s, j)),
                  pl.BlockSpec((3, tc), lambda s, j: (0, j)),
                  pl.BlockSpec((1, tc), lambda s, j: (0, j))],
        out_specs=pl.BlockSpec((seq_len, tc), lambda s, j: (s, j)),
        out_shape=jax.ShapeDtypeStruct((n_seq * seq_len, c), F32),
        compiler_params=_params(("parallel", "parallel"), 40),
        name=f"shortconv_{seq_len}",
    )(p, w, b.reshape(1, c))


def _seqmm_kernel(a_ref, x_ref, o_ref):
    o_ref[...] = _dot(a_ref[...], x_ref[...].astype(BF16))


def _seqmm(a_bf16, x, n_seq, col0, n_cols, name):
    m, k = a_bf16.shape
    tm = min(m, 1024)
    tn = 512 if k > 512 else 1024
    col_block0 = col0 // tn
    return pl.pallas_call(
        _seqmm_kernel,
        grid=(n_seq, n_cols // tn, m // tm),
        in_specs=[pl.BlockSpec((tm, k), lambda s, j, i: (i, 0)),
                  pl.BlockSpec((k, tn), lambda s, j, i: (s, col_block0 + j))],
        out_specs=pl.BlockSpec((tm, tn), lambda s, j, i: (s * (m // tm) + i, j)),
        out_shape=jax.ShapeDtypeStruct((n_seq * m, n_cols), F32),
        compiler_params=_params(("parallel", "parallel", "arbitrary"), 40),
        name=name,
    )(a_bf16, x)


def _hyena_inv_kernel(g_ref, y_ref, k_ref, yin_ref, gate_ref, bias_ref, o_ref, z_ref, *, seq_len):
    @pl.when(pl.program_id(2) == 0)
    def _():
        yc, ys = y_ref[0:seq_len], y_ref[seq_len:]
        kc, ks = k_ref[0:seq_len], k_ref[seq_len:]
        z_ref[0:seq_len] = (yc * kc - ys * ks).astype(BF16)
        z_ref[seq_len:] = (yc * ks + ys * kc).astype(BF16)

    conv = _dot(g_ref[...], z_ref[...]) * (1.0 / seq_len)
    o_ref[...] = gate_ref[...] * (conv + yin_ref[...] * bias_ref[...])


def _hyena_inv(g_bf16, yspec, kspec, order, yin, yin_col0, u, gate_col0, bias, n_seq, seq_len):
    tn = 256 if seq_len > 512 else 1024
    tt = min(seq_len, 512)
    nb = HY_DIM // tn
    kern = functools.partial(_hyena_inv_kernel, seq_len=seq_len)
    return pl.pallas_call(
        kern,
        grid=(n_seq, nb, seq_len // tt),
        in_specs=[pl.BlockSpec((tt, 2 * seq_len), lambda s, j, t: (t, 0)),
                  pl.BlockSpec((2 * seq_len, tn), lambda s, j, t: (s, j)),
                  pl.BlockSpec((2 * seq_len, tn), lambda s, j, t: (0, order * nb + j)),
                  pl.BlockSpec((tt, tn), lambda s, j, t: (s * (seq_len // tt) + t, yin_col0 // tn + j)),
                  pl.BlockSpec((tt, tn), lambda s, j, t: (s * (seq_len // tt) + t, gate_col0 // tn + j)),
                  pl.BlockSpec((1, tn), lambda s, j, t: (0, order * nb + j))],
        out_specs=pl.BlockSpec((tt, tn), lambda s, j, t: (s * (seq_len // tt) + t, j)),
        out_shape=jax.ShapeDtypeStruct((n_seq * seq_len, HY_DIM), F32),
        scratch_shapes=[pltpu.VMEM((2 * seq_len, tn), BF16)],
        compiler_params=_params(("parallel", "parallel", "arbitrary"), 48),
        name=f"hyena_inv_{seq_len}_{order}",
    )(g_bf16, yspec, kspec, yin, u, bias.reshape(1, 2 * HY_DIM))


def _dft_matrices(seq_len):
    f = lax.broadcasted_iota(jnp.int32, (seq_len, seq_len), 0)
    s = lax.broadcasted_iota(jnp.int32, (seq_len, seq_len), 1)
    ang = (((2 * f + 1) * s) % (4 * seq_len)).astype(F32) * (math.pi / (2 * seq_len))
    c, sn = jnp.cos(ang), jnp.sin(ang)
    fwd = jnp.concatenate([c, sn], axis=0).astype(BF16)
    inv = jnp.concatenate([c.T, sn.T], axis=1).astype(BF16)
    return fwd, inv


def _hyena_filter_time(seq_len, w1, b1, w2, b2, w3, freq):
    hp = lax.Precision.HIGHEST
    t = jnp.linspace(0.0, 1.0, seq_len, dtype=F32)[:, None]
    bands = (HY_EMB - 1) // 2
    ang = 2.0 * math.pi * jnp.arange(seq_len, dtype=F32)[:, None] / seq_len
    fb = jnp.linspace(1e-4, bands - 1, bands, dtype=F32)[None, :]
    z = jnp.concatenate([t, jnp.cos(fb * ang), -jnp.sin(fb * ang)], axis=-1)
    h = jnp.sin(freq * (jnp.dot(z, w1, precision=hp) + b1))
    h = jnp.sin(freq * (jnp.dot(h, w2, precision=hp) + b2))
    h = jnp.dot(h, w3, precision=hp).reshape(seq_len, 2, 2, HY_DIM)
    deltas = jnp.linspace(math.log(HY_DECAY_TARGET) / HY_SLOW_DECAY,
                          math.log(HY_DECAY_TARGET) / HY_FAST_DECAY, HY_DIM, dtype=F32)
    h = h * jnp.exp(-t * jnp.abs(deltas))[:, None, None, :]
    return h * lax.rsqrt(jnp.sum(h * h, axis=(0, 1), keepdims=True))


def _filter_spec_kernel(a_ref, hf_ref, hb_ref, o_ref, *, seq_len, tm):
    sign = jnp.where(pl.program_id(1) * tm < seq_len, 1.0, -1.0)
    hb = hb_ref[...]
    hb = jnp.where(lax.broadcasted_iota(jnp.int32, hb.shape, 0) == 0, 0.0, hb)
    hi, lo = _split_bf16(hf_ref[...] + sign * hb)
    o_ref[...] = _dot(a_ref[...], hi) + _dot(a_ref[...], lo)


def _hyena_filter_spec(h, fwd_bf16, seq_len):
    h2d = h.reshape(seq_len, 4 * HY_DIM)
    tm = min(seq_len, 1024)
    tn = 512
    nb = 2 * HY_DIM // tn
    kern = functools.partial(_filter_spec_kernel, seq_len=seq_len, tm=tm)
    return pl.pallas_call(
        kern,
        grid=(nb, 2 * seq_len // tm),
        in_specs=[pl.BlockSpec((tm, seq_len), lambda j, i: (i, 0)),
                  pl.BlockSpec((seq_len, tn), lambda j, i: (0, j)),
                  pl.BlockSpec((seq_len, tn), lambda j, i: (0, nb + j))],
        out_specs=pl.BlockSpec((tm, tn), lambda j, i: (i, j)),
        out_shape=jax.ShapeDtypeStruct((2 * seq_len, 2 * HY_DIM), F32),
        compiler_params=_params(("parallel", "arbitrary"), 40),
        name=f"filter_spec_{seq_len}",
    )(fwd_bf16, h2d, h2d)


def _hyena(p, short_w, short_b, bias, h_time, n_seq, seq_len, row_block0):
    fwd, inv = _dft_matrices(seq_len)
    kspec = _hyena_filter_spec(h_time, fwd, seq_len)
    u = _shortconv(p, short_w, short_b, n_seq, seq_len, row_block0)
    yspec = _seqmm(fwd, u, n_seq, 2 * HY_DIM, HY_DIM, f"hyena_fwd_{seq_len}_0")
    y1 = _hyena_inv(inv, yspec, kspec, 0, u, 2 * HY_DIM, u, 0, bias, n_seq, seq_len)
    yspec = _seqmm(fwd, y1, n_seq, 0, HY_DIM, f"hyena_fwd_{seq_len}_1")
    return _hyena_inv(inv, yspec, kspec, 1, y1, 0, u, HY_DIM, bias, n_seq, seq_len)


def _nt_dot(a, b):
    return lax.dot_general(a, b, (((1,), (1,)), ((), ())), preferred_element_type=F32)


def _ctx_attn_kernel(q_ref, k_ref, v_ref, o_ref, ko_ref, vo_ref):
    scale = HEAD_DIM ** -0.5
    ko_ref[...] = k_ref[...]
    vo_ref[...] = v_ref[...]
    for h in range(N_HEADS):
        sl = slice(h * HEAD_DIM, (h + 1) * HEAD_DIM)
        s = _nt_dot(q_ref[:, sl].astype(BF16), k_ref[:, sl].astype(BF16)) * scale
        e = jnp.exp(s - jnp.max(s, axis=-1, keepdims=True))
        pr = e / jnp.sum(e, axis=-1, keepdims=True)
        o_ref[:, sl] = _dot(pr.astype(BF16), v_ref[:, sl].astype(BF16))


def _ctx_attn(p):
    qb = 3 * HY_DIM // NA_DIM
    spec = lambda c: pl.BlockSpec((L_PROMPT, NA_DIM), lambda s: (s, c))
    out = jax.ShapeDtypeStruct((T_PROMPT, NA_DIM), F32)
    return pl.pallas_call(
        _ctx_attn_kernel,
        grid=(N_PROMPT_SEQ,),
        in_specs=[spec(qb), spec(qb + 1), spec(qb + 2)],
        out_specs=[spec(0), spec(0), spec(0)],
        out_shape=[out, out, out],
        compiler_params=_params(("parallel",), 40),
        name="ctx_attn",
    )(p, p, p)


NA_Q_ROWS = 8
NA_K_ROWS = NA_Q_ROWS + WIN_R
NA_ROW_BLOCKS = GRID_ROWS // NA_Q_ROWS
NA_HEADS_PER_STEP = LANES // HEAD_DIM


def _na_key_row0(rb, xp=jnp):
    return xp.clip(rb * NA_Q_ROWS - WIN_R // 2, 0, GRID_ROWS - NA_K_ROWS)


def _na_kernel(q_ref, k_ref, v_ref, kc_ref, vc_ref, b_ref, o_ref):
    scale = HEAD_DIM ** -0.5
    k0 = pl.multiple_of(_na_key_row0(pl.program_id(1)) * GRID_W, GRID_W)
    n_loc = NA_K_ROWS * GRID_W
    for h in range(NA_HEADS_PER_STEP):
        sl = slice(h * HEAD_DIM, (h + 1) * HEAD_DIM)
        q = q_ref[:, sl].astype(BF16)
        s_loc = _nt_dot(q, k_ref[pl.ds(k0, n_loc), sl].astype(BF16)) * scale + b_ref[0, 0, h]
        s_ctx = _nt_dot(q, kc_ref[0, :, sl].astype(BF16)) * scale
        m = jnp.maximum(jnp.max(s_loc, axis=-1, keepdims=True), jnp.max(s_ctx, axis=-1, keepdims=True))
        e_loc = jnp.exp(s_loc - m)
        e_ctx = jnp.exp(s_ctx - m)
        den = jnp.sum(e_loc, axis=-1, keepdims=True) + jnp.sum(e_ctx, axis=-1, keepdims=True)
        o_ref[:, sl] = (_dot((e_loc / den).astype(BF16), v_ref[pl.ds(k0, n_loc), sl].astype(BF16))
                        + _dot((e_ctx / den).astype(BF16), vc_ref[0, :, sl].astype(BF16)))


NA_BIAS_KINDS = 3


def _na_bias_kind(rb):
    return jnp.where(rb == 0, 0, jnp.where(rb == NA_ROW_BLOCKS - 1, 2, 1))


def _na_bias_table(rpb):
    q = np.arange(GRID_W)[:, None]
    kc = np.arange(GRID_W)[None, :]
    cs = np.clip(q - WIN_C // 2, 0, GRID_W - WIN_C)
    col_valid = (kc >= cs) & (kc < cs + WIN_C)
    col_off = np.clip(kc - q, -(WIN_C - 1), WIN_C - 1) + WIN_C - 1
    b = rpb.astype(F32)[:, :, col_off]
    b = jnp.where(jnp.asarray(col_valid)[None, None], b, NEG_BIG)
    def geometry(rb):
        q_row = rb * NA_Q_ROWS + np.arange(NA_Q_ROWS)[:, None]
        k_row = _na_key_row0(rb, np) + np.arange(NA_K_ROWS)[None, :]
        row_start = np.clip(q_row - WIN_R // 2, 0, GRID_ROWS - WIN_R)
        row_valid = (k_row >= row_start) & (k_row < row_start + WIN_R)
        return row_valid, np.clip(k_row - q_row + WIN_R - 1, 0, 2 * WIN_R - 2)

    for rb in range(2, NA_ROW_BLOCKS - 1):
        assert all(np.array_equal(a, c) for a, c in zip(geometry(rb), geometry(1)))
    tables = []
    for rb in (0, 1, NA_ROW_BLOCKS - 1):
        row_valid, d = geometry(rb)
        t = jnp.where(jnp.asarray(row_valid)[None, :, :, None, None], b[:, d], NEG_BIG)
        tables.append(t.transpose(0, 1, 3, 2, 4).reshape(N_HEADS, NA_Q_ROWS * GRID_W, NA_K_ROWS * GRID_W))
    t = jnp.stack(tables)
    return t.reshape(NA_BIAS_KINDS, N_HEADS // NA_HEADS_PER_STEP, NA_HEADS_PER_STEP, *t.shape[2:])


def _na_attn(p, cache_k, cache_v, rpb):
    tq = NA_Q_ROWS * GRID_W
    q_col0 = 3 * HY_DIM // LANES
    k_col0 = q_col0 + NA_DIM // LANES
    v_col0 = k_col0 + NA_DIM // LANES
    q_blk0 = T_PROMPT // tq
    seq0 = T_PROMPT // L_LATENT
    bias = _na_bias_table(rpb)
    ctx_spec = pl.BlockSpec((1, cache_k.shape[1], LANES), lambda hp, rb, b: (b, 0, hp))
    return pl.pallas_call(
        _na_kernel,
        grid=(N_HEADS // NA_HEADS_PER_STEP, NA_ROW_BLOCKS, N_LATENT_SEQ),
        in_specs=[pl.BlockSpec((tq, LANES), lambda hp, rb, b: (q_blk0 + b * NA_ROW_BLOCKS + rb, q_col0 + hp)),
                  pl.BlockSpec((L_LATENT, LANES), lambda hp, rb, b: (seq0 + b, k_col0 + hp)),
                  pl.BlockSpec((L_LATENT, LANES), lambda hp, rb, b: (seq0 + b, v_col0 + hp)),
                  ctx_spec, ctx_spec,
                  pl.BlockSpec((1, 1, NA_HEADS_PER_STEP, tq, NA_K_ROWS * GRID_W),
                               lambda hp, rb, b: (_na_bias_kind(rb), hp, 0, 0, 0))],
        out_specs=pl.BlockSpec((tq, LANES), lambda hp, rb, b: (b * NA_ROW_BLOCKS + rb, hp)),
        out_shape=jax.ShapeDtypeStruct((T_LATENT, NA_DIM), F32),
        compiler_params=_params(("parallel", "parallel", "arbitrary"), 48),
        name="na_attn",
    )(p, p, p, cache_k, cache_v, bias)


def _pool_kernel(prev_ref, cur_ref, next_ref, gw_ref, sc_ref, o_ref, ext_ref, *, tm):
    i = pl.program_id(0)
    row0 = i * tm
    seq_len = jnp.where(row0 < T_PROMPT, L_PROMPT, L_LATENT)
    pos0 = jnp.where(row0 < T_PROMPT, row0 % L_PROMPT, (row0 - T_PROMPT) % L_LATENT)
    first = pos0 == 0
    last = pos0 + tm == seq_len
    h = POOL_HALO
    ext_ref[0:h] = jnp.where(first, 0.0, prev_ref[...])
    ext_ref[h:h + tm] = cur_ref[...]
    ext_ref[h + tm:] = jnp.where(last, 0.0, next_ref[...])
    pos = pos0 + lax.broadcasted_iota(jnp.int32, (tm, 1), 0)
    for g, w in enumerate(POOL_WINDOWS):
        cols = slice(g * POOL_GROUP, (g + 1) * POOL_GROUP)
        acc = ext_ref[h - w // 2:h - w // 2 + tm, cols]
        for j in range(1, w):
            acc = acc + ext_ref[h - w // 2 + j:h - w // 2 + j + tm, cols]
        cnt = jnp.minimum(pos + w // 2, seq_len) - jnp.maximum(pos - w // 2, 0)
        d = acc / cnt.astype(F32) - cur_ref[:, cols]
        o_ref[:, cols] = _dot(d.astype(BF16), gw_ref[g]) * sc_ref[:, cols]


def _pool(u, group_w_bf16, scale):
    tm = 256
    hb = tm // POOL_HALO
    n_halo = T_ALL // POOL_HALO
    kern = functools.partial(_pool_kernel, tm=tm)
    return pl.pallas_call(
        kern,
        grid=(T_ALL // tm,),
        in_specs=[pl.BlockSpec((POOL_HALO, D), lambda i: (jnp.maximum(i * hb - 1, 0), 0)),
                  pl.BlockSpec((tm, D), lambda i: (i, 0)),
                  pl.BlockSpec((POOL_HALO, D), lambda i: (jnp.minimum((i + 1) * hb, n_halo - 1), 0)),
                  pl.BlockSpec((len(POOL_WINDOWS), POOL_GROUP, POOL_GROUP), lambda i: (0, 0, 0)),
                  pl.BlockSpec((1, D), lambda i: (0, 0))],
        out_specs=pl.BlockSpec((tm, D), lambda i: (i, 0)),
        out_shape=jax.ShapeDtypeStruct((T_ALL, D), F32),
        scratch_shapes=[pltpu.VMEM((tm + 2 * POOL_HALO, D), F32)],
        compiler_params=_params(("parallel",), 40),
        name="pool",
    )(u, u, u, group_w_bf16, scale.reshape(1, D))


def _post_mixer_kernel(*refs, n1, n2, tm):
    a1_refs, a2_refs = refs[:n1], refs[n1:n1 + n2]
    (w_ref, x_ref, gpost_ref, gate_ref, gpre_ref, sh_ref, sc_ref, rwh_ref, rwl_ref, rb_ref,
     xo_ref, h_ref, lg_ref) = refs[n1 + n2:]
    in_prompt = pl.program_id(0) * tm < T_PROMPT

    def rows(part_refs):
        if len(part_refs) == 1:
            return part_refs[0][...].astype(BF16)
        return jnp.where(in_prompt, part_refs[0][...], part_refs[1][...]).astype(BF16)

    a1, a2 = rows(a1_refs), rows(a2_refs)
    k1 = a1.shape[1]
    m = _dot(a1, w_ref[0:k1]) + _dot(a2, w_ref[k1:])
    xn = x_ref[...] + gate_ref[0] * (_rms(m) * gpost_ref[...])
    xo_ref[...] = xn
    h = _rms(xn) * gpre_ref[...] * (1.0 + sc_ref[0]) + sh_ref[0]
    hh, hl = _split_bf16(h)
    h_ref[...] = _pack_bf16_pairs(hh)
    lg_ref[...] = _dot(hh, rwh_ref[...]) + _dot(hl, rwh_ref[...]) + _dot(hh, rwl_ref[...]) + rb_ref[...]


def _post_mixer(a1_parts, a1_col, a2_parts, a2_col, w_bf16, x, g_post, gate, g_pre, shift, scale,
                rw_hi, rw_lo, rb, name):
    tm = 256
    kh = w_bf16.shape[0] // 2
    row = lambda i: (i, 0)
    const = lambda i: (0, 0)
    mod_spec = pl.BlockSpec((1, 1, D), lambda i: (_mod_row(i * tm), 0, 0))
    vec = pl.BlockSpec((1, D), const)
    n_prompt_tiles = T_PROMPT // tm

    def part_specs(parts, col):
        if len(parts) == 1:
            return [pl.BlockSpec((tm, kh), lambda i: (i, col))]
        return [pl.BlockSpec((tm, kh), lambda i: (jnp.minimum(i, n_prompt_tiles - 1), col)),
                pl.BlockSpec((tm, kh), lambda i: (jnp.maximum(i - n_prompt_tiles, 0), col))]

    kern = functools.partial(_post_mixer_kernel, n1=len(a1_parts), n2=len(a2_parts), tm=tm)
    return pl.pallas_call(
        kern,
        grid=(T_ALL // tm,),
        in_specs=part_specs(a1_parts, a1_col) + part_specs(a2_parts, a2_col) + [
                  pl.BlockSpec((2 * kh, D), const),
                  pl.BlockSpec((tm, D), row), vec, mod_spec, vec, mod_spec, mod_spec,
                  pl.BlockSpec((D, LANES), const), pl.BlockSpec((D, LANES), const),
                  pl.BlockSpec((1, LANES), const)],
        out_specs=[pl.BlockSpec((tm, D), row), pl.BlockSpec((tm, D // 2), row), pl.BlockSpec((tm, LANES), row)],
        out_shape=[jax.ShapeDtypeStruct((T_ALL, D), F32), jax.ShapeDtypeStruct((T_ALL, D // 2), jnp.uint32),
                   jax.ShapeDtypeStruct((T_ALL, LANES), F32)],
        compiler_params=_params(("parallel",), 48),
        name=name,
    )(*a1_parts, *a2_parts, w_bf16, x, g_post, gate, g_pre, shift, scale, rw_hi, rw_lo, rb)


def _route_kernel(lg_ref, tri_ref, eidx_ref, gate_ref, rank_ref, cnt_ref, carry_ref):
    @pl.when(pl.program_id(0) == 0)
    def _():
        carry_ref[...] = jnp.zeros_like(carry_ref)

    shape = lg_ref.shape
    lane = lax.broadcasted_iota(jnp.int32, shape, 1).astype(F32)
    lg = jnp.where(lane < N_EXPERTS, lg_ref[...], -jnp.inf)
    multi = jnp.zeros(shape, F32)
    vals, sels = [], []
    eidx = jnp.zeros(shape, F32)
    for k in range(TOP_K):
        m = jnp.max(lg, axis=-1, keepdims=True)
        idx = jnp.min(jnp.where(lg == m, lane, float(LANES)), axis=-1, keepdims=True)
        sel = lane == idx
        multi = jnp.where(sel, 1.0, multi)
        lg = jnp.where(sel, -jnp.inf, lg)
        eidx = jnp.where(lane == k, idx, eidx)
        vals.append(m)
        sels.append(sel)
    exps = [jnp.exp(v - vals[0]) for v in vals]
    den = exps[0] + exps[1] + exps[2] + exps[3]
    rank_all = _dot(tri_ref[...], multi.astype(BF16)) + carry_ref[0:1]
    gate = jnp.zeros(shape, F32)
    rank = jnp.zeros(shape, F32)
    for k in range(TOP_K):
        gate = jnp.where(lane == k, exps[k] / den, gate)
        rank = jnp.where(lane == k, jnp.sum(jnp.where(sels[k], rank_all, 0.0), axis=-1, keepdims=True), rank)
    eidx_ref[...] = eidx.astype(jnp.int32)
    gate_ref[...] = gate
    rank_ref[...] = rank.astype(jnp.int32)
    carry_ref[0:1] = carry_ref[0:1] + jnp.sum(multi, axis=0, keepdims=True)
    cnt_ref[...] = carry_ref[...]


def _route(logits):
    tm = 256
    tri = (np.arange(tm)[:, None] > np.arange(tm)[None, :]).astype(np.float32)
    row = lambda i: (i, 0)
    spec = pl.BlockSpec((tm, LANES), row)
    return pl.pallas_call(
        _route_kernel,
        grid=(T_ALL // tm,),
        in_specs=[spec, pl.BlockSpec((tm, tm), lambda i: (0, 0))],
        out_specs=[spec, spec, spec, pl.BlockSpec((8, LANES), lambda i: (0, 0))],
        out_shape=[jax.ShapeDtypeStruct((T_ALL, LANES), jnp.int32), jax.ShapeDtypeStruct((T_ALL, LANES), F32),
                   jax.ShapeDtypeStruct((T_ALL, LANES), jnp.int32), jax.ShapeDtypeStruct((8, LANES), F32)],
        scratch_shapes=[pltpu.VMEM((8, LANES), F32)],
        compiler_params=_params(("arbitrary",), 32),
        name="route",
    )(logits, jnp.asarray(tri, BF16))


def _dest_kernel(eidx_ref, rank_ref, start_ref, o_ref):
    shape = eidx_ref.shape
    lane = lax.broadcasted_iota(jnp.int32, shape, 1).astype(F32)
    eidx = eidx_ref[...].astype(F32)
    starts = start_ref[...].astype(F32)
    dest = jnp.zeros(shape, F32)
    for k in range(TOP_K):
        e_k = jnp.sum(jnp.where(lane == k, eidx, 0.0), axis=-1, keepdims=True)
        start = jnp.sum(jnp.where(lane == e_k, starts, 0.0), axis=-1, keepdims=True)
        dest = jnp.where(lane == k, start, dest)
    o_ref[...] = jnp.where(lane < TOP_K, dest.astype(jnp.int32) + rank_ref[...], 0)


def _dest(eidx, rank, group_start):
    tm = 1024
    spec = pl.BlockSpec((tm, LANES), lambda i: (i, 0))
    return pl.pallas_call(
        _dest_kernel,
        grid=(T_ALL // tm,),
        in_specs=[spec, spec, pl.BlockSpec((1, LANES), lambda i: (0, 0))],
        out_specs=spec,
        out_shape=jax.ShapeDtypeStruct((T_ALL, LANES), jnp.int32),
        compiler_params=_params(("parallel",), 32),
        name="dest",
    )(eidx, rank, group_start)


def _dispatch_kernel(dest_ref, lo_ref, hi_ref, nu_ref, h_ref, o_ref, zero_ref, sem, zsem, *, tm):
    i = pl.program_id(0)
    base = i * tm * TOP_K

    @pl.when(i == 0)
    def _():
        zero_ref[...] = jnp.zeros_like(zero_ref)

    def issue(t, carry):
        for k in range(TOP_K):
            d = dest_ref[base + t * TOP_K + k]
            pltpu.make_async_copy(h_ref.at[pl.ds(t, 1)], o_ref.at[pl.ds(d, 1)], sem).start()
        return carry

    lax.fori_loop(0, tm, issue, 0, unroll=2)

    @pl.when(i < N_EXPERTS)
    def _():
        zrow = lambda r: pltpu.make_async_copy(zero_ref.at[pl.ds(0, 1)], o_ref.at[pl.ds(r, 1)], zsem)
        lax.fori_loop(lo_ref[i], hi_ref[i], lambda r, c: (zrow(r).start(), c)[1], 0)
        lax.fori_loop(lo_ref[i], hi_ref[i], lambda r, c: (zrow(r).wait(), c)[1], 0)

    blk = nu_ref[0] + i - N_EXPERTS

    @pl.when((i >= N_EXPERTS) & (blk < MOE_BLOCKS))
    def _():
        r0 = pl.multiple_of(blk * MOE_TM, MOE_TM)
        cp = pltpu.make_async_copy(zero_ref, o_ref.at[pl.ds(r0, MOE_TM)], zsem)
        cp.start()
        cp.wait()

    for k in range(TOP_K):
        pltpu.make_async_copy(h_ref, o_ref.at[pl.ds(0, tm)], sem).wait()


def _dispatch(dest_flat, pad_lo, pad_hi, n_used, h):
    tm = 128
    assert T_ALL // tm >= 2 * N_EXPERTS
    kern = functools.partial(_dispatch_kernel, tm=tm)
    return pl.pallas_call(
        kern,
        grid_spec=pltpu.PrefetchScalarGridSpec(
            num_scalar_prefetch=4,
            grid=(T_ALL // tm,),
            in_specs=[pl.BlockSpec((tm, D // 2), lambda i, *_: (i, 0))],
            out_specs=pl.BlockSpec(memory_space=pl.ANY),
            scratch_shapes=[pltpu.VMEM((MOE_TM, D // 2), jnp.uint32), pltpu.SemaphoreType.DMA,
                            pltpu.SemaphoreType.DMA]),
        out_shape=jax.ShapeDtypeStruct((MOE_ROWS, D // 2), jnp.uint32),
        compiler_params=_params(("arbitrary",), 32),
        name="dispatch",
    )(dest_flat, pad_lo, pad_hi, n_used, h)


def _new_weights(be_ref, i):
    return (i == 0) | (be_ref[i] != be_ref[jnp.maximum(i - 1, 0)])


def _expert_up_kernel(be_ref, nu_ref, x_ref, w_ref, b_ref, o_ref, wbf_ref):
    i = pl.program_id(1)
    tn = o_ref.shape[1]

    @pl.when(i < nu_ref[0])
    def _():
        @pl.when(_new_weights(be_ref, i))
        def _():
            wbf_ref[...] = w_ref[0, 0].astype(BF16)

        x = jnp.concatenate(_unpack_bf16_pairs(x_ref[...]), axis=1)
        even = lax.broadcasted_iota(jnp.int32, (x.shape[0], LANES), 1) % 2 == 0
        for c in range(tn // LANES):
            cols = slice(2 * LANES * c, 2 * LANES * (c + 1))
            hb = _dot(x, wbf_ref[:, cols]) + b_ref[0, 0, :, cols]
            first, second = hb[:, :LANES], hb[:, LANES:]
            g = jnp.where(even, first, pltpu.roll(second, 1, 1))
            lin = jnp.where(even, pltpu.roll(first, LANES - 1, 1), second)
            g = jnp.minimum(g, SWIGLU_LIMIT)
            lin = jnp.clip(lin, -SWIGLU_LIMIT, SWIGLU_LIMIT)
            act = g * jax.nn.sigmoid(SWIGLU_ALPHA * g) * (lin + 1.0)
            o_ref[:, LANES * c:LANES * (c + 1)] = act.astype(BF16)

    @pl.when(i >= nu_ref[0])
    def _():
        o_ref[...] = jnp.zeros_like(o_ref)


def _expert_up(blk_e, n_used, xs, w_gu, b_gu, layer):
    tn = FF_TILE
    blk = lambda j, i, be, nu: (jnp.minimum(i, nu[0] - 1), 0)
    return pl.pallas_call(
        _expert_up_kernel,
        grid_spec=pltpu.PrefetchScalarGridSpec(
            num_scalar_prefetch=2,
            grid=(D_FF // tn, MOE_BLOCKS),
            in_specs=[pl.BlockSpec((MOE_TM, D // 2), blk),
                      pl.BlockSpec((1, 1, D, 2 * tn), lambda j, i, be, nu: (layer, be[i], 0, j)),
                      pl.BlockSpec((1, 1, 1, 2 * tn), lambda j, i, be, nu: (layer, be[i], 0, j))],
            out_specs=pl.BlockSpec((MOE_TM, tn), lambda j, i, be, nu: (i, j)),
            scratch_shapes=[pltpu.VMEM((D, 2 * tn), BF16)]),
        out_shape=jax.ShapeDtypeStruct((MOE_ROWS, D_FF), BF16),
        compiler_params=_params(("arbitrary", "arbitrary"), 56),
        name="expert_up",
    )(blk_e, n_used, xs, w_gu, b_gu.reshape(N_LAYERS, N_EXPERTS, 1, 2 * D_FF))


def _expert_down_kernel(be_ref, nu_ref, a_ref, w_ref, b_ref, o_ref, wbf_ref, wtmp_ref):
    i = pl.program_id(1)

    @pl.when(i < nu_ref[0])
    def _():
        @pl.when(_new_weights(be_ref, i))
        def _():
            half = LANES // 2
            for g in range(D_FF // LANES):
                r0 = LANES * g
                for c in range(wbf_ref.shape[1] // LANES):
                    cols = slice(LANES * c, LANES * (c + 1))
                    wtmp_ref[c, pl.ds(0, half, stride=2), :] = w_ref[0, 0, r0:r0 + half, cols]
                    wtmp_ref[c, pl.ds(1, half, stride=2), :] = w_ref[0, 0, r0 + half:r0 + LANES, cols]
                    wbf_ref[r0:r0 + LANES, cols] = wtmp_ref[c].astype(BF16)

        o_ref[...] = _dot(a_ref[...], wbf_ref[...]) + b_ref[0, 0]

    @pl.when(i >= nu_ref[0])
    def _():
        o_ref[...] = jnp.zeros_like(o_ref)


def _expert_down(blk_e, n_used, a, w_down, b_down, layer):
    tn = D // 2
    blk = lambda j, i, be, nu: (jnp.minimum(i, nu[0] - 1), 0)
    return pl.pallas_call(
        _expert_down_kernel,
        grid_spec=pltpu.PrefetchScalarGridSpec(
            num_scalar_prefetch=2,
            grid=(D // tn, MOE_BLOCKS),
            in_specs=[pl.BlockSpec((MOE_TM, D_FF), blk),
                      pl.BlockSpec((1, 1, D_FF, tn), lambda j, i, be, nu: (layer, be[i], 0, j)),
                      pl.BlockSpec((1, 1, 1, tn), lambda j, i, be, nu: (layer, be[i], 0, j))],
            out_specs=pl.BlockSpec((MOE_TM, tn), lambda j, i, be, nu: (i, j)),
            scratch_shapes=[pltpu.VMEM((D_FF, tn), BF16), pltpu.VMEM((tn // LANES, LANES, LANES), F32)]),
        out_shape=jax.ShapeDtypeStruct((MOE_ROWS, D), F32),
        compiler_params=_params(("arbitrary", "arbitrary"), 48),
        name="expert_down",
    )(blk_e, n_used, a, w_down, b_down.reshape(N_LAYERS, N_EXPERTS, 1, D))


def _combine_kernel(dest_ref, y_ref, gate_ref, x_ref, gpost_ref, g2_ref, o_ref, buf_ref, sem, *, tm):
    i = pl.program_id(0)

    def gather(step, slot):
        base = step * tm * TOP_K

        def issue(t, carry):
            for k in range(TOP_K):
                d = dest_ref[base + t * TOP_K + k]
                pltpu.make_async_copy(y_ref.at[pl.ds(d, 1)], buf_ref.at[slot, k, pl.ds(t, 1)],
                                      sem.at[slot]).start()
            return carry

        lax.fori_loop(0, tm, issue, 0, unroll=2)

    @pl.when(i == 0)
    def _():
        gather(0, 0)

    @pl.when(i + 1 < pl.num_programs(0))
    def _():
        gather(i + 1, (i + 1) % 2)

    slot = i % 2
    for k in range(TOP_K):
        pltpu.make_async_copy(y_ref.at[pl.ds(0, tm)], buf_ref.at[slot, k], sem.at[slot]).wait()
    f = gate_ref[:, 0:1] * buf_ref[slot, 0]
    for k in range(1, TOP_K):
        f = f + gate_ref[:, k:k + 1] * buf_ref[slot, k]
    o_ref[...] = x_ref[...] + g2_ref[0] * (_rms(f) * gpost_ref[...])


def _combine(dest_flat, ys, gate, x, g_post, gate2, name):
    tm = 128
    kern = functools.partial(_combine_kernel, tm=tm)
    return pl.pallas_call(
        kern,
        grid_spec=pltpu.PrefetchScalarGridSpec(
            num_scalar_prefetch=1,
            grid=(T_ALL // tm,),
            in_specs=[pl.BlockSpec(memory_space=pl.ANY),
                      pl.BlockSpec((tm, LANES), lambda i, d: (i, 0)),
                      pl.BlockSpec((tm, D), lambda i, d: (i, 0)),
                      pl.BlockSpec((1, D), lambda i, d: (0, 0)),
                      pl.BlockSpec((1, 1, D), lambda i, d: (_mod_row(i * tm), 0, 0))],
            out_specs=pl.BlockSpec((tm, D), lambda i, d: (i, 0)),
            scratch_shapes=[pltpu.VMEM((2, TOP_K, tm, D), F32), pltpu.SemaphoreType.DMA((2,))]),
        out_shape=jax.ShapeDtypeStruct((T_ALL, D), F32),
        compiler_params=_params(("arbitrary",), 32),
        name=name,
    )(dest_flat, ys, gate, x, g_post, gate2)


def _moe(h, logits, x, g_post, gate2, w_gu, b_gu, w_down, b_down, layer):
    eidx, gate, rank, counts = _route(logits)
    cnt = counts[0, :N_EXPERTS].astype(jnp.int32)
    padded = (cnt + MOE_TM - 1) // MOE_TM * MOE_TM
    group_end = jnp.cumsum(padded)
    group_start = jnp.zeros((1, LANES), jnp.int32).at[0, :N_EXPERTS].set(group_end - padded)
    n_used = group_end[-1] // MOE_TM
    blk_row0 = jnp.arange(MOE_BLOCKS, dtype=jnp.int32) * MOE_TM
    blk_row0 = jnp.minimum(blk_row0, (n_used - 1) * MOE_TM)
    blk_e = jnp.sum(group_end[None, :] <= blk_row0[:, None], axis=1).astype(jnp.int32)
    n_used = n_used.astype(jnp.int32).reshape(1)
    dest = _dest(eidx, rank, group_start)[:, :TOP_K].reshape(N_ASSIGN)
    xs = _dispatch(dest, group_end - padded + cnt, group_end, n_used, h)
    a = _expert_up(blk_e, n_used, xs, w_gu, b_gu, layer)
    ys = _expert_down(blk_e, n_used, a, w_down, b_down, layer)
    return _combine(dest, ys, gate, x, g_post, gate2, f"combine_{layer}")


def kernel(x_prompt, x_sample, cache_k, cache_v, c, c_ctx, ada_w, ada_b, norm_mix_pre, norm_mix_post,
           norm_ffn_pre, norm_ffn_post, ab_in_w, ab_out_w, hy_short_w, hy_short_b, hy_w1, hy_b1, hy_w2,
           hy_b2, hy_w3, hy_freq, hy_bias, na_rpb, pool_in_w, pool_group_w, pool_scale, pool_out_w,
           router_w, router_b, exp_w_gu, exp_b_gu, exp_w_down, exp_b_down):
    x = jnp.concatenate([x_prompt.reshape(T_PROMPT, D), x_sample.reshape(T_LATENT, D)], axis=0)
    cvec = jnp.concatenate([c_ctx[None], c, jnp.zeros((8 - 1 - N_LATENT_SEQ, D), F32)], axis=0)
    mods = _adaln(cvec, ada_w, ada_b).reshape(N_LAYERS, 8, 6, D)

    new_k = new_v = None
    for layer in range(N_LAYERS):
        sh1, sc1, g1, sh2, sc2, g2 = [mods[layer, :, k][:, None, :] for k in range(6)]
        row = lambda a: a[layer].reshape(1, D)
        if layer % 2 == 0:
            j = layer // 2
            p = _modmm(x, row(norm_mix_pre), sh1, sc1, ab_in_w[j].astype(BF16), "ab_in")
            filt = (hy_w1[j], hy_b1[j], hy_w2[j], hy_b2[j], hy_w3[j], hy_freq[j])
            hy = (hy_short_w[j], hy_short_b[j], hy_bias[j])
            yh_p = _hyena(p, *hy, _hyena_filter_time(L_PROMPT, *filt), N_PROMPT_SEQ, L_PROMPT, 0)
            yh_s = _hyena(p, *hy, _hyena_filter_time(L_LATENT, *filt), N_LATENT_SEQ, L_LATENT,
                          T_PROMPT // L_LATENT)
            ya_p, new_k, new_v = _ctx_attn(p)
            ck = cache_k[:, j].reshape(N_LATENT_SEQ, -1, NA_DIM)
            cv = cache_v[:, j].reshape(N_LATENT_SEQ, -1, NA_DIM)
            ya_s = _na_attn(p, ck, cv, na_rpb[j])
            a1 = [yh_p, yh_s]
            a2 = [ya_p, ya_s]
            a1_col = a2_col = 0
            w_out = ab_out_w[j].astype(BF16)
        else:
            j = layer // 2
            u = _modmm(x, row(norm_mix_pre), sh1, sc1, pool_in_w[j].astype(BF16), "pool_in")
            a1 = a2 = [_pool(u, pool_group_w[j].astype(BF16), pool_scale[j])]
            a1_col, a2_col = 0, 1
            w_out = pool_out_w[j].astype(BF16)
        rw = jnp.zeros((D, LANES), F32).at[:, :N_EXPERTS].set(router_w[layer])
        rw_hi, rw_lo = _split_bf16(rw)
        rb = jnp.zeros((1, LANES), F32).at[0, :N_EXPERTS].set(router_b[layer])
        x, h, logits = _post_mixer(a1, a1_col, a2, a2_col, w_out, x, row(norm_mix_post), g1,
                                   row(norm_ffn_pre), sh2, sc2, rw_hi, rw_lo, rb, f"post_mixer_{layer}")
        x = _moe(h, logits, x, row(norm_ffn_post), g2, exp_w_gu, exp_b_gu, exp_w_down, exp_b_down, layer)

    y_prompt = x[:T_PROMPT].reshape(x_prompt.shape)
    y_sample = x[T_PROMPT:].reshape(x_sample.shape)
    kv_shape = (N_PROMPT_SEQ, 1, L_PROMPT, N_HEADS, HEAD_DIM)
    return y_prompt, y_sample, new_k.reshape(kv_shape), new_v.reshape(kv_shape)
```

```python
import functools
import math

import jax
import jax.numpy as jnp
import numpy as np
from jax import lax
from jax.experimental import pallas as pl
from jax.experimental.pallas import tpu as pltpu

F32 = jnp.float32
BF16 = jnp.bfloat16

D = 2048
N_PROMPT_SEQ = 32
L_PROMPT = 256
N_LATENT_SEQ = 4
L_LATENT = 2048
T_PROMPT = N_PROMPT_SEQ * L_PROMPT
T_LATENT = N_LATENT_SEQ * L_LATENT
T_ALL = T_PROMPT + T_LATENT
N_LAYERS = 2
GRID_W = 64
GRID_ROWS = L_LATENT // GRID_W
HY_DIM = 1024
HY_EMB = 33
HY_DECAY_TARGET = 1e-2
HY_FAST_DECAY = 0.3
HY_SLOW_DECAY = 1.5
N_HEADS = 16
HEAD_DIM = 64
NA_DIM = N_HEADS * HEAD_DIM
WIN_R = 8
WIN_C = 16
AB_IN = 3 * HY_DIM + 3 * NA_DIM
POOL_WINDOWS = (2, 4, 8, 16)
POOL_GROUP = D // len(POOL_WINDOWS)
POOL_HALO = 8
N_EXPERTS = 32
TOP_K = 4
D_FF = D
SWIGLU_LIMIT = 7.0
SWIGLU_ALPHA = 1.702
RMS_EPS = 1e-6
NEG_BIG = -1e30

LANES = 128
MOE_TM = 512
FF_TILE = 1024
N_ASSIGN = T_ALL * TOP_K
MOE_BLOCKS = -(-(N_ASSIGN + N_EXPERTS * (MOE_TM - 1)) // MOE_TM)
MOE_ROWS = MOE_BLOCKS * MOE_TM
MIB = 1 << 20


def _params(semantics, vmem_mib):
    return pltpu.CompilerParams(dimension_semantics=semantics, vmem_limit_bytes=vmem_mib * MIB)


def _mod_row(row0):
    return jnp.where(row0 < T_PROMPT, 0, 1 + (row0 - T_PROMPT) // L_LATENT)


def _rms(x):
    return x * lax.rsqrt(jnp.mean(x * x, axis=-1, keepdims=True) + RMS_EPS)


def _split_bf16(x):
    hi = x.astype(BF16)
    lo = (x - hi.astype(F32)).astype(BF16)
    return hi, lo


def _dot(a, b):
    return jnp.dot(a, b, preferred_element_type=F32)


def _pack_bf16_pairs(x_bf16):
    half = x_bf16.shape[1] // 2
    bits = lax.bitcast_convert_type(x_bf16.astype(F32), jnp.uint32)
    return (bits[:, :half] & jnp.uint32(0xFFFF0000)) | (bits[:, half:] >> 16)


def _unpack_bf16_pairs(packed):
    hi = lax.bitcast_convert_type(packed & jnp.uint32(0xFFFF0000), F32).astype(BF16)
    lo = lax.bitcast_convert_type(packed << 16, F32).astype(BF16)
    return hi, lo


def _adaln_kernel(cv_ref, w_ref, b_ref, o_ref):
    s = jax.nn.silu(cv_ref[...]).astype(BF16)
    o_ref[0] = _dot(s, w_ref[0].astype(BF16)) + b_ref[0]


def _adaln(cvec, ada_w, ada_b):
    n = ada_w.shape[-1]
    tn = 1024
    return pl.pallas_call(
        _adaln_kernel,
        grid=(N_LAYERS, n // tn),
        in_specs=[pl.BlockSpec((8, D), lambda l, j: (0, 0)),
                  pl.BlockSpec((1, D, tn), lambda l, j: (l, 0, j)),
                  pl.BlockSpec((1, 1, tn), lambda l, j: (l, 0, j))],
        out_specs=pl.BlockSpec((1, 8, tn), lambda l, j: (l, 0, j)),
        out_shape=jax.ShapeDtypeStruct((N_LAYERS, 8, n), F32),
        compiler_params=_params(("parallel", "arbitrary"), 40),
        name="adaln",
    )(cvec, ada_w, ada_b.reshape(N_LAYERS, 1, n))


def _modmm_kernel(x_ref, g_ref, sh_ref, sc_ref, w_ref, o_ref, h_ref):
    @pl.when(pl.program_id(1) == 0)
    def _():
        h = _rms(x_ref[...]) * g_ref[...] * (1.0 + sc_ref[0]) + sh_ref[0]
        h_ref[...] = h.astype(BF16)

    o_ref[...] = _dot(h_ref[...], w_ref[...])


def _modmm(x, g, shift, scale, w_bf16, name):
    n = w_bf16.shape[1]
    tm, tn = 1024, 1024
    mod_spec = pl.BlockSpec((1, 1, D), lambda i, j: (_mod_row(i * tm), 0, 0))
    return pl.pallas_call(
        _modmm_kernel,
        grid=(T_ALL // tm, n // tn),
        in_specs=[pl.BlockSpec((tm, D), lambda i, j: (i, 0)),
                  pl.BlockSpec((1, D), lambda i, j: (0, 0)),
                  mod_spec, mod_spec,
                  pl.BlockSpec((D, tn), lambda i, j: (0, j))],
        out_specs=pl.BlockSpec((tm, tn), lambda i, j: (i, j)),
        out_shape=jax.ShapeDtypeStruct((T_ALL, n), F32),
        scratch_shapes=[pltpu.VMEM((tm, D), BF16)],
        compiler_params=_params(("parallel", "arbitrary"), 48),
        name=name,
    )(x, g, shift, scale, w_bf16)


def _shortconv_kernel(z_ref, w_ref, b_ref, o_ref):
    z = z_ref[...]
    n = z.shape[0]
    row = lax.broadcasted_iota(jnp.int32, z.shape, 0)
    prev = jnp.where(row == 0, 0.0, pltpu.roll(z, 1, 0))
    nxt = jnp.where(row == n - 1, 0.0, pltpu.roll(z, n - 1, 0))
    o_ref[...] = w_ref[0:1] * prev + w_ref[1:2] * z + w_ref[2:3] * nxt + b_ref[...]


def _shortconv(p, w, b, n_seq, seq_len, row_block0):
    c = 3 * HY_DIM
    tc = 512 if seq_len > 512 else c
    return pl.pallas_call(
        _shortconv_kernel,
        grid=(n_seq, c // tc),
        in_specs=[pl.BlockSpec((seq_len, tc), lambda s, j: (row_block0 + s, j)),
                  pl.BlockSpec((3, tc), lambda s, j: (0, j)),
                  pl.BlockSpec((1, tc), lambda s, j: (0, j))],
        out_specs=pl.BlockSpec((seq_len, tc), lambda s, j: (s, j)),
        out_shape=jax.ShapeDtypeStruct((n_seq * seq_len, c), F32),
        compiler_params=_params(("parallel", "parallel"), 40),
        name=f"shortconv_{seq_len}",
    )(p, w, b.reshape(1, c))


def _seqmm_kernel(a_ref, x_ref, o_ref):
    o_ref[...] = _dot(a_ref[...], x_ref[...].astype(BF16))


def _seqmm(a_bf16, x, n_seq, col0, n_cols, name):
    m, k = a_bf16.shape
    tm = min(m, 1024)
    tn = 512 if k > 512 else 1024
    col_block0 = col0 // tn
    return pl.pallas_call(
        _seqmm_kernel,
        grid=(n_seq, n_cols // tn, m // tm),
        in_specs=[pl.BlockSpec((tm, k), lambda s, j, i: (i, 0)),
                  pl.BlockSpec((k, tn), lambda s, j, i: (s, col_block0 + j))],
        out_specs=pl.BlockSpec((tm, tn), lambda s, j, i: (s * (m // tm) + i, j)),
        out_shape=jax.ShapeDtypeStruct((n_seq * m, n_cols), F32),
        compiler_params=_params(("parallel", "parallel", "arbitrary"), 40),
        name=name,
    )(a_bf16, x)


def _hyena_inv_kernel(g_ref, y_ref, k_ref, yin_ref, gate_ref, bias_ref, o_ref, z_ref, *, seq_len):
    @pl.when(pl.program_id(2) == 0)
    def _():
        yc, ys = y_ref[0:seq_len], y_ref[seq_len:]
        kc, ks = k_ref[0:seq_len], k_ref[seq_len:]
        z_ref[0:seq_len] = (yc * kc - ys * ks).astype(BF16)
        z_ref[seq_len:] = (yc * ks + ys * kc).astype(BF16)

    conv = _dot(g_ref[...], z_ref[...]) * (1.0 / seq_len)
    o_ref[...] = gate_ref[...] * (conv + yin_ref[...] * bias_ref[...])


def _hyena_inv(g_bf16, yspec, kspec, order, yin, yin_col0, u, gate_col0, bias, n_seq, seq_len):
    tn = 256 if seq_len > 512 else 1024
    tt = min(seq_len, 512)
    nb = HY_DIM // tn
    kern = functools.partial(_hyena_inv_kernel, seq_len=seq_len)
    return pl.pallas_call(
        kern,
        grid=(n_seq, nb, seq_len // tt),
        in_specs=[pl.BlockSpec((tt, 2 * seq_len), lambda s, j, t: (t, 0)),
                  pl.BlockSpec((2 * seq_len, tn), lambda s, j, t: (s, j)),
                  pl.BlockSpec((2 * seq_len, tn), lambda s, j, t: (0, order * nb + j)),
                  pl.BlockSpec((tt, tn), lambda s, j, t: (s * (seq_len // tt) + t, yin_col0 // tn + j)),
                  pl.BlockSpec((tt, tn), lambda s, j, t: (s * (seq_len // tt) + t, gate_col0 // tn + j)),
                  pl.BlockSpec((1, tn), lambda s, j, t: (0, order * nb + j))],
        out_specs=pl.BlockSpec((tt, tn), lambda s, j, t: (s * (seq_len // tt) + t, j)),
        out_shape=jax.ShapeDtypeStruct((n_seq * seq_len, HY_DIM), F32),
        scratch_shapes=[pltpu.VMEM((2 * seq_len, tn), BF16)],
        compiler_params=_params(("parallel", "parallel", "arbitrary"), 48),
        name=f"hyena_inv_{seq_len}_{order}",
    )(g_bf16, yspec, kspec, yin, u, bias.reshape(1, 2 * HY_DIM))


def _dft_matrices(seq_len):
    f = lax.broadcasted_iota(jnp.int32, (seq_len, seq_len), 0)
    s = lax.broadcasted_iota(jnp.int32, (seq_len, seq_len), 1)
    ang = (((2 * f + 1) * s) % (4 * seq_len)).astype(F32) * (math.pi / (2 * seq_len))
    c, sn = jnp.cos(ang), jnp.sin(ang)
    fwd = jnp.concatenate([c, sn], axis=0).astype(BF16)
    inv = jnp.concatenate([c.T, sn.T], axis=1).astype(BF16)
    return fwd, inv


def _hyena_filter_time(seq_len, w1, b1, w2, b2, w3, freq):
    hp = lax.Precision.HIGHEST
    t = jnp.linspace(0.0, 1.0, seq_len, dtype=F32)[:, None]
    bands = (HY_EMB - 1) // 2
    ang = 2.0 * math.pi * jnp.arange(seq_len, dtype=F32)[:, None] / seq_len
    fb = jnp.linspace(1e-4, bands - 1, bands, dtype=F32)[None, :]
    z = jnp.concatenate([t, jnp.cos(fb * ang), -jnp.sin(fb * ang)], axis=-1)
    h = jnp.sin(freq * (jnp.dot(z, w1, precision=hp) + b1))
    h = jnp.sin(freq * (jnp.dot(h, w2, precision=hp) + b2))
    h = jnp.dot(h, w3, precision=hp).reshape(seq_len, 2, 2, HY_DIM)
    deltas = jnp.linspace(math.log(HY_DECAY_TARGET) / HY_SLOW_DECAY,
                          math.log(HY_DECAY_TARGET) / HY_FAST_DECAY, HY_DIM, dtype=F32)
    h = h * jnp.exp(-t * jnp.abs(deltas))[:, None, None, :]
    return h * lax.rsqrt(jnp.sum(h * h, axis=(0, 1), keepdims=True))


def _filter_spec_kernel(a_ref, hf_ref, hb_ref, o_ref, *, seq_len, tm):
    sign = jnp.where(pl.program_id(1) * tm < seq_len, 1.0, -1.0)
    hb = hb_ref[...]
    hb = jnp.where(lax.broadcasted_iota(jnp.int32, hb.shape, 0) == 0, 0.0, hb)
    hi, lo = _split_bf16(hf_ref[...] + sign * hb)
    o_ref[...] = _dot(a_ref[...], hi) + _dot(a_ref[...], lo)


def _hyena_filter_spec(h, fwd_bf16, seq_len):
    h2d = h.reshape(seq_len, 4 * HY_DIM)
    tm = min(seq_len, 1024)
    tn = 512
    nb = 2 * HY_DIM // tn
    kern = functools.partial(_filter_spec_kernel, seq_len=seq_len, tm=tm)
    return pl.pallas_call(
        kern,
        grid=(nb, 2 * seq_len // tm),
        in_specs=[pl.BlockSpec((tm, seq_len), lambda j, i: (i, 0)),
                  pl.BlockSpec((seq_len, tn), lambda j, i: (0, j)),
                  pl.BlockSpec((seq_len, tn), lambda j, i: (0, nb + j))],
        out_specs=pl.BlockSpec((tm, tn), lambda j, i: (i, j)),
        out_shape=jax.ShapeDtypeStruct((2 * seq_len, 2 * HY_DIM), F32),
        compiler_params=_params(("parallel", "arbitrary"), 40),
        name=f"filter_spec_{seq_len}",
    )(fwd_bf16, h2d, h2d)


def _hyena(p, short_w, short_b, bias, h_time, n_seq, seq_len, row_block0):
    fwd, inv = _dft_matrices(seq_len)
    kspec = _hyena_filter_spec(h_time, fwd, seq_len)
    u = _shortconv(p, short_w, short_b, n_seq, seq_len, row_block0)
    yspec = _seqmm(fwd, u, n_seq, 2 * HY_DIM, HY_DIM, f"hyena_fwd_{seq_len}_0")
    y1 = _hyena_inv(inv, yspec, kspec, 0, u, 2 * HY_DIM, u, 0, bias, n_seq, seq_len)
    yspec = _seqmm(fwd, y1, n_seq, 0, HY_DIM, f"hyena_fwd_{seq_len}_1")
    return _hyena_inv(inv, yspec, kspec, 1, y1, 0, u, HY_DIM, bias, n_seq, seq_len)


def _nt_dot(a, b):
    return lax.dot_general(a, b, (((1,), (1,)), ((), ())), preferred_element_type=F32)


def _ctx_attn_kernel(q_ref, k_ref, v_ref, o_ref, ko_ref, vo_ref):
    scale = HEAD_DIM ** -0.5
    ko_ref[...] = k_ref[...]
    vo_ref[...] = v_ref[...]
    for h in range(N_HEADS):
        sl = slice(h * HEAD_DIM, (h + 1) * HEAD_DIM)
        s = _nt_dot(q_ref[:, sl].astype(BF16), k_ref[:, sl].astype(BF16)) * scale
        e = jnp.exp(s - jnp.max(s, axis=-1, keepdims=True))
        pr = e / jnp.sum(e, axis=-1, keepdims=True)
        o_ref[:, sl] = _dot(pr.astype(BF16), v_ref[:, sl].astype(BF16))


def _ctx_attn(p):
    qb = 3 * HY_DIM // NA_DIM
    spec = lambda c: pl.BlockSpec((L_PROMPT, NA_DIM), lambda s: (s, c))
    out = jax.ShapeDtypeStruct((T_PROMPT, NA_DIM), F32)
    return pl.pallas_call(
        _ctx_attn_kernel,
        grid=(N_PROMPT_SEQ,),
        in_specs=[spec(qb), spec(qb + 1), spec(qb + 2)],
        out_specs=[spec(0), spec(0), spec(0)],
        out_shape=[out, out, out],
        compiler_params=_params(("parallel",), 40),
        name="ctx_attn",
    )(p, p, p)


NA_Q_ROWS = 8
NA_K_ROWS = NA_Q_ROWS + WIN_R
NA_ROW_BLOCKS = GRID_ROWS // NA_Q_ROWS
NA_HEADS_PER_STEP = LANES // HEAD_DIM


def _na_key_row0(rb, xp=jnp):
    return xp.clip(rb * NA_Q_ROWS - WIN_R // 2, 0, GRID_ROWS - NA_K_ROWS)


def _na_kernel(q_ref, k_ref, v_ref, kc_ref, vc_ref, b_ref, o_ref):
    scale = HEAD_DIM ** -0.5
    k0 = pl.multiple_of(_na_key_row0(pl.program_id(1)) * GRID_W, GRID_W)
    n_loc = NA_K_ROWS * GRID_W
    for h in range(NA_HEADS_PER_STEP):
        sl = slice(h * HEAD_DIM, (h + 1) * HEAD_DIM)
        q = q_ref[:, sl].astype(BF16)
        s_loc = _nt_dot(q, k_ref[pl.ds(k0, n_loc), sl].astype(BF16)) * scale + b_ref[0, 0, h]
        s_ctx = _nt_dot(q, kc_ref[0, :, sl].astype(BF16)) * scale
        m = jnp.maximum(jnp.max(s_loc, axis=-1, keepdims=True), jnp.max(s_ctx, axis=-1, keepdims=True))
        e_loc = jnp.exp(s_loc - m)
        e_ctx = jnp.exp(s_ctx - m)
        den = jnp.sum(e_loc, axis=-1, keepdims=True) + jnp.sum(e_ctx, axis=-1, keepdims=True)
        o_ref[:, sl] = (_dot((e_loc / den).astype(BF16), v_ref[pl.ds(k0, n_loc), sl].astype(BF16))
                        + _dot((e_ctx / den).astype(BF16), vc_ref[0, :, sl].astype(BF16)))


NA_BIAS_KINDS = 3


def _na_bias_kind(rb):
    return jnp.where(rb == 0, 0, jnp.where(rb == NA_ROW_BLOCKS - 1, 2, 1))


def _na_bias_table(rpb):
    q = np.arange(GRID_W)[:, None]
    kc = np.arange(GRID_W)[None, :]
    cs = np.clip(q - WIN_C // 2, 0, GRID_W - WIN_C)
    col_valid = (kc >= cs) & (kc < cs + WIN_C)
    col_off = np.clip(kc - q, -(WIN_C - 1), WIN_C - 1) + WIN_C - 1
    b = rpb.astype(F32)[:, :, col_off]
    b = jnp.where(jnp.asarray(col_valid)[None, None], b, NEG_BIG)
    neg = jnp.full_like(b, NEG_BIG)
    b_next = jnp.concatenate([b[:, 1:], neg[:, :1]], axis=1)
    pairs = jnp.stack([jnp.concatenate([b, b_next], axis=-1), jnp.concatenate([b, neg], axis=-1),
                       jnp.concatenate([neg, b], axis=-1)], axis=1)

    def geometry(rb):
        q_row = rb * NA_Q_ROWS + np.arange(NA_Q_ROWS)[:, None]
        k_row = _na_key_row0(rb, np) + np.arange(NA_K_ROWS)[None, :]
        row_start = np.clip(q_row - WIN_R // 2, 0, GRID_ROWS - WIN_R)
        row_valid = (k_row >= row_start) & (k_row < row_start + WIN_R)
        return row_valid, k_row - q_row + WIN_R - 1

    for rb in range(2, NA_ROW_BLOCKS - 1):
        assert all(np.array_equal(a, c) for a, c in zip(geometry(rb), geometry(1)))
    geoms = [geometry(rb) for rb in (0, 1, NA_ROW_BLOCKS - 1)]
    kern = functools.partial(_na_bias_kernel, geoms=geoms)
    return pl.pallas_call(
        kern,
        grid=(N_HEADS, NA_BIAS_KINDS),
        in_specs=[pl.BlockSpec((1,) + pairs.shape[1:], lambda h, kind: (h, 0, 0, 0, 0))],
        out_specs=pl.BlockSpec((1, 1, 1, NA_Q_ROWS * GRID_W, NA_K_ROWS * GRID_W),
                               lambda h, kind: (kind, h // NA_HEADS_PER_STEP, h % NA_HEADS_PER_STEP, 0, 0)),
        out_shape=jax.ShapeDtypeStruct((NA_BIAS_KINDS, N_HEADS // NA_HEADS_PER_STEP, NA_HEADS_PER_STEP,
                                        NA_Q_ROWS * GRID_W, NA_K_ROWS * GRID_W), F32),
        compiler_params=_params(("parallel", "arbitrary"), 32),
        name="na_bias",
    )(pairs)


def _na_bias_kernel(p_ref, o_ref, *, geoms):
    for kind, (row_valid, d) in enumerate(geoms):
        @pl.when(pl.program_id(1) == kind)
        def _(row_valid=row_valid, d=d):
            for iq in range(NA_Q_ROWS):
                rows = slice(iq * GRID_W, (iq + 1) * GRID_W)
                for ik in range(0, NA_K_ROWS, 2):
                    cols = slice(ik * GRID_W, (ik + 2) * GRID_W)
                    left, right = row_valid[iq, ik], row_valid[iq, ik + 1]
                    if left and right:
                        o_ref[0, 0, 0, rows, cols] = p_ref[0, 0, int(d[iq, ik])]
                    elif left:
                        o_ref[0, 0, 0, rows, cols] = p_ref[0, 1, int(d[iq, ik])]
                    elif right:
                        o_ref[0, 0, 0, rows, cols] = p_ref[0, 2, int(d[iq, ik + 1])]
                    else:
                        o_ref[0, 0, 0, rows, cols] = jnp.full((GRID_W, 2 * GRID_W), NEG_BIG, F32)


def _na_attn(p, cache_k, cache_v, rpb):
    tq = NA_Q_ROWS * GRID_W
    q_col0 = 3 * HY_DIM // LANES
    k_col0 = q_col0 + NA_DIM // LANES
    v_col0 = k_col0 + NA_DIM // LANES
    q_blk0 = T_PROMPT // tq
    seq0 = T_PROMPT // L_LATENT
    bias = _na_bias_table(rpb)
    ctx_spec = pl.BlockSpec((1, cache_k.shape[1], LANES), lambda hp, rb, b: (b, 0, hp))
    return pl.pallas_call(
        _na_kernel,
        grid=(N_HEADS // NA_HEADS_PER_STEP, NA_ROW_BLOCKS, N_LATENT_SEQ),
        in_specs=[pl.BlockSpec((tq, LANES), lambda hp, rb, b: (q_blk0 + b * NA_ROW_BLOCKS + rb, q_col0 + hp)),
                  pl.BlockSpec((L_LATENT, LANES), lambda hp, rb, b: (seq0 + b, k_col0 + hp)),
                  pl.BlockSpec((L_LATENT, LANES), lambda hp, rb, b: (seq0 + b, v_col0 + hp)),
                  ctx_spec, ctx_spec,
                  pl.BlockSpec((1, 1, NA_HEADS_PER_STEP, tq, NA_K_ROWS * GRID_W),
                               lambda hp, rb, b: (_na_bias_kind(rb), hp, 0, 0, 0))],
        out_specs=pl.BlockSpec((tq, LANES), lambda hp, rb, b: (b * NA_ROW_BLOCKS + rb, hp)),
        out_shape=jax.ShapeDtypeStruct((T_LATENT, NA_DIM), F32),
        compiler_params=_params(("parallel", "parallel", "arbitrary"), 48),
        name="na_attn",
    )(p, p, p, cache_k, cache_v, bias)


def _pool_kernel(prev_ref, cur_ref, next_ref, gw_ref, sc_ref, o_ref, ext_ref, *, tm):
    i = pl.program_id(0)
    row0 = i * tm
    seq_len = jnp.where(row0 < T_PROMPT, L_PROMPT, L_LATENT)
    pos0 = jnp.where(row0 < T_PROMPT, row0 % L_PROMPT, (row0 - T_PROMPT) % L_LATENT)
    first = pos0 == 0
    last = pos0 + tm == seq_len
    h = POOL_HALO
    ext_ref[0:h] = jnp.where(first, 0.0, prev_ref[...])
    ext_ref[h:h + tm] = cur_ref[...]
    ext_ref[h + tm:] = jnp.where(last, 0.0, next_ref[...])
    pos = pos0 + lax.broadcasted_iota(jnp.int32, (tm, 1), 0)
    for g, w in enumerate(POOL_WINDOWS):
        cols = slice(g * POOL_GROUP, (g + 1) * POOL_GROUP)
        acc = ext_ref[h - w // 2:h - w // 2 + tm, cols]
        for j in range(1, w):
            acc = acc + ext_ref[h - w // 2 + j:h - w // 2 + j + tm, cols]
        cnt = jnp.minimum(pos + w // 2, seq_len) - jnp.maximum(pos - w // 2, 0)
        d = acc / cnt.astype(F32) - cur_ref[:, cols]
        o_ref[:, cols] = _dot(d.astype(BF16), gw_ref[g]) * sc_ref[:, cols]


def _pool(u, group_w_bf16, scale):
    tm = 256
    hb = tm // POOL_HALO
    n_halo = T_ALL // POOL_HALO
    kern = functools.partial(_pool_kernel, tm=tm)
    return pl.pallas_call(
        kern,
        grid=(T_ALL // tm,),
        in_specs=[pl.BlockSpec((POOL_HALO, D), lambda i: (jnp.maximum(i * hb - 1, 0), 0)),
                  pl.BlockSpec((tm, D), lambda i: (i, 0)),
                  pl.BlockSpec((POOL_HALO, D), lambda i: (jnp.minimum((i + 1) * hb, n_halo - 1), 0)),
                  pl.BlockSpec((len(POOL_WINDOWS), POOL_GROUP, POOL_GROUP), lambda i: (0, 0, 0)),
                  pl.BlockSpec((1, D), lambda i: (0, 0))],
        out_specs=pl.BlockSpec((tm, D), lambda i: (i, 0)),
        out_shape=jax.ShapeDtypeStruct((T_ALL, D), F32),
        scratch_shapes=[pltpu.VMEM((tm + 2 * POOL_HALO, D), F32)],
        compiler_params=_params(("parallel",), 40),
        name="pool",
    )(u, u, u, group_w_bf16, scale.reshape(1, D))


def _post_mixer_kernel(*refs, n1, n2, tm):
    a1_refs, a2_refs = refs[:n1], refs[n1:n1 + n2]
    (w_ref, x_ref, gpost_ref, gate_ref, gpre_ref, sh_ref, sc_ref, rwh_ref, rwl_ref, rb_ref,
     xo_ref, h_ref, lg_ref) = refs[n1 + n2:]
    in_prompt = pl.program_id(0) * tm < T_PROMPT

    def rows(part_refs):
        if len(part_refs) == 1:
            return part_refs[0][...].astype(BF16)
        return jnp.where(in_prompt, part_refs[0][...], part_refs[1][...]).astype(BF16)

    a1, a2 = rows(a1_refs), rows(a2_refs)
    k1 = a1.shape[1]
    m = _dot(a1, w_ref[0:k1]) + _dot(a2, w_ref[k1:])
    xn = x_ref[...] + gate_ref[0] * (_rms(m) * gpost_ref[...])
    xo_ref[...] = xn
    h = _rms(xn) * gpre_ref[...] * (1.0 + sc_ref[0]) + sh_ref[0]
    hh, hl = _split_bf16(h)
    h_ref[...] = _pack_bf16_pairs(hh)
    lg_ref[...] = _dot(hh, rwh_ref[...]) + _dot(hl, rwh_ref[...]) + _dot(hh, rwl_ref[...]) + rb_ref[...]


def _post_mixer(a1_parts, a1_col, a2_parts, a2_col, w_bf16, x, g_post, gate, g_pre, shift, scale,
                rw_hi, rw_lo, rb, name):
    tm = 256
    kh = w_bf16.shape[0] // 2
    row = lambda i: (i, 0)
    const = lambda i: (0, 0)
    mod_spec = pl.BlockSpec((1, 1, D), lambda i: (_mod_row(i * tm), 0, 0))
    vec = pl.BlockSpec((1, D), const)
    n_prompt_tiles = T_PROMPT // tm

    def part_specs(parts, col):
        if len(parts) == 1:
            return [pl.BlockSpec((tm, kh), lambda i: (i, col))]
        return [pl.BlockSpec((tm, kh), lambda i: (jnp.minimum(i, n_prompt_tiles - 1), col)),
                pl.BlockSpec((tm, kh), lambda i: (jnp.maximum(i - n_prompt_tiles, 0), col))]

    kern = functools.partial(_post_mixer_kernel, n1=len(a1_parts), n2=len(a2_parts), tm=tm)
    return pl.pallas_call(
        kern,
        grid=(T_ALL // tm,),
        in_specs=part_specs(a1_parts, a1_col) + part_specs(a2_parts, a2_col) + [
                  pl.BlockSpec((2 * kh, D), const),
                  pl.BlockSpec((tm, D), row), vec, mod_spec, vec, mod_spec, mod_spec,
                  pl.BlockSpec((D, LANES), const), pl.BlockSpec((D, LANES), const),
                  pl.BlockSpec((1, LANES), const)],
        out_specs=[pl.BlockSpec((tm, D), row), pl.BlockSpec((tm, D // 2), row), pl.BlockSpec((tm, LANES), row)],
        out_shape=[jax.ShapeDtypeStruct((T_ALL, D), F32), jax.ShapeDtypeStruct((T_ALL, D // 2), jnp.uint32),
                   jax.ShapeDtypeStruct((T_ALL, LANES), F32)],
        compiler_params=_params(("parallel",), 48),
        name=name,
    )(*a1_parts, *a2_parts, w_bf16, x, g_post, gate, g_pre, shift, scale, rw_hi, rw_lo, rb)


def _route_kernel(lg_ref, tri_ref, eidx_ref, gate_ref, rank_ref, cnt_ref, carry_ref):
    @pl.when(pl.program_id(0) == 0)
    def _():
        carry_ref[...] = jnp.zeros_like(carry_ref)

    shape = lg_ref.shape
    lane = lax.broadcasted_iota(jnp.int32, shape, 1).astype(F32)
    lg = jnp.where(lane < N_EXPERTS, lg_ref[...], -jnp.inf)
    multi = jnp.zeros(shape, F32)
    vals, sels = [], []
    eidx = jnp.zeros(shape, F32)
    for k in range(TOP_K):
        m = jnp.max(lg, axis=-1, keepdims=True)
        idx = jnp.min(jnp.where(lg == m, lane, float(LANES)), axis=-1, keepdims=True)
        sel = lane == idx
        multi = jnp.where(sel, 1.0, multi)
        lg = jnp.where(sel, -jnp.inf, lg)
        eidx = jnp.where(lane == k, idx, eidx)
        vals.append(m)
        sels.append(sel)
    exps = [jnp.exp(v - vals[0]) for v in vals]
    den = exps[0] + exps[1] + exps[2] + exps[3]
    rank_all = _dot(tri_ref[...], multi.astype(BF16)) + carry_ref[0:1]
    gate = jnp.zeros(shape, F32)
    rank = jnp.zeros(shape, F32)
    for k in range(TOP_K):
        gate = jnp.where(lane == k, exps[k] / den, gate)
        rank = jnp.where(lane == k, jnp.sum(jnp.where(sels[k], rank_all, 0.0), axis=-1, keepdims=True), rank)
    eidx_ref[...] = eidx.astype(jnp.int32)
    gate_ref[...] = gate
    rank_ref[...] = rank.astype(jnp.int32)
    carry_ref[0:1] = carry_ref[0:1] + jnp.sum(multi, axis=0, keepdims=True)
    cnt_ref[...] = carry_ref[...]


def _route(logits):
    tm = 256
    tri = (np.arange(tm)[:, None] > np.arange(tm)[None, :]).astype(np.float32)
    row = lambda i: (i, 0)
    spec = pl.BlockSpec((tm, LANES), row)
    return pl.pallas_call(
        _route_kernel,
        grid=(T_ALL // tm,),
        in_specs=[spec, pl.BlockSpec((tm, tm), lambda i: (0, 0))],
        out_specs=[spec, spec, spec, pl.BlockSpec((8, LANES), lambda i: (0, 0))],
        out_shape=[jax.ShapeDtypeStruct((T_ALL, LANES), jnp.int32), jax.ShapeDtypeStruct((T_ALL, LANES), F32),
                   jax.ShapeDtypeStruct((T_ALL, LANES), jnp.int32), jax.ShapeDtypeStruct((8, LANES), F32)],
        scratch_shapes=[pltpu.VMEM((8, LANES), F32)],
        compiler_params=_params(("arbitrary",), 32),
        name="route",
    )(logits, jnp.asarray(tri, BF16))


def _dest_kernel(eidx_ref, rank_ref, start_ref, o_ref):
    shape = eidx_ref.shape
    lane = lax.broadcasted_iota(jnp.int32, shape, 1).astype(F32)
    eidx = eidx_ref[...].astype(F32)
    starts = start_ref[...].astype(F32)
    dest = jnp.zeros(shape, F32)
    for k in range(TOP_K):
        e_k = jnp.sum(jnp.where(lane == k, eidx, 0.0), axis=-1, keepdims=True)
        start = jnp.sum(jnp.where(lane == e_k, starts, 0.0), axis=-1, keepdims=True)
        dest = jnp.where(lane == k, start, dest)
    o_ref[...] = jnp.where(lane < TOP_K, dest.astype(jnp.int32) + rank_ref[...], 0)


def _dest(eidx, rank, group_start):
    tm = 1024
    spec = pl.BlockSpec((tm, LANES), lambda i: (i, 0))
    return pl.pallas_call(
        _dest_kernel,
        grid=(T_ALL // tm,),
        in_specs=[spec, spec, pl.BlockSpec((1, LANES), lambda i: (0, 0))],
        out_specs=spec,
        out_shape=jax.ShapeDtypeStruct((T_ALL, LANES), jnp.int32),
        compiler_params=_params(("parallel",), 32),
        name="dest",
    )(eidx, rank, group_start)


def _dispatch_kernel(dest_ref, lo_ref, hi_ref, nu_ref, h_ref, o_ref, zero_ref, sem, zsem, *, tm):
    i = pl.program_id(0)
    base = i * tm * TOP_K

    @pl.when(i == 0)
    def _():
        zero_ref[...] = jnp.zeros_like(zero_ref)

    def issue(t, carry):
        for k in range(TOP_K):
            d = dest_ref[base + t * TOP_K + k]
            pltpu.make_async_copy(h_ref.at[pl.ds(t, 1)], o_ref.at[pl.ds(d, 1)], sem).start()
        return carry

    lax.fori_loop(0, tm, issue, 0, unroll=2)

    @pl.when(i < N_EXPERTS)
    def _():
        lo = lo_ref[i]
        n_pad = hi_ref[i] - lo
        head = jnp.minimum((8 - lo % 8) % 8, n_pad)

        def zero_row(r, carry):
            cp = pltpu.make_async_copy(zero_ref.at[pl.ds(0, 1)], o_ref.at[pl.ds(lo + r, 1)], zsem)
            cp.start()
            cp.wait()
            return carry

        lax.fori_loop(0, head, zero_row, 0)
        pos = lo + head
        rest = n_pad - head
        size = MOE_TM // 2
        while size >= 8:
            @pl.when((rest & size) != 0)
            def _(pos=pos, size=size):
                cp = pltpu.make_async_copy(zero_ref.at[pl.ds(0, size)],
                                           o_ref.at[pl.ds(pl.multiple_of(pos, 8), size)], zsem)
                cp.start()
                cp.wait()

            pos = pos + (rest & size)
            size //= 2

    blk = nu_ref[0] + i - N_EXPERTS

    @pl.when((i >= N_EXPERTS) & (blk < MOE_BLOCKS))
    def _():
        r0 = pl.multiple_of(blk * MOE_TM, MOE_TM)
        cp = pltpu.make_async_copy(zero_ref, o_ref.at[pl.ds(r0, MOE_TM)], zsem)
        cp.start()
        cp.wait()

    for k in range(TOP_K):
        pltpu.make_async_copy(h_ref, o_ref.at[pl.ds(0, tm)], sem).wait()


def _dispatch(dest_flat, pad_lo, pad_hi, n_used, h):
    tm = 128
    assert T_ALL // tm >= 2 * N_EXPERTS
    kern = functools.partial(_dispatch_kernel, tm=tm)
    return pl.pallas_call(
        kern,
        grid_spec=pltpu.PrefetchScalarGridSpec(
            num_scalar_prefetch=4,
            grid=(T_ALL // tm,),
            in_specs=[pl.BlockSpec((tm, D // 2), lambda i, *_: (i, 0))],
            out_specs=pl.BlockSpec(memory_space=pl.ANY),
            scratch_shapes=[pltpu.VMEM((MOE_TM, D // 2), jnp.uint32), pltpu.SemaphoreType.DMA,
                            pltpu.SemaphoreType.DMA]),
        out_shape=jax.ShapeDtypeStruct((MOE_ROWS, D // 2), jnp.uint32),
        compiler_params=_params(("arbitrary",), 32),
        name="dispatch",
    )(dest_flat, pad_lo, pad_hi, n_used, h)


def _new_weights(be_ref, i):
    return (i == 0) | (be_ref[i] != be_ref[jnp.maximum(i - 1, 0)])


def _for_valid_rows(n_valid, compute):
    many = n_valid > MOE_TM // 2
    pl.when(many)(lambda: compute(MOE_TM))
    pl.when(jnp.logical_not(many))(lambda: compute(MOE_TM // 2))


def _expert_up_kernel(be_ref, nu_ref, bv_ref, x_ref, w_ref, b_ref, o_ref, wbf_ref):
    i = pl.program_id(1)
    tn = o_ref.shape[1]

    def compute(rows):
        x = jnp.concatenate(_unpack_bf16_pairs(x_ref[0:rows]), axis=1)
        even = lax.broadcasted_iota(jnp.int32, (rows, LANES), 1) % 2 == 0
        for c in range(tn // LANES):
            cols = slice(2 * LANES * c, 2 * LANES * (c + 1))
            hb = _dot(x, wbf_ref[:, cols]) + b_ref[0, 0, :, cols]
            first, second = hb[:, :LANES], hb[:, LANES:]
            g = jnp.where(even, first, pltpu.roll(second, 1, 1))
            lin = jnp.where(even, pltpu.roll(first, LANES - 1, 1), second)
            g = jnp.minimum(g, SWIGLU_LIMIT)
            lin = jnp.clip(lin, -SWIGLU_LIMIT, SWIGLU_LIMIT)
            act = g * jax.nn.sigmoid(SWIGLU_ALPHA * g) * (lin + 1.0)
            o_ref[0:rows, LANES * c:LANES * (c + 1)] = act.astype(BF16)
        if rows < MOE_TM:
            o_ref[rows:] = jnp.zeros((MOE_TM - rows, tn), BF16)

    @pl.when(i < nu_ref[0])
    def _():
        @pl.when(_new_weights(be_ref, i))
        def _():
            wbf_ref[...] = w_ref[0, 0].astype(BF16)

        _for_valid_rows(bv_ref[i], compute)

    @pl.when(i >= nu_ref[0])
    def _():
        o_ref[...] = jnp.zeros_like(o_ref)


def _expert_up(blk_e, n_used, blk_valid, xs, w_gu, b_gu, layer):
    tn = FF_TILE
    blk = lambda j, i, be, nu, bv: (jnp.minimum(i, nu[0] - 1), 0)
    return pl.pallas_call(
        _expert_up_kernel,
        grid_spec=pltpu.PrefetchScalarGridSpec(
            num_scalar_prefetch=3,
            grid=(D_FF // tn, MOE_BLOCKS),
            in_specs=[pl.BlockSpec((MOE_TM, D // 2), blk),
                      pl.BlockSpec((1, 1, D, 2 * tn), lambda j, i, be, nu, bv: (layer, be[i], 0, j)),
                      pl.BlockSpec((1, 1, 1, 2 * tn), lambda j, i, be, nu, bv: (layer, be[i], 0, j))],
            out_specs=pl.BlockSpec((MOE_TM, tn), lambda j, i, be, nu, bv: (i, j)),
            scratch_shapes=[pltpu.VMEM((D, 2 * tn), BF16)]),
        out_shape=jax.ShapeDtypeStruct((MOE_ROWS, D_FF), BF16),
        compiler_params=_params(("arbitrary", "arbitrary"), 56),
        name="expert_up",
    )(blk_e, n_used, blk_valid, xs, w_gu, b_gu.reshape(N_LAYERS, N_EXPERTS, 1, 2 * D_FF))


DOWN_TILE = D // 2


def _expert_down_kernel(be_ref, nu_ref, bv_ref, a_ref, w_ref, b_ref, o_ref, wbf_ref, wtmp_ref):
    i = pl.program_id(1)

    def compute(rows):
        y = _dot(a_ref[0:rows], wbf_ref[...]) + b_ref[0, 0]
        o_ref[0:rows] = _pack_bf16_pairs(y.astype(BF16))
        if rows < MOE_TM:
            o_ref[rows:] = jnp.zeros((MOE_TM - rows, o_ref.shape[1]), jnp.uint32)

    @pl.when(i < nu_ref[0])
    def _():
        @pl.when(_new_weights(be_ref, i))
        def _():
            half = LANES // 2
            for g in range(D_FF // LANES):
                r0 = LANES * g
                for c in range(wbf_ref.shape[1] // LANES):
                    cols = slice(LANES * c, LANES * (c + 1))
                    wtmp_ref[c, pl.ds(0, half, stride=2), :] = w_ref[0, 0, r0:r0 + half, cols]
                    wtmp_ref[c, pl.ds(1, half, stride=2), :] = w_ref[0, 0, r0 + half:r0 + LANES, cols]
                    wbf_ref[r0:r0 + LANES, cols] = wtmp_ref[c].astype(BF16)

        _for_valid_rows(bv_ref[i], compute)

    @pl.when(i >= nu_ref[0])
    def _():
        o_ref[...] = jnp.zeros_like(o_ref)


def _expert_down(blk_e, n_used, blk_valid, a, w_down, b_down, layer):
    tn = DOWN_TILE
    blk = lambda j, i, be, nu, bv: (jnp.minimum(i, nu[0] - 1), 0)
    return pl.pallas_call(
        _expert_down_kernel,
        grid_spec=pltpu.PrefetchScalarGridSpec(
            num_scalar_prefetch=3,
            grid=(D // tn, MOE_BLOCKS),
            in_specs=[pl.BlockSpec((MOE_TM, D_FF), blk),
                      pl.BlockSpec((1, 1, D_FF, tn), lambda j, i, be, nu, bv: (layer, be[i], 0, j)),
                      pl.BlockSpec((1, 1, 1, tn), lambda j, i, be, nu, bv: (layer, be[i], 0, j))],
            out_specs=pl.BlockSpec((MOE_TM, tn // 2), lambda j, i, be, nu, bv: (i, j)),
            scratch_shapes=[pltpu.VMEM((D_FF, tn), BF16), pltpu.VMEM((tn // LANES, LANES, LANES), F32)]),
        out_shape=jax.ShapeDtypeStruct((MOE_ROWS, D // 2), jnp.uint32),
        compiler_params=_params(("arbitrary", "arbitrary"), 48),
        name="expert_down",
    )(blk_e, n_used, blk_valid, a, w_down, b_down.reshape(N_LAYERS, N_EXPERTS, 1, D))


def _combine_kernel(dest_ref, y_ref, gate_ref, x_ref, gpost_ref, g2_ref, o_ref, buf_ref, sem, *, tm):
    i = pl.program_id(0)

    def gather(step, slot):
        base = step * tm * TOP_K

        def issue(t, carry):
            for k in range(TOP_K):
                d = dest_ref[base + t * TOP_K + k]
                pltpu.make_async_copy(y_ref.at[pl.ds(d, 1)], buf_ref.at[slot, k, pl.ds(t, 1)],
                                      sem.at[slot]).start()
            return carry

        lax.fori_loop(0, tm, issue, 0, unroll=2)

    @pl.when(i == 0)
    def _():
        gather(0, 0)

    @pl.when(i + 1 < pl.num_programs(0))
    def _():
        gather(i + 1, (i + 1) % 2)

    slot = i % 2
    for k in range(TOP_K):
        pltpu.make_async_copy(y_ref.at[pl.ds(0, tm)], buf_ref.at[slot, k], sem.at[slot]).wait()

    def expert_rows(k):
        hi, lo = _unpack_bf16_pairs(buf_ref[slot, k])
        tw = DOWN_TILE // 2
        parts = []
        for j in range(D // DOWN_TILE):
            parts += [hi[:, j * tw:(j + 1) * tw], lo[:, j * tw:(j + 1) * tw]]
        return jnp.concatenate(parts, axis=1).astype(F32)

    f = gate_ref[:, 0:1] * expert_rows(0)
    for k in range(1, TOP_K):
        f = f + gate_ref[:, k:k + 1] * expert_rows(k)
    o_ref[...] = x_ref[...] + g2_ref[0] * (_rms(f) * gpost_ref[...])


def _combine(dest_flat, ys, gate, x, g_post, gate2, name):
    tm = 128
    kern = functools.partial(_combine_kernel, tm=tm)
    return pl.pallas_call(
        kern,
        grid_spec=pltpu.PrefetchScalarGridSpec(
            num_scalar_prefetch=1,
            grid=(T_ALL // tm,),
            in_specs=[pl.BlockSpec(memory_space=pl.ANY),
                      pl.BlockSpec((tm, LANES), lambda i, d: (i, 0)),
                      pl.BlockSpec((tm, D), lambda i, d: (i, 0)),
                      pl.BlockSpec((1, D), lambda i, d: (0, 0)),
                      pl.BlockSpec((1, 1, D), lambda i, d: (_mod_row(i * tm), 0, 0))],
            out_specs=pl.BlockSpec((tm, D), lambda i, d: (i, 0)),
            scratch_shapes=[pltpu.VMEM((2, TOP_K, tm, D // 2), jnp.uint32), pltpu.SemaphoreType.DMA((2,))]),
        out_shape=jax.ShapeDtypeStruct((T_ALL, D), F32),
        compiler_params=_params(("arbitrary",), 32),
        name=name,
    )(dest_flat, ys, gate, x, g_post, gate2)


def _moe(h, logits, x, g_post, gate2, w_gu, b_gu, w_down, b_down, layer):
    eidx, gate, rank, counts = _route(logits)
    cnt = counts[0, :N_EXPERTS].astype(jnp.int32)
    padded = (cnt + MOE_TM - 1) // MOE_TM * MOE_TM
    group_end = jnp.cumsum(padded)
    group_start = jnp.zeros((1, LANES), jnp.int32).at[0, :N_EXPERTS].set(group_end - padded)
    n_used = group_end[-1] // MOE_TM
    blk_row0 = jnp.arange(MOE_BLOCKS, dtype=jnp.int32) * MOE_TM
    blk_row0 = jnp.minimum(blk_row0, (n_used - 1) * MOE_TM)
    blk_e = jnp.sum(group_end[None, :] <= blk_row0[:, None], axis=1).astype(jnp.int32)
    n_used = n_used.astype(jnp.int32).reshape(1)
    filled_end = group_end - padded + cnt
    blk_valid = jnp.clip(filled_end[jnp.minimum(blk_e, N_EXPERTS - 1)] - blk_row0, 0, MOE_TM).astype(jnp.int32)
    dest = _dest(eidx, rank, group_start)[:, :TOP_K].reshape(N_ASSIGN)
    xs = _dispatch(dest, filled_end, group_end, n_used, h)
    a = _expert_up(blk_e, n_used, blk_valid, xs, w_gu, b_gu, layer)
    ys = _expert_down(blk_e, n_used, blk_valid, a, w_down, b_down, layer)
    return _combine(dest, ys, gate, x, g_post, gate2, f"combine_{layer}")


def kernel(x_prompt, x_sample, cache_k, cache_v, c, c_ctx, ada_w, ada_b, norm_mix_pre, norm_mix_post,
           norm_ffn_pre, norm_ffn_post, ab_in_w, ab_out_w, hy_short_w, hy_short_b, hy_w1, hy_b1, hy_w2,
           hy_b2, hy_w3, hy_freq, hy_bias, na_rpb, pool_in_w, pool_group_w, pool_scale, pool_out_w,
           router_w, router_b, exp_w_gu, exp_b_gu, exp_w_down, exp_b_down):
    x = jnp.concatenate([x_prompt.reshape(T_PROMPT, D), x_sample.reshape(T_LATENT, D)], axis=0)
    cvec = jnp.concatenate([c_ctx[None], c, jnp.zeros((8 - 1 - N_LATENT_SEQ, D), F32)], axis=0)
    mods = _adaln(cvec, ada_w, ada_b).reshape(N_LAYERS, 8, 6, D)

    new_k = new_v = None
    for layer in range(N_LAYERS):
        sh1, sc1, g1, sh2, sc2, g2 = [mods[layer, :, k][:, None, :] for k in range(6)]
        row = lambda a: a[layer].reshape(1, D)
        if layer % 2 == 0:
            j = layer // 2
            p = _modmm(x, row(norm_mix_pre), sh1, sc1, ab_in_w[j].astype(BF16), "ab_in")
            filt = (hy_w1[j], hy_b1[j], hy_w2[j], hy_b2[j], hy_w3[j], hy_freq[j])
            hy = (hy_short_w[j], hy_short_b[j], hy_bias[j])
            yh_p = _hyena(p, *hy, _hyena_filter_time(L_PROMPT, *filt), N_PROMPT_SEQ, L_PROMPT, 0)
            yh_s = _hyena(p, *hy, _hyena_filter_time(L_LATENT, *filt), N_LATENT_SEQ, L_LATENT,
                          T_PROMPT // L_LATENT)
            ya_p, new_k, new_v = _ctx_attn(p)
            ck = cache_k[:, j].reshape(N_LATENT_SEQ, -1, NA_DIM)
            cv = cache_v[:, j].reshape(N_LATENT_SEQ, -1, NA_DIM)
            ya_s = _na_attn(p, ck, cv, na_rpb[j])
            a1 = [yh_p, yh_s]
            a2 = [ya_p, ya_s]
            a1_col = a2_col = 0
            w_out = ab_out_w[j].astype(BF16)
        else:
            j = layer // 2
            u = _modmm(x, row(norm_mix_pre), sh1, sc1, pool_in_w[j].astype(BF16), "pool_in")
            a1 = a2 = [_pool(u, pool_group_w[j].astype(BF16), pool_scale[j])]
            a1_col, a2_col = 0, 1
            w_out = pool_out_w[j].astype(BF16)
        rw = jnp.zeros((D, LANES), F32).at[:, :N_EXPERTS].set(router_w[layer])
        rw_hi, rw_lo = _split_bf16(rw)
        rb = jnp.zeros((1, LANES), F32).at[0, :N_EXPERTS].set(router_b[layer])
        x, h, logits = _post_mixer(a1, a1_col, a2, a2_col, w_out, x, row(norm_mix_post), g1,
                                   row(norm_ffn_pre), sh2, sc2, rw_hi, rw_lo, rb, f"post_mixer_{layer}")
        x = _moe(h, logits, x, row(norm_ffn_post), g2, exp_w_gu, exp_b_gu, exp_w_down, exp_b_down, layer)

    y_prompt = x[:T_PROMPT].reshape(x_prompt.shape)
    y_sample = x[T_PROMPT:].reshape(x_sample.shape)
    kv_shape = (N_PROMPT_SEQ, 1, L_PROMPT, N_HEADS, HEAD_DIM)
    return y_prompt, y_sample, new_k.reshape(kv_shape), new_v.reshape(kv_shape)
```

```python
import functools
import math

import jax
import jax.numpy as jnp
import numpy as np
from jax import lax
from jax.experimental import pallas as pl
from jax.experimental.pallas import tpu as pltpu

F32 = jnp.float32
BF16 = jnp.bfloat16

D = 2048
N_PROMPT_SEQ = 32
L_PROMPT = 256
N_LATENT_SEQ = 4
L_LATENT = 2048
T_PROMPT = N_PROMPT_SEQ * L_PROMPT
T_LATENT = N_LATENT_SEQ * L_LATENT
T_ALL = T_PROMPT + T_LATENT
N_LAYERS = 2
GRID_W = 64
GRID_ROWS = L_LATENT // GRID_W
HY_DIM = 1024
HY_EMB = 33
HY_DECAY_TARGET = 1e-2
HY_FAST_DECAY = 0.3
HY_SLOW_DECAY = 1.5
N_HEADS = 16
HEAD_DIM = 64
NA_DIM = N_HEADS * HEAD_DIM
WIN_R = 8
WIN_C = 16
AB_IN = 3 * HY_DIM + 3 * NA_DIM
POOL_WINDOWS = (2, 4, 8, 16)
POOL_GROUP = D // len(POOL_WINDOWS)
POOL_HALO = 8
N_EXPERTS = 32
TOP_K = 4
D_FF = D
SWIGLU_LIMIT = 7.0
SWIGLU_ALPHA = 1.702
RMS_EPS = 1e-6
NEG_BIG = -1e30

LANES = 128
MOE_TM = 512
FF_TILE = 1024
N_ASSIGN = T_ALL * TOP_K
MOE_BLOCKS = -(-(N_ASSIGN + N_EXPERTS * (MOE_TM - 1)) // MOE_TM)
MOE_ROWS = MOE_BLOCKS * MOE_TM
MIB = 1 << 20


def _params(semantics, vmem_mib):
    return pltpu.CompilerParams(dimension_semantics=semantics, vmem_limit_bytes=vmem_mib * MIB)


def _mod_row(row0):
    return jnp.where(row0 < T_PROMPT, 0, 1 + (row0 - T_PROMPT) // L_LATENT)


def _rms(x):
    return x * lax.rsqrt(jnp.mean(x * x, axis=-1, keepdims=True) + RMS_EPS)


def _split_bf16(x):
    hi = x.astype(BF16)
    lo = (x - hi.astype(F32)).astype(BF16)
    return hi, lo


def _dot(a, b):
    return jnp.dot(a, b, preferred_element_type=F32)


def _pack_bf16_pairs(x_bf16):
    half = x_bf16.shape[1] // 2
    bits = lax.bitcast_convert_type(x_bf16.astype(F32), jnp.uint32)
    return (bits[:, :half] & jnp.uint32(0xFFFF0000)) | (bits[:, half:] >> 16)


def _unpack_bf16_pairs(packed):
    hi = lax.bitcast_convert_type(packed & jnp.uint32(0xFFFF0000), F32).astype(BF16)
    lo = lax.bitcast_convert_type(packed << 16, F32).astype(BF16)
    return hi, lo


def _adaln_kernel(cv_ref, w_ref, b_ref, o_ref):
    s = jax.nn.silu(cv_ref[...]).astype(BF16)
    o_ref[0] = _dot(s, w_ref[0].astype(BF16)) + b_ref[0]


def _adaln(cvec, ada_w, ada_b):
    n = ada_w.shape[-1]
    tn = 1024
    return pl.pallas_call(
        _adaln_kernel,
        grid=(N_LAYERS, n // tn),
        in_specs=[pl.BlockSpec((8, D), lambda l, j: (0, 0)),
                  pl.BlockSpec((1, D, tn), lambda l, j: (l, 0, j)),
                  pl.BlockSpec((1, 1, tn), lambda l, j: (l, 0, j))],
        out_specs=pl.BlockSpec((1, 8, tn), lambda l, j: (l, 0, j)),
        out_shape=jax.ShapeDtypeStruct((N_LAYERS, 8, n), F32),
        compiler_params=_params(("parallel", "arbitrary"), 40),
        name="adaln",
    )(cvec, ada_w, ada_b.reshape(N_LAYERS, 1, n))


def _token_specs(parts, tm, width, col):
    if len(parts) == 1:
        return [pl.BlockSpec((tm, width), lambda i, *_: (i, col))]
    n_prompt_tiles = T_PROMPT // tm
    return [pl.BlockSpec((tm, width), lambda i, *_: (jnp.minimum(i, n_prompt_tiles - 1), col)),
            pl.BlockSpec((tm, width), lambda i, *_: (jnp.maximum(i - n_prompt_tiles, 0), col))]


def _token_rows(part_refs, tm):
    if len(part_refs) == 1:
        return part_refs[0][...]
    return jnp.where(pl.program_id(0) * tm < T_PROMPT, part_refs[0][...], part_refs[1][...])


def _modmm_kernel(*refs, n_x, tm):
    x_refs = refs[:n_x]
    g_ref, sh_ref, sc_ref, w_ref, o_ref, h_ref = refs[n_x:]

    @pl.when(pl.program_id(1) == 0)
    def _():
        h = _rms(_token_rows(x_refs, tm)) * g_ref[...] * (1.0 + sc_ref[0]) + sh_ref[0]
        h_ref[...] = h.astype(BF16)

    o_ref[...] = _dot(h_ref[...], w_ref[...])


def _modmm(x_parts, g, shift, scale, w_bf16, name):
    n = w_bf16.shape[1]
    tm, tn = 1024 // len(x_parts), 1024
    mod_spec = pl.BlockSpec((1, 1, D), lambda i, j: (_mod_row(i * tm), 0, 0))
    return pl.pallas_call(
        functools.partial(_modmm_kernel, n_x=len(x_parts), tm=tm),
        grid=(T_ALL // tm, n // tn),
        in_specs=_token_specs(x_parts, tm, D, 0) + [
                  pl.BlockSpec((1, D), lambda i, j: (0, 0)),
                  mod_spec, mod_spec,
                  pl.BlockSpec((D, tn), lambda i, j: (0, j))],
        out_specs=pl.BlockSpec((tm, tn), lambda i, j: (i, j)),
        out_shape=jax.ShapeDtypeStruct((T_ALL, n), F32),
        scratch_shapes=[pltpu.VMEM((tm, D), BF16)],
        compiler_params=_params(("parallel", "arbitrary"), 48),
        name=name,
    )(*x_parts, g, shift, scale, w_bf16)


def _shortconv_kernel(z_ref, w_ref, b_ref, o_ref):
    z = z_ref[...]
    n = z.shape[0]
    row = lax.broadcasted_iota(jnp.int32, z.shape, 0)
    prev = jnp.where(row == 0, 0.0, pltpu.roll(z, 1, 0))
    nxt = jnp.where(row == n - 1, 0.0, pltpu.roll(z, n - 1, 0))
    o_ref[...] = w_ref[0:1] * prev + w_ref[1:2] * z + w_ref[2:3] * nxt + b_ref[...]


def _shortconv(p, w, b, n_seq, seq_len, row_block0):
    c = 3 * HY_DIM
    tc = 512 if seq_len > 512 else c
    return pl.pallas_call(
        _shortconv_kernel,
        grid=(n_seq, c // tc),
        in_specs=[pl.BlockSpec((seq_len, tc), lambda s, j: (row_block0 + s, j)),
                  pl.BlockSpec((3, tc), lambda s, j: (0, j)),
                  pl.BlockSpec((1, tc), lambda s, j: (0, j))],
        out_specs=pl.BlockSpec((seq_len, tc), lambda s, j: (s, j)),
        out_shape=jax.ShapeDtypeStruct((n_seq * seq_len, c), F32),
        compiler_params=_params(("parallel", "parallel"), 40),
        name=f"shortconv_{seq_len}",
    )(p, w, b.reshape(1, c))


def _seqmm_kernel(a_ref, x_ref, o_ref):
    o_ref[...] = _dot(a_ref[...], x_ref[...].astype(BF16))


def _seqmm(a_bf16, x, n_seq, col0, n_cols, name):
    m, k = a_bf16.shape
    tm = min(m, 1024)
    tn = 512 if k > 512 else 1024
    col_block0 = col0 // tn
    return pl.pallas_call(
        _seqmm_kernel,
        grid=(n_seq, n_cols // tn, m // tm),
        in_specs=[pl.BlockSpec((tm, k), lambda s, j, i: (i, 0)),
                  pl.BlockSpec((k, tn), lambda s, j, i: (s, col_block0 + j))],
        out_specs=pl.BlockSpec((tm, tn), lambda s, j, i: (s * (m // tm) + i, j)),
        out_shape=jax.ShapeDtypeStruct((n_seq * m, n_cols), F32),
        compiler_params=_params(("parallel", "parallel", "arbitrary"), 40),
        name=name,
    )(a_bf16, x)


def _hyena_inv_kernel(g_ref, y_ref, k_ref, yin_ref, gate_ref, bias_ref, o_ref, z_ref, *, seq_len):
    @pl.when(pl.program_id(2) == 0)
    def _():
        yc, ys = y_ref[0:seq_len], y_ref[seq_len:]
        kc, ks = k_ref[0:seq_len], k_ref[seq_len:]
        z_ref[0:seq_len] = (yc * kc - ys * ks).astype(BF16)
        z_ref[seq_len:] = (yc * ks + ys * kc).astype(BF16)

    conv = _dot(g_ref[...], z_ref[...]) * (1.0 / seq_len)
    o_ref[...] = gate_ref[...] * (conv + yin_ref[...] * bias_ref[...])


def _hyena_inv(g_bf16, yspec, kspec, order, yin, yin_col0, u, gate_col0, bias, n_seq, seq_len):
    tn = 256 if seq_len > 512 else 1024
    tt = min(seq_len, 512)
    nb = HY_DIM // tn
    kern = functools.partial(_hyena_inv_kernel, seq_len=seq_len)
    return pl.pallas_call(
        kern,
        grid=(n_seq, nb, seq_len // tt),
        in_specs=[pl.BlockSpec((tt, 2 * seq_len), lambda s, j, t: (t, 0)),
                  pl.BlockSpec((2 * seq_len, tn), lambda s, j, t: (s, j)),
                  pl.BlockSpec((2 * seq_len, tn), lambda s, j, t: (0, order * nb + j)),
                  pl.BlockSpec((tt, tn), lambda s, j, t: (s * (seq_len // tt) + t, yin_col0 // tn + j)),
                  pl.BlockSpec((tt, tn), lambda s, j, t: (s * (seq_len // tt) + t, gate_col0 // tn + j)),
                  pl.BlockSpec((1, tn), lambda s, j, t: (0, order * nb + j))],
        out_specs=pl.BlockSpec((tt, tn), lambda s, j, t: (s * (seq_len // tt) + t, j)),
        out_shape=jax.ShapeDtypeStruct((n_seq * seq_len, HY_DIM), F32),
        scratch_shapes=[pltpu.VMEM((2 * seq_len, tn), BF16)],
        compiler_params=_params(("parallel", "parallel", "arbitrary"), 48),
        name=f"hyena_inv_{seq_len}_{order}",
    )(g_bf16, yspec, kspec, yin, u, bias.reshape(1, 2 * HY_DIM))


def _dft_matrices(seq_len):
    f = lax.broadcasted_iota(jnp.int32, (seq_len, seq_len), 0)
    s = lax.broadcasted_iota(jnp.int32, (seq_len, seq_len), 1)
    ang = (((2 * f + 1) * s) % (4 * seq_len)).astype(F32) * (math.pi / (2 * seq_len))
    c, sn = jnp.cos(ang), jnp.sin(ang)
    fwd = jnp.concatenate([c, sn], axis=0).astype(BF16)
    inv = jnp.concatenate([c.T, sn.T], axis=1).astype(BF16)
    return fwd, inv


def _hyena_filter_time(seq_len, w1, b1, w2, b2, w3, freq):
    hp = lax.Precision.HIGHEST
    t = jnp.linspace(0.0, 1.0, seq_len, dtype=F32)[:, None]
    bands = (HY_EMB - 1) // 2
    ang = 2.0 * math.pi * jnp.arange(seq_len, dtype=F32)[:, None] / seq_len
    fb = jnp.linspace(1e-4, bands - 1, bands, dtype=F32)[None, :]
    z = jnp.concatenate([t, jnp.cos(fb * ang), -jnp.sin(fb * ang)], axis=-1)
    h = jnp.sin(freq * (jnp.dot(z, w1, precision=hp) + b1))
    h = jnp.sin(freq * (jnp.dot(h, w2, precision=hp) + b2))
    h = jnp.dot(h, w3, precision=hp).reshape(seq_len, 2, 2, HY_DIM)
    deltas = jnp.linspace(math.log(HY_DECAY_TARGET) / HY_SLOW_DECAY,
                          math.log(HY_DECAY_TARGET) / HY_FAST_DECAY, HY_DIM, dtype=F32)
    h = h * jnp.exp(-t * jnp.abs(deltas))[:, None, None, :]
    return h * lax.rsqrt(jnp.sum(h * h, axis=(0, 1), keepdims=True))


def _filter_spec_kernel(a_ref, hf_ref, hb_ref, o_ref, *, seq_len, tm):
    sign = jnp.where(pl.program_id(1) * tm < seq_len, 1.0, -1.0)
    hb = hb_ref[...]
    hb = jnp.where(lax.broadcasted_iota(jnp.int32, hb.shape, 0) == 0, 0.0, hb)
    hi, lo = _split_bf16(hf_ref[...] + sign * hb)
    o_ref[...] = _dot(a_ref[...], hi) + _dot(a_ref[...], lo)


def _hyena_filter_spec(h, fwd_bf16, seq_len):
    h2d = h.reshape(seq_len, 4 * HY_DIM)
    tm = min(seq_len, 1024)
    tn = 512
    nb = 2 * HY_DIM // tn
    kern = functools.partial(_filter_spec_kernel, seq_len=seq_len, tm=tm)
    return pl.pallas_call(
        kern,
        grid=(nb, 2 * seq_len // tm),
        in_specs=[pl.BlockSpec((tm, seq_len), lambda j, i: (i, 0)),
                  pl.BlockSpec((seq_len, tn), lambda j, i: (0, j)),
                  pl.BlockSpec((seq_len, tn), lambda j, i: (0, nb + j))],
        out_specs=pl.BlockSpec((tm, tn), lambda j, i: (i, j)),
        out_shape=jax.ShapeDtypeStruct((2 * seq_len, 2 * HY_DIM), F32),
        compiler_params=_params(("parallel", "arbitrary"), 40),
        name=f"filter_spec_{seq_len}",
    )(fwd_bf16, h2d, h2d)


def _hyena(p, short_w, short_b, bias, h_time, n_seq, seq_len, row_block0):
    fwd, inv = _dft_matrices(seq_len)
    kspec = _hyena_filter_spec(h_time, fwd, seq_len)
    u = _shortconv(p, short_w, short_b, n_seq, seq_len, row_block0)
    yspec = _seqmm(fwd, u, n_seq, 2 * HY_DIM, HY_DIM, f"hyena_fwd_{seq_len}_0")
    y1 = _hyena_inv(inv, yspec, kspec, 0, u, 2 * HY_DIM, u, 0, bias, n_seq, seq_len)
    yspec = _seqmm(fwd, y1, n_seq, 0, HY_DIM, f"hyena_fwd_{seq_len}_1")
    return _hyena_inv(inv, yspec, kspec, 1, y1, 0, u, HY_DIM, bias, n_seq, seq_len)


def _nt_dot(a, b):
    return lax.dot_general(a, b, (((1,), (1,)), ((), ())), preferred_element_type=F32)


def _ctx_attn_kernel(q_ref, k_ref, v_ref, o_ref, ko_ref, vo_ref):
    scale = HEAD_DIM ** -0.5
    ko_ref[...] = k_ref[...]
    vo_ref[...] = v_ref[...]
    for h in range(N_HEADS):
        sl = slice(h * HEAD_DIM, (h + 1) * HEAD_DIM)
        s = _nt_dot(q_ref[:, sl].astype(BF16), k_ref[:, sl].astype(BF16)) * scale
        e = jnp.exp(s - jnp.max(s, axis=-1, keepdims=True))
        pr = e / jnp.sum(e, axis=-1, keepdims=True)
        o_ref[:, sl] = _dot(pr.astype(BF16), v_ref[:, sl].astype(BF16))


def _ctx_attn(p):
    qb = 3 * HY_DIM // NA_DIM
    spec = lambda c: pl.BlockSpec((L_PROMPT, NA_DIM), lambda s: (s, c))
    out = jax.ShapeDtypeStruct((T_PROMPT, NA_DIM), F32)
    return pl.pallas_call(
        _ctx_attn_kernel,
        grid=(N_PROMPT_SEQ,),
        in_specs=[spec(qb), spec(qb + 1), spec(qb + 2)],
        out_specs=[spec(0), spec(0), spec(0)],
        out_shape=[out, out, out],
        compiler_params=_params(("parallel",), 40),
        name="ctx_attn",
    )(p, p, p)


NA_Q_ROWS = 8
NA_K_ROWS = NA_Q_ROWS + WIN_R
NA_ROW_BLOCKS = GRID_ROWS // NA_Q_ROWS
NA_HEADS_PER_STEP = LANES // HEAD_DIM


def _na_key_row0(rb, xp=jnp):
    return xp.clip(rb * NA_Q_ROWS - WIN_R // 2, 0, GRID_ROWS - NA_K_ROWS)


def _na_kernel(q_ref, k_ref, v_ref, kc_ref, vc_ref, b_ref, o_ref):
    scale = HEAD_DIM ** -0.5
    k0 = pl.multiple_of(_na_key_row0(pl.program_id(1)) * GRID_W, GRID_W)
    n_loc = NA_K_ROWS * GRID_W
    for h in range(NA_HEADS_PER_STEP):
        sl = slice(h * HEAD_DIM, (h + 1) * HEAD_DIM)
        q = q_ref[:, sl].astype(BF16)
        s_loc = _nt_dot(q, k_ref[pl.ds(k0, n_loc), sl].astype(BF16)) * scale + b_ref[0, 0, h]
        s_ctx = _nt_dot(q, kc_ref[0, :, sl].astype(BF16)) * scale
        m = jnp.maximum(jnp.max(s_loc, axis=-1, keepdims=True), jnp.max(s_ctx, axis=-1, keepdims=True))
        e_loc = jnp.exp(s_loc - m)
        e_ctx = jnp.exp(s_ctx - m)
        den = jnp.sum(e_loc, axis=-1, keepdims=True) + jnp.sum(e_ctx, axis=-1, keepdims=True)
        o_ref[:, sl] = (_dot((e_loc / den).astype(BF16), v_ref[pl.ds(k0, n_loc), sl].astype(BF16))
                        + _dot((e_ctx / den).astype(BF16), vc_ref[0, :, sl].astype(BF16)))


NA_BIAS_KINDS = 3


def _na_bias_kind(rb):
    return jnp.where(rb == 0, 0, jnp.where(rb == NA_ROW_BLOCKS - 1, 2, 1))


def _na_bias_table(rpb):
    q = np.arange(GRID_W)[:, None]
    kc = np.arange(GRID_W)[None, :]
    cs = np.clip(q - WIN_C // 2, 0, GRID_W - WIN_C)
    col_valid = (kc >= cs) & (kc < cs + WIN_C)
    col_off = np.clip(kc - q, -(WIN_C - 1), WIN_C - 1) + WIN_C - 1
    b = rpb.astype(F32)[:, :, col_off]
    b = jnp.where(jnp.asarray(col_valid)[None, None], b, NEG_BIG)
    neg = jnp.full_like(b, NEG_BIG)
    b_next = jnp.concatenate([b[:, 1:], neg[:, :1]], axis=1)
    pairs = jnp.stack([jnp.concatenate([b, b_next], axis=-1), jnp.concatenate([b, neg], axis=-1),
                       jnp.concatenate([neg, b], axis=-1)], axis=1)

    def geometry(rb):
        q_row = rb * NA_Q_ROWS + np.arange(NA_Q_ROWS)[:, None]
        k_row = _na_key_row0(rb, np) + np.arange(NA_K_ROWS)[None, :]
        row_start = np.clip(q_row - WIN_R // 2, 0, GRID_ROWS - WIN_R)
        row_valid = (k_row >= row_start) & (k_row < row_start + WIN_R)
        return row_valid, k_row - q_row + WIN_R - 1

    for rb in range(2, NA_ROW_BLOCKS - 1):
        assert all(np.array_equal(a, c) for a, c in zip(geometry(rb), geometry(1)))
    geoms = [geometry(rb) for rb in (0, 1, NA_ROW_BLOCKS - 1)]
    kern = functools.partial(_na_bias_kernel, geoms=geoms)
    return pl.pallas_call(
        kern,
        grid=(N_HEADS, NA_BIAS_KINDS),
        in_specs=[pl.BlockSpec((1,) + pairs.shape[1:], lambda h, kind: (h, 0, 0, 0, 0))],
        out_specs=pl.BlockSpec((1, 1, 1, NA_Q_ROWS * GRID_W, NA_K_ROWS * GRID_W),
                               lambda h, kind: (kind, h // NA_HEADS_PER_STEP, h % NA_HEADS_PER_STEP, 0, 0)),
        out_shape=jax.ShapeDtypeStruct((NA_BIAS_KINDS, N_HEADS // NA_HEADS_PER_STEP, NA_HEADS_PER_STEP,
                                        NA_Q_ROWS * GRID_W, NA_K_ROWS * GRID_W), F32),
        compiler_params=_params(("parallel", "arbitrary"), 32),
        name="na_bias",
    )(pairs)


def _na_bias_kernel(p_ref, o_ref, *, geoms):
    for kind, (row_valid, d) in enumerate(geoms):
        @pl.when(pl.program_id(1) == kind)
        def _(row_valid=row_valid, d=d):
            for iq in range(NA_Q_ROWS):
                rows = slice(iq * GRID_W, (iq + 1) * GRID_W)
                for ik in range(0, NA_K_ROWS, 2):
                    cols = slice(ik * GRID_W, (ik + 2) * GRID_W)
                    left, right = row_valid[iq, ik], row_valid[iq, ik + 1]
                    if left and right:
                        o_ref[0, 0, 0, rows, cols] = p_ref[0, 0, int(d[iq, ik])]
                    elif left:
                        o_ref[0, 0, 0, rows, cols] = p_ref[0, 1, int(d[iq, ik])]
                    elif right:
                        o_ref[0, 0, 0, rows, cols] = p_ref[0, 2, int(d[iq, ik + 1])]
                    else:
                        o_ref[0, 0, 0, rows, cols] = jnp.full((GRID_W, 2 * GRID_W), NEG_BIG, F32)


def _na_attn(p, cache_k, cache_v, rpb):
    tq = NA_Q_ROWS * GRID_W
    q_col0 = 3 * HY_DIM // LANES
    k_col0 = q_col0 + NA_DIM // LANES
    v_col0 = k_col0 + NA_DIM // LANES
    q_blk0 = T_PROMPT // tq
    seq0 = T_PROMPT // L_LATENT
    bias = _na_bias_table(rpb)
    ctx_spec = pl.BlockSpec((1, cache_k.shape[1], LANES), lambda hp, rb, b: (b, 0, hp))
    return pl.pallas_call(
        _na_kernel,
        grid=(N_HEADS // NA_HEADS_PER_STEP, NA_ROW_BLOCKS, N_LATENT_SEQ),
        in_specs=[pl.BlockSpec((tq, LANES), lambda hp, rb, b: (q_blk0 + b * NA_ROW_BLOCKS + rb, q_col0 + hp)),
                  pl.BlockSpec((L_LATENT, LANES), lambda hp, rb, b: (seq0 + b, k_col0 + hp)),
                  pl.BlockSpec((L_LATENT, LANES), lambda hp, rb, b: (seq0 + b, v_col0 + hp)),
                  ctx_spec, ctx_spec,
                  pl.BlockSpec((1, 1, NA_HEADS_PER_STEP, tq, NA_K_ROWS * GRID_W),
                               lambda hp, rb, b: (_na_bias_kind(rb), hp, 0, 0, 0))],
        out_specs=pl.BlockSpec((tq, LANES), lambda hp, rb, b: (b * NA_ROW_BLOCKS + rb, hp)),
        out_shape=jax.ShapeDtypeStruct((T_LATENT, NA_DIM), F32),
        compiler_params=_params(("parallel", "parallel", "arbitrary"), 48),
        name="na_attn",
    )(p, p, p, cache_k, cache_v, bias)


def _pool_kernel(prev_ref, cur_ref, next_ref, gw_ref, sc_ref, o_ref, ext_ref, *, tm):
    i = pl.program_id(0)
    row0 = i * tm
    seq_len = jnp.where(row0 < T_PROMPT, L_PROMPT, L_LATENT)
    pos0 = jnp.where(row0 < T_PROMPT, row0 % L_PROMPT, (row0 - T_PROMPT) % L_LATENT)
    first = pos0 == 0
    last = pos0 + tm == seq_len
    h = POOL_HALO
    ext_ref[0:h] = jnp.where(first, 0.0, prev_ref[...])
    ext_ref[h:h + tm] = cur_ref[...]
    ext_ref[h + tm:] = jnp.where(last, 0.0, next_ref[...])
    pos = pos0 + lax.broadcasted_iota(jnp.int32, (tm, 1), 0)
    for g, w in enumerate(POOL_WINDOWS):
        cols = slice(g * POOL_GROUP, (g + 1) * POOL_GROUP)
        acc = ext_ref[h - w // 2:h - w // 2 + tm, cols]
        for j in range(1, w):
            acc = acc + ext_ref[h - w // 2 + j:h - w // 2 + j + tm, cols]
        cnt = jnp.minimum(pos + w // 2, seq_len) - jnp.maximum(pos - w // 2, 0)
        d = acc / cnt.astype(F32) - cur_ref[:, cols]
        o_ref[:, cols] = _dot(d.astype(BF16), gw_ref[g]) * sc_ref[:, cols]


def _pool(u, group_w_bf16, scale):
    tm = 256
    hb = tm // POOL_HALO
    n_halo = T_ALL // POOL_HALO
    kern = functools.partial(_pool_kernel, tm=tm)
    return pl.pallas_call(
        kern,
        grid=(T_ALL // tm,),
        in_specs=[pl.BlockSpec((POOL_HALO, D), lambda i: (jnp.maximum(i * hb - 1, 0), 0)),
                  pl.BlockSpec((tm, D), lambda i: (i, 0)),
                  pl.BlockSpec((POOL_HALO, D), lambda i: (jnp.minimum((i + 1) * hb, n_halo - 1), 0)),
                  pl.BlockSpec((len(POOL_WINDOWS), POOL_GROUP, POOL_GROUP), lambda i: (0, 0, 0)),
                  pl.BlockSpec((1, D), lambda i: (0, 0))],
        out_specs=pl.BlockSpec((tm, D), lambda i: (i, 0)),
        out_shape=jax.ShapeDtypeStruct((T_ALL, D), F32),
        scratch_shapes=[pltpu.VMEM((tm + 2 * POOL_HALO, D), F32)],
        compiler_params=_params(("parallel",), 40),
        name="pool",
    )(u, u, u, group_w_bf16, scale.reshape(1, D))


def _post_mixer_kernel(*refs, n1, n2, n_x, tm):
    a1_refs, a2_refs, x_refs = refs[:n1], refs[n1:n1 + n2], refs[n1 + n2:n1 + n2 + n_x]
    (w_ref, gpost_ref, gate_ref, gpre_ref, sh_ref, sc_ref, rwh_ref, rwl_ref, rb_ref,
     xo_ref, h_ref, lg_ref) = refs[n1 + n2 + n_x:]
    a1 = _token_rows(a1_refs, tm).astype(BF16)
    a2 = _token_rows(a2_refs, tm).astype(BF16)
    k1 = a1.shape[1]
    m = _dot(a1, w_ref[0:k1]) + _dot(a2, w_ref[k1:])
    xn = _token_rows(x_refs, tm) + gate_ref[0] * (_rms(m) * gpost_ref[...])
    xo_ref[...] = xn
    h = _rms(xn) * gpre_ref[...] * (1.0 + sc_ref[0]) + sh_ref[0]
    hh, hl = _split_bf16(h)
    h_ref[...] = _pack_bf16_pairs(hh)
    lg_ref[...] = _dot(hh, rwh_ref[...]) + _dot(hl, rwh_ref[...]) + _dot(hh, rwl_ref[...]) + rb_ref[...]


def _post_mixer(a1_parts, a1_col, a2_parts, a2_col, w_bf16, x_parts, g_post, gate, g_pre, shift, scale,
                rw_hi, rw_lo, rb, name):
    tm = 256
    kh = w_bf16.shape[0] // 2
    row = lambda i: (i, 0)
    const = lambda i: (0, 0)
    mod_spec = pl.BlockSpec((1, 1, D), lambda i: (_mod_row(i * tm), 0, 0))
    vec = pl.BlockSpec((1, D), const)
    kern = functools.partial(_post_mixer_kernel, n1=len(a1_parts), n2=len(a2_parts), n_x=len(x_parts), tm=tm)
    return pl.pallas_call(
        kern,
        grid=(T_ALL // tm,),
        in_specs=_token_specs(a1_parts, tm, kh, a1_col) + _token_specs(a2_parts, tm, kh, a2_col)
                 + _token_specs(x_parts, tm, D, 0) + [
                  pl.BlockSpec((2 * kh, D), const),
                  vec, mod_spec, vec, mod_spec, mod_spec,
                  pl.BlockSpec((D, LANES), const), pl.BlockSpec((D, LANES), const),
                  pl.BlockSpec((1, LANES), const)],
        out_specs=[pl.BlockSpec((tm, D), row), pl.BlockSpec((tm, D // 2), row), pl.BlockSpec((tm, LANES), row)],
        out_shape=[jax.ShapeDtypeStruct((T_ALL, D), F32), jax.ShapeDtypeStruct((T_ALL, D // 2), jnp.uint32),
                   jax.ShapeDtypeStruct((T_ALL, LANES), F32)],
        compiler_params=_params(("parallel",), 48),
        name=name,
    )(*a1_parts, *a2_parts, *x_parts, w_bf16, g_post, gate, g_pre, shift, scale, rw_hi, rw_lo, rb)


def _route_kernel(lg_ref, tri_ref, eidx_ref, gate_ref, rank_ref, cnt_ref, carry_ref):
    @pl.when(pl.program_id(0) == 0)
    def _():
        carry_ref[...] = jnp.zeros_like(carry_ref)

    shape = lg_ref.shape
    lane = lax.broadcasted_iota(jnp.int32, shape, 1).astype(F32)
    lg = jnp.where(lane < N_EXPERTS, lg_ref[...], -jnp.inf)
    multi = jnp.zeros(shape, F32)
    vals, sels = [], []
    eidx = jnp.zeros(shape, F32)
    for k in range(TOP_K):
        m = jnp.max(lg, axis=-1, keepdims=True)
        idx = jnp.min(jnp.where(lg == m, lane, float(LANES)), axis=-1, keepdims=True)
        sel = lane == idx
        multi = jnp.where(sel, 1.0, multi)
        lg = jnp.where(sel, -jnp.inf, lg)
        eidx = jnp.where(lane == k, idx, eidx)
        vals.append(m)
        sels.append(sel)
    exps = [jnp.exp(v - vals[0]) for v in vals]
    den = exps[0] + exps[1] + exps[2] + exps[3]
    rank_all = _dot(tri_ref[...], multi.astype(BF16)) + carry_ref[0:1]
    gate = jnp.zeros(shape, F32)
    rank = jnp.zeros(shape, F32)
    for k in range(TOP_K):
        gate = jnp.where(lane == k, exps[k] / den, gate)
        rank = jnp.where(lane == k, jnp.sum(jnp.where(sels[k], rank_all, 0.0), axis=-1, keepdims=True), rank)
    eidx_ref[...] = eidx.astype(jnp.int32)
    gate_ref[...] = gate
    rank_ref[...] = rank.astype(jnp.int32)
    carry_ref[0:1] = carry_ref[0:1] + jnp.sum(multi, axis=0, keepdims=True)
    cnt_ref[...] = carry_ref[...]


def _route(logits):
    tm = 256
    tri = (np.arange(tm)[:, None] > np.arange(tm)[None, :]).astype(np.float32)
    row = lambda i: (i, 0)
    spec = pl.BlockSpec((tm, LANES), row)
    return pl.pallas_call(
        _route_kernel,
        grid=(T_ALL // tm,),
        in_specs=[spec, pl.BlockSpec((tm, tm), lambda i: (0, 0))],
        out_specs=[spec, spec, spec, pl.BlockSpec((8, LANES), lambda i: (0, 0))],
        out_shape=[jax.ShapeDtypeStruct((T_ALL, LANES), jnp.int32), jax.ShapeDtypeStruct((T_ALL, LANES), F32),
                   jax.ShapeDtypeStruct((T_ALL, LANES), jnp.int32), jax.ShapeDtypeStruct((8, LANES), F32)],
        scratch_shapes=[pltpu.VMEM((8, LANES), F32)],
        compiler_params=_params(("arbitrary",), 32),
        name="route",
    )(logits, jnp.asarray(tri, BF16))


def _dest_kernel(eidx_ref, rank_ref, start_ref, o_ref):
    shape = eidx_ref.shape
    lane = lax.broadcasted_iota(jnp.int32, shape, 1).astype(F32)
    eidx = eidx_ref[...].astype(F32)
    starts = start_ref[...].astype(F32)
    dest = jnp.zeros(shape, F32)
    for k in range(TOP_K):
        e_k = jnp.sum(jnp.where(lane == k, eidx, 0.0), axis=-1, keepdims=True)
        start = jnp.sum(jnp.where(lane == e_k, starts, 0.0), axis=-1, keepdims=True)
        dest = jnp.where(lane == k, start, dest)
    o_ref[...] = jnp.where(lane < TOP_K, dest.astype(jnp.int32) + rank_ref[...], 0)


def _dest(eidx, rank, group_start):
    tm = 1024
    spec = pl.BlockSpec((tm, LANES), lambda i: (i, 0))
    return pl.pallas_call(
        _dest_kernel,
        grid=(T_ALL // tm,),
        in_specs=[spec, spec, pl.BlockSpec((1, LANES), lambda i: (0, 0))],
        out_specs=spec,
        out_shape=jax.ShapeDtypeStruct((T_ALL, LANES), jnp.int32),
        compiler_params=_params(("parallel",), 32),
        name="dest",
    )(eidx, rank, group_start)


def _dispatch_kernel(dest_ref, lo_ref, hi_ref, nu_ref, h_ref, o_ref, zero_ref, sem, zsem, *, tm):
    i = pl.program_id(0)
    base = i * tm * TOP_K

    @pl.when(i == 0)
    def _():
        zero_ref[...] = jnp.zeros_like(zero_ref)

    def issue(t, carry):
        for k in range(TOP_K):
            d = dest_ref[base + t * TOP_K + k]
            pltpu.make_async_copy(h_ref.at[pl.ds(t, 1)], o_ref.at[pl.ds(d, 1)], sem).start()
        return carry

    def zero_copies(do):
        @pl.when(i < N_EXPERTS)
        def _():
            lo = lo_ref[i]
            n_pad = hi_ref[i] - lo
            head = jnp.minimum((8 - lo % 8) % 8, n_pad)

            def zero_row(r, carry):
                do(pltpu.make_async_copy(zero_ref.at[pl.ds(0, 1)], o_ref.at[pl.ds(lo + r, 1)], zsem))
                return carry

            lax.fori_loop(0, head, zero_row, 0)
            pos = lo + head
            rest = n_pad - head
            size = MOE_TM // 2
            while size >= 8:
                @pl.when((rest & size) != 0)
                def _(pos=pos, size=size):
                    do(pltpu.make_async_copy(zero_ref.at[pl.ds(0, size)],
                                             o_ref.at[pl.ds(pl.multiple_of(pos, 8), size)], zsem))

                pos = pos + (rest & size)
                size //= 2

        blk = nu_ref[0] + i - N_EXPERTS

        @pl.when((i >= N_EXPERTS) & (blk < MOE_BLOCKS))
        def _():
            r0 = pl.multiple_of(blk * MOE_TM, MOE_TM)
            do(pltpu.make_async_copy(zero_ref, o_ref.at[pl.ds(r0, MOE_TM)], zsem))

    zero_copies(lambda cp: cp.start())
    lax.fori_loop(0, tm, issue, 0, unroll=2)
    zero_copies(lambda cp: cp.wait())
    for k in range(TOP_K):
        pltpu.make_async_copy(h_ref, o_ref.at[pl.ds(0, tm)], sem).wait()


def _dispatch(dest_flat, pad_lo, pad_hi, n_used, h):
    tm = 128
    assert T_ALL // tm >= 2 * N_EXPERTS
    kern = functools.partial(_dispatch_kernel, tm=tm)
    return pl.pallas_call(
        kern,
        grid_spec=pltpu.PrefetchScalarGridSpec(
            num_scalar_prefetch=4,
            grid=(T_ALL // tm,),
            in_specs=[pl.BlockSpec((tm, D // 2), lambda i, *_: (i, 0))],
            out_specs=pl.BlockSpec(memory_space=pl.ANY),
            scratch_shapes=[pltpu.VMEM((MOE_TM, D // 2), jnp.uint32), pltpu.SemaphoreType.DMA,
                            pltpu.SemaphoreType.DMA]),
        out_shape=jax.ShapeDtypeStruct((MOE_ROWS, D // 2), jnp.uint32),
        compiler_params=_params(("arbitrary",), 32),
        name="dispatch",
    )(dest_flat, pad_lo, pad_hi, n_used, h)


def _new_weights(be_ref, i):
    return (i == 0) | (be_ref[i] != be_ref[jnp.maximum(i - 1, 0)])


def _for_valid_rows(n_valid, compute):
    many = n_valid > MOE_TM // 2
    pl.when(many)(lambda: compute(MOE_TM))
    pl.when(jnp.logical_not(many))(lambda: compute(MOE_TM // 2))


def _expert_up_kernel(be_ref, nu_ref, bv_ref, x_ref, w_ref, b_ref, o_ref, wbf_ref):
    i = pl.program_id(1)
    tn = o_ref.shape[1]

    def compute(rows):
        x = jnp.concatenate(_unpack_bf16_pairs(x_ref[0:rows]), axis=1)
        even = lax.broadcasted_iota(jnp.int32, (rows, LANES), 1) % 2 == 0
        for c in range(tn // LANES):
            cols = slice(2 * LANES * c, 2 * LANES * (c + 1))
            hb = _dot(x, wbf_ref[:, cols]) + b_ref[0, 0, :, cols]
            first, second = hb[:, :LANES], hb[:, LANES:]
            g = jnp.where(even, first, pltpu.roll(second, 1, 1))
            lin = jnp.where(even, pltpu.roll(first, LANES - 1, 1), second)
            g = jnp.minimum(g, SWIGLU_LIMIT)
            lin = jnp.clip(lin, -SWIGLU_LIMIT, SWIGLU_LIMIT)
            act = g * jax.nn.sigmoid(SWIGLU_ALPHA * g) * (lin + 1.0)
            o_ref[0:rows, LANES * c:LANES * (c + 1)] = act.astype(BF16)
        if rows < MOE_TM:
            o_ref[rows:] = jnp.zeros((MOE_TM - rows, tn), BF16)

    @pl.when(i < nu_ref[0])
    def _():
        @pl.when(_new_weights(be_ref, i))
        def _():
            wbf_ref[...] = w_ref[0, 0].astype(BF16)

        _for_valid_rows(bv_ref[i], compute)

    @pl.when(i >= nu_ref[0])
    def _():
        o_ref[...] = jnp.zeros_like(o_ref)


def _expert_up(blk_e, n_used, blk_valid, xs, w_gu, b_gu, layer):
    tn = FF_TILE
    blk = lambda j, i, be, nu, bv: (jnp.minimum(i, nu[0] - 1), 0)
    return pl.pallas_call(
        _expert_up_kernel,
        grid_spec=pltpu.PrefetchScalarGridSpec(
            num_scalar_prefetch=3,
            grid=(D_FF // tn, MOE_BLOCKS),
            in_specs=[pl.BlockSpec((MOE_TM, D // 2), blk),
                      pl.BlockSpec((1, 1, D, 2 * tn), lambda j, i, be, nu, bv: (layer, be[i], 0, j)),
                      pl.BlockSpec((1, 1, 1, 2 * tn), lambda j, i, be, nu, bv: (layer, be[i], 0, j))],
            out_specs=pl.BlockSpec((MOE_TM, tn), lambda j, i, be, nu, bv: (i, j)),
            scratch_shapes=[pltpu.VMEM((D, 2 * tn), BF16)]),
        out_shape=jax.ShapeDtypeStruct((MOE_ROWS, D_FF), BF16),
        compiler_params=_params(("arbitrary", "arbitrary"), 56),
        name="expert_up",
    )(blk_e, n_used, blk_valid, xs, w_gu, b_gu.reshape(N_LAYERS, N_EXPERTS, 1, 2 * D_FF))


DOWN_TILE = D


def _expert_down_kernel(be_ref, nu_ref, bv_ref, a_ref, w_ref, b_ref, o_ref, wbf_ref, wtmp_ref):
    i = pl.program_id(1)

    def compute(rows):
        y = _dot(a_ref[0:rows], wbf_ref[...]) + b_ref[0, 0]
        o_ref[0:rows] = _pack_bf16_pairs(y.astype(BF16))
        if rows < MOE_TM:
            o_ref[rows:] = jnp.zeros((MOE_TM - rows, o_ref.shape[1]), jnp.uint32)

    @pl.when(i < nu_ref[0])
    def _():
        @pl.when(_new_weights(be_ref, i))
        def _():
            half = LANES // 2
            for g in range(D_FF // LANES):
                r0 = LANES * g
                for c in range(wbf_ref.shape[1] // LANES):
                    cols = slice(LANES * c, LANES * (c + 1))
                    wtmp_ref[c, pl.ds(0, half, stride=2), :] = w_ref[0, 0, r0:r0 + half, cols]
                    wtmp_ref[c, pl.ds(1, half, stride=2), :] = w_ref[0, 0, r0 + half:r0 + LANES, cols]
                    wbf_ref[r0:r0 + LANES, cols] = wtmp_ref[c].astype(BF16)

        _for_valid_rows(bv_ref[i], compute)

    @pl.when(i >= nu_ref[0])
    def _():
        o_ref[...] = jnp.zeros_like(o_ref)


def _expert_down(blk_e, n_used, blk_valid, a, w_down, b_down, layer):
    tn = DOWN_TILE
    blk = lambda j, i, be, nu, bv: (jnp.minimum(i, nu[0] - 1), 0)
    return pl.pallas_call(
        _expert_down_kernel,
        grid_spec=pltpu.PrefetchScalarGridSpec(
            num_scalar_prefetch=3,
            grid=(D // tn, MOE_BLOCKS),
            in_specs=[pl.BlockSpec((MOE_TM, D_FF), blk),
                      pl.BlockSpec((1, 1, D_FF, tn), lambda j, i, be, nu, bv: (layer, be[i], 0, j)),
                      pl.BlockSpec((1, 1, 1, tn), lambda j, i, be, nu, bv: (layer, be[i], 0, j))],
            out_specs=pl.BlockSpec((MOE_TM, tn // 2), lambda j, i, be, nu, bv: (i, j)),
            scratch_shapes=[pltpu.VMEM((D_FF, tn), BF16), pltpu.VMEM((tn // LANES, LANES, LANES), F32)]),
        out_shape=jax.ShapeDtypeStruct((MOE_ROWS, D // 2), jnp.uint32),
        compiler_params=_params(("arbitrary", "arbitrary"), 60),
        name="expert_down",
    )(blk_e, n_used, blk_valid, a, w_down, b_down.reshape(N_LAYERS, N_EXPERTS, 1, D))


def _combine_kernel(dest_ref, y_ref, gate_ref, x_ref, gpost_ref, g2_ref, *rest, tm):
    (*o_refs, buf_ref, sem) = rest
    i = pl.program_id(0)

    def gather(step, slot):
        base = step * tm * TOP_K

        def issue(t, carry):
            for k in range(TOP_K):
                d = dest_ref[base + t * TOP_K + k]
                pltpu.make_async_copy(y_ref.at[pl.ds(d, 1)], buf_ref.at[slot, k, pl.ds(t, 1)],
                                      sem.at[slot]).start()
            return carry

        lax.fori_loop(0, tm, issue, 0, unroll=2)

    @pl.when(i == 0)
    def _():
        gather(0, 0)

    @pl.when(i + 1 < pl.num_programs(0))
    def _():
        gather(i + 1, (i + 1) % 2)

    slot = i % 2
    for k in range(TOP_K):
        pltpu.make_async_copy(y_ref.at[pl.ds(0, tm)], buf_ref.at[slot, k], sem.at[slot]).wait()

    def expert_rows(k):
        hi, lo = _unpack_bf16_pairs(buf_ref[slot, k])
        tw = DOWN_TILE // 2
        parts = []
        for j in range(D // DOWN_TILE):
            parts += [hi[:, j * tw:(j + 1) * tw], lo[:, j * tw:(j + 1) * tw]]
        return jnp.concatenate(parts, axis=1).astype(F32)

    f = gate_ref[:, 0:1] * expert_rows(0)
    for k in range(1, TOP_K):
        f = f + gate_ref[:, k:k + 1] * expert_rows(k)
    out = x_ref[...] + g2_ref[0] * (_rms(f) * gpost_ref[...])
    if len(o_refs) == 1:
        o_refs[0][...] = out
    else:
        in_prompt = i * tm < T_PROMPT
        @pl.when(in_prompt)
        def _():
            o_refs[0][...] = out

        @pl.when(jnp.logical_not(in_prompt))
        def _():
            o_refs[1][...] = out


def _combine(dest_flat, ys, gate, x, g_post, gate2, name, split_out=False):
    tm = 128
    kern = functools.partial(_combine_kernel, tm=tm)
    n_prompt_tiles = T_PROMPT // tm
    if split_out:
        out_specs = [pl.BlockSpec((tm, D), lambda i, d: (jnp.minimum(i, n_prompt_tiles - 1), 0)),
                     pl.BlockSpec((tm, D), lambda i, d: (jnp.maximum(i - n_prompt_tiles, 0), 0))]
        out_shape = [jax.ShapeDtypeStruct((T_PROMPT, D), F32), jax.ShapeDtypeStruct((T_LATENT, D), F32)]
    else:
        out_specs = pl.BlockSpec((tm, D), lambda i, d: (i, 0))
        out_shape = jax.ShapeDtypeStruct((T_ALL, D), F32)
    return pl.pallas_call(
        kern,
        grid_spec=pltpu.PrefetchScalarGridSpec(
            num_scalar_prefetch=1,
            grid=(T_ALL // tm,),
            in_specs=[pl.BlockSpec(memory_space=pl.ANY),
                      pl.BlockSpec((tm, LANES), lambda i, d: (i, 0)),
                      pl.BlockSpec((tm, D), lambda i, d: (i, 0)),
                      pl.BlockSpec((1, D), lambda i, d: (0, 0)),
                      pl.BlockSpec((1, 1, D), lambda i, d: (_mod_row(i * tm), 0, 0))],
            out_specs=out_specs,
            scratch_shapes=[pltpu.VMEM((2, TOP_K, tm, D // 2), jnp.uint32), pltpu.SemaphoreType.DMA((2,))]),
        out_shape=out_shape,
        compiler_params=_params(("arbitrary",), 32),
        name=name,
    )(dest_flat, ys, gate, x, g_post, gate2)


def _moe(h, logits, x, g_post, gate2, w_gu, b_gu, w_down, b_down, layer):
    eidx, gate, rank, counts = _route(logits)
    cnt = counts[0, :N_EXPERTS].astype(jnp.int32)
    padded = (cnt + MOE_TM - 1) // MOE_TM * MOE_TM
    group_end = jnp.cumsum(padded)
    group_start = jnp.zeros((1, LANES), jnp.int32).at[0, :N_EXPERTS].set(group_end - padded)
    n_used = group_end[-1] // MOE_TM
    blk_row0 = jnp.arange(MOE_BLOCKS, dtype=jnp.int32) * MOE_TM
    blk_row0 = jnp.minimum(blk_row0, (n_used - 1) * MOE_TM)
    blk_e = jnp.sum(group_end[None, :] <= blk_row0[:, None], axis=1).astype(jnp.int32)
    n_used = n_used.astype(jnp.int32).reshape(1)
    filled_end = group_end - padded + cnt
    blk_valid = jnp.clip(filled_end[jnp.minimum(blk_e, N_EXPERTS - 1)] - blk_row0, 0, MOE_TM).astype(jnp.int32)
    dest = _dest(eidx, rank, group_start)[:, :TOP_K].reshape(N_ASSIGN)
    xs = _dispatch(dest, filled_end, group_end, n_used, h)
    a = _expert_up(blk_e, n_used, blk_valid, xs, w_gu, b_gu, layer)
    ys = _expert_down(blk_e, n_used, blk_valid, a, w_down, b_down, layer)
    return _combine(dest, ys, gate, x, g_post, gate2, f"combine_{layer}", split_out=layer == N_LAYERS - 1)


def kernel(x_prompt, x_sample, cache_k, cache_v, c, c_ctx, ada_w, ada_b, norm_mix_pre, norm_mix_post,
           norm_ffn_pre, norm_ffn_post, ab_in_w, ab_out_w, hy_short_w, hy_short_b, hy_w1, hy_b1, hy_w2,
           hy_b2, hy_w3, hy_freq, hy_bias, na_rpb, pool_in_w, pool_group_w, pool_scale, pool_out_w,
           router_w, router_b, exp_w_gu, exp_b_gu, exp_w_down, exp_b_down):
    x = [x_prompt.reshape(T_PROMPT, D), x_sample.reshape(T_LATENT, D)]
    cvec = jnp.concatenate([c_ctx[None], c, jnp.zeros((8 - 1 - N_LATENT_SEQ, D), F32)], axis=0)
    mods = _adaln(cvec, ada_w, ada_b).reshape(N_LAYERS, 8, 6, D)

    new_k = new_v = None
    for layer in range(N_LAYERS):
        sh1, sc1, g1, sh2, sc2, g2 = [mods[layer, :, k][:, None, :] for k in range(6)]
        row = lambda a: a[layer].reshape(1, D)
        if layer % 2 == 0:
            j = layer // 2
            p = _modmm(x, row(norm_mix_pre), sh1, sc1, ab_in_w[j].astype(BF16), "ab_in")
            filt = (hy_w1[j], hy_b1[j], hy_w2[j], hy_b2[j], hy_w3[j], hy_freq[j])
            hy = (hy_short_w[j], hy_short_b[j], hy_bias[j])
            yh_p = _hyena(p, *hy, _hyena_filter_time(L_PROMPT, *filt), N_PROMPT_SEQ, L_PROMPT, 0)
            yh_s = _hyena(p, *hy, _hyena_filter_time(L_LATENT, *filt), N_LATENT_SEQ, L_LATENT,
                          T_PROMPT // L_LATENT)
            ya_p, new_k, new_v = _ctx_attn(p)
            ck = cache_k[:, j].reshape(N_LATENT_SEQ, -1, NA_DIM)
            cv = cache_v[:, j].reshape(N_LATENT_SEQ, -1, NA_DIM)
            ya_s = _na_attn(p, ck, cv, na_rpb[j])
            a1 = [yh_p, yh_s]
            a2 = [ya_p, ya_s]
            a1_col = a2_col = 0
            w_out = ab_out_w[j].astype(BF16)
        else:
            j = layer // 2
            u = _modmm(x, row(norm_mix_pre), sh1, sc1, pool_in_w[j].astype(BF16), "pool_in")
            a1 = a2 = [_pool(u, pool_group_w[j].astype(BF16), pool_scale[j])]
            a1_col, a2_col = 0, 1
            w_out = pool_out_w[j].astype(BF16)
        rw = jnp.zeros((D, LANES), F32).at[:, :N_EXPERTS].set(router_w[layer])
        rw_hi, rw_lo = _split_bf16(rw)
        rb = jnp.zeros((1, LANES), F32).at[0, :N_EXPERTS].set(router_b[layer])
        xm, h, logits = _post_mixer(a1, a1_col, a2, a2_col, w_out, x, row(norm_mix_post), g1,
                                    row(norm_ffn_pre), sh2, sc2, rw_hi, rw_lo, rb, f"post_mixer_{layer}")
        x = _moe(h, logits, xm, row(norm_ffn_post), g2, exp_w_gu, exp_b_gu, exp_w_down, exp_b_down, layer)
        x = list(x) if isinstance(x, (list, tuple)) else [x]

    y_prompt = x[0].reshape(x_prompt.shape)
    y_sample = x[1].reshape(x_sample.shape)
    kv_shape = (N_PROMPT_SEQ, 1, L_PROMPT, N_HEADS, HEAD_DIM)
    return y_prompt, y_sample, new_k.reshape(kv_shape), new_v.reshape(kv_shape)
```

```python
import functools
import math

import jax
import jax.numpy as jnp
import numpy as np
from jax import lax
from jax.experimental import pallas as pl
from jax.experimental.pallas import tpu as pltpu

F32 = jnp.float32
BF16 = jnp.bfloat16

D = 2048
N_PROMPT_SEQ = 32
L_PROMPT = 256
N_LATENT_SEQ = 4
L_LATENT = 2048
T_PROMPT = N_PROMPT_SEQ * L_PROMPT
T_LATENT = N_LATENT_SEQ * L_LATENT
T_ALL = T_PROMPT + T_LATENT
N_LAYERS = 2
GRID_W = 64
GRID_ROWS = L_LATENT // GRID_W
HY_DIM = 1024
HY_EMB = 33
HY_DECAY_TARGET = 1e-2
HY_FAST_DECAY = 0.3
HY_SLOW_DECAY = 1.5
N_HEADS = 16
HEAD_DIM = 64
NA_DIM = N_HEADS * HEAD_DIM
WIN_R = 8
WIN_C = 16
AB_IN = 3 * HY_DIM + 3 * NA_DIM
POOL_WINDOWS = (2, 4, 8, 16)
POOL_GROUP = D // len(POOL_WINDOWS)
POOL_HALO = 8
N_EXPERTS = 32
TOP_K = 4
D_FF = D
SWIGLU_LIMIT = 7.0
SWIGLU_ALPHA = 1.702
RMS_EPS = 1e-6
NEG_BIG = -1e30

LANES = 128
SUBLANES = 8
MOE_TM = 512
FF_TILE = 1024
N_ASSIGN = T_ALL * TOP_K
MOE_BLOCKS = -(-(N_ASSIGN + N_EXPERTS * (MOE_TM - 1)) // MOE_TM)
MOE_ROWS = MOE_BLOCKS * MOE_TM
MIB = 1 << 20


def _params(semantics, vmem_mib):
    return pltpu.CompilerParams(dimension_semantics=semantics, vmem_limit_bytes=vmem_mib * MIB)


def _mod_row(row0):
    return jnp.where(row0 < T_PROMPT, 0, 1 + (row0 - T_PROMPT) // L_LATENT)


def _rms(x):
    return x * lax.rsqrt(jnp.mean(x * x, axis=-1, keepdims=True) + RMS_EPS)


def _split_bf16(x):
    hi = x.astype(BF16)
    lo = (x - hi.astype(F32)).astype(BF16)
    return hi, lo


def _dot(a, b):
    return jnp.dot(a, b, preferred_element_type=F32)


def _pack_bf16_pairs(x_bf16):
    half = x_bf16.shape[1] // 2
    bits = lax.bitcast_convert_type(x_bf16.astype(F32), jnp.uint32)
    return (bits[:, :half] & jnp.uint32(0xFFFF0000)) | (bits[:, half:] >> 16)


def _unpack_bf16_pairs(packed):
    hi = lax.bitcast_convert_type(packed & jnp.uint32(0xFFFF0000), F32).astype(BF16)
    lo = lax.bitcast_convert_type(packed << 16, F32).astype(BF16)
    return hi, lo


def _adaln_kernel(cv_ref, w_ref, b_ref, o_ref):
    s = jax.nn.silu(cv_ref[...]).astype(BF16)
    o_ref[0] = _dot(s, w_ref[0].astype(BF16)) + b_ref[0]


def _adaln(cvec, ada_w, ada_b):
    n = ada_w.shape[-1]
    tn = 1024
    return pl.pallas_call(
        _adaln_kernel,
        grid=(N_LAYERS, n // tn),
        in_specs=[pl.BlockSpec((8, D), lambda l, j: (0, 0)),
                  pl.BlockSpec((1, D, tn), lambda l, j: (l, 0, j)),
                  pl.BlockSpec((1, 1, tn), lambda l, j: (l, 0, j))],
        out_specs=pl.BlockSpec((1, 8, tn), lambda l, j: (l, 0, j)),
        out_shape=jax.ShapeDtypeStruct((N_LAYERS, 8, n), F32),
        compiler_params=_params(("parallel", "arbitrary"), 40),
        name="adaln",
    )(cvec, ada_w, ada_b.reshape(N_LAYERS, 1, n))


def _token_specs(parts, tm, width, col):
    if len(parts) == 1:
        return [pl.BlockSpec((tm, width), lambda i, *_: (i, col))]
    n_prompt_tiles = T_PROMPT // tm
    return [pl.BlockSpec((tm, width), lambda i, *_: (jnp.minimum(i, n_prompt_tiles - 1), col)),
            pl.BlockSpec((tm, width), lambda i, *_: (jnp.maximum(i - n_prompt_tiles, 0), col))]


def _token_rows(part_refs, tm):
    if len(part_refs) == 1:
        return part_refs[0][...]
    return jnp.where(pl.program_id(0) * tm < T_PROMPT, part_refs[0][...], part_refs[1][...])


def _modmm_kernel(*refs, n_x, tm):
    x_refs = refs[:n_x]
    g_ref, sh_ref, sc_ref, w_ref, o_ref, h_ref = refs[n_x:]

    @pl.when(pl.program_id(1) == 0)
    def _():
        h = _rms(_token_rows(x_refs, tm)) * g_ref[...] * (1.0 + sc_ref[0]) + sh_ref[0]
        h_ref[...] = h.astype(BF16)

    o_ref[...] = _dot(h_ref[...], w_ref[...])


def _modmm(x_parts, g, shift, scale, w_bf16, name):
    n = w_bf16.shape[1]
    tm, tn = (1024, 1024) if len(x_parts) == 1 else (512, 2048)
    mod_spec = pl.BlockSpec((1, 1, D), lambda i, j: (_mod_row(i * tm), 0, 0))
    return pl.pallas_call(
        functools.partial(_modmm_kernel, n_x=len(x_parts), tm=tm),
        grid=(T_ALL // tm, n // tn),
        in_specs=_token_specs(x_parts, tm, D, 0) + [
                  pl.BlockSpec((1, D), lambda i, j: (0, 0)),
                  mod_spec, mod_spec,
                  pl.BlockSpec((D, tn), lambda i, j: (0, j))],
        out_specs=pl.BlockSpec((tm, tn), lambda i, j: (i, j)),
        out_shape=jax.ShapeDtypeStruct((T_ALL, n), F32),
        scratch_shapes=[pltpu.VMEM((tm, D), BF16)],
        compiler_params=_params(("parallel", "arbitrary"), 48),
        name=name,
    )(*x_parts, g, shift, scale, w_bf16)


def _shortconv_kernel(z_ref, w_ref, b_ref, o_ref):
    z = z_ref[...]
    n = z.shape[0]
    row = lax.broadcasted_iota(jnp.int32, z.shape, 0)
    prev = jnp.where(row == 0, 0.0, pltpu.roll(z, 1, 0))
    nxt = jnp.where(row == n - 1, 0.0, pltpu.roll(z, n - 1, 0))
    o_ref[...] = w_ref[0:1] * prev + w_ref[1:2] * z + w_ref[2:3] * nxt + b_ref[...]


def _shortconv(p, w, b, n_seq, seq_len, row_block0):
    c = 3 * HY_DIM
    tc = 512 if seq_len > 512 else c
    return pl.pallas_call(
        _shortconv_kernel,
        grid=(n_seq, c // tc),
        in_specs=[pl.BlockSpec((seq_len, tc), lambda s, j: (row_block0 + s, j)),
                  pl.BlockSpec((3, tc), lambda s, j: (0, j)),
                  pl.BlockSpec((1, tc), lambda s, j: (0, j))],
        out_specs=pl.BlockSpec((seq_len, tc), lambda s, j: (s, j)),
        out_shape=jax.ShapeDtypeStruct((n_seq * seq_len, c), F32),
        compiler_params=_params(("parallel", "parallel"), 40),
        name=f"shortconv_{seq_len}",
    )(p, w, b.reshape(1, c))


def _seqmm_kernel(a_ref, x_ref, o_ref):
    o_ref[...] = _dot(a_ref[...], x_ref[...].astype(BF16))


def _seqmm(a_bf16, x, n_seq, col0, n_cols, name):
    m, k = a_bf16.shape
    tm = min(m, 1024)
    tn = 512 if k > 512 else 1024
    col_block0 = col0 // tn
    return pl.pallas_call(
        _seqmm_kernel,
        grid=(n_seq, n_cols // tn, m // tm),
        in_specs=[pl.BlockSpec((tm, k), lambda s, j, i: (i, 0)),
                  pl.BlockSpec((k, tn), lambda s, j, i: (s, col_block0 + j))],
        out_specs=pl.BlockSpec((tm, tn), lambda s, j, i: (s * (m // tm) + i, j)),
        out_shape=jax.ShapeDtypeStruct((n_seq * m, n_cols), F32),
        compiler_params=_params(("parallel", "parallel", "arbitrary"), 40),
        name=name,
    )(a_bf16, x)


def _hyena_inv_kernel(g_ref, y_ref, k_ref, yin_ref, gate_ref, bias_ref, o_ref, z_ref, *, seq_len):
    @pl.when(pl.program_id(2) == 0)
    def _():
        yc, ys = y_ref[0:seq_len], y_ref[seq_len:]
        kc, ks = k_ref[0:seq_len], k_ref[seq_len:]
        z_ref[0:seq_len] = (yc * kc - ys * ks).astype(BF16)
        z_ref[seq_len:] = (yc * ks + ys * kc).astype(BF16)

    conv = _dot(g_ref[...], z_ref[...]) * (1.0 / seq_len)
    o_ref[...] = gate_ref[...] * (conv + yin_ref[...] * bias_ref[...])


def _hyena_inv(g_bf16, yspec, kspec, order, yin, yin_col0, u, gate_col0, bias, n_seq, seq_len):
    tn = 256 if seq_len > 512 else 1024
    tt = min(seq_len, 512)
    nb = HY_DIM // tn
    kern = functools.partial(_hyena_inv_kernel, seq_len=seq_len)
    return pl.pallas_call(
        kern,
        grid=(n_seq, nb, seq_len // tt),
        in_specs=[pl.BlockSpec((tt, 2 * seq_len), lambda s, j, t: (t, 0)),
                  pl.BlockSpec((2 * seq_len, tn), lambda s, j, t: (s, j)),
                  pl.BlockSpec((2 * seq_len, tn), lambda s, j, t: (0, order * nb + j)),
                  pl.BlockSpec((tt, tn), lambda s, j, t: (s * (seq_len // tt) + t, yin_col0 // tn + j)),
                  pl.BlockSpec((tt, tn), lambda s, j, t: (s * (seq_len // tt) + t, gate_col0 // tn + j)),
                  pl.BlockSpec((1, tn), lambda s, j, t: (0, order * nb + j))],
        out_specs=pl.BlockSpec((tt, tn), lambda s, j, t: (s * (seq_len // tt) + t, j)),
        out_shape=jax.ShapeDtypeStruct((n_seq * seq_len, HY_DIM), F32),
        scratch_shapes=[pltpu.VMEM((2 * seq_len, tn), BF16)],
        compiler_params=_params(("parallel", "parallel", "arbitrary"), 48),
        name=f"hyena_inv_{seq_len}_{order}",
    )(g_bf16, yspec, kspec, yin, u, bias.reshape(1, 2 * HY_DIM))


def _dft_matrices(seq_len):
    f = lax.broadcasted_iota(jnp.int32, (seq_len, seq_len), 0)
    s = lax.broadcasted_iota(jnp.int32, (seq_len, seq_len), 1)
    ang = (((2 * f + 1) * s) % (4 * seq_len)).astype(F32) * (math.pi / (2 * seq_len))
    c, sn = jnp.cos(ang), jnp.sin(ang)
    fwd = jnp.concatenate([c, sn], axis=0).astype(BF16)
    inv = jnp.concatenate([c.T, sn.T], axis=1).astype(BF16)
    return fwd, inv


def _hyena_filter_time(seq_len, w1, b1, w2, b2, w3, freq):
    hp = lax.Precision.HIGHEST
    t = jnp.linspace(0.0, 1.0, seq_len, dtype=F32)[:, None]
    bands = (HY_EMB - 1) // 2
    ang = 2.0 * math.pi * jnp.arange(seq_len, dtype=F32)[:, None] / seq_len
    fb = jnp.linspace(1e-4, bands - 1, bands, dtype=F32)[None, :]
    z = jnp.concatenate([t, jnp.cos(fb * ang), -jnp.sin(fb * ang)], axis=-1)
    h = jnp.sin(freq * (jnp.dot(z, w1, precision=hp) + b1))
    h = jnp.sin(freq * (jnp.dot(h, w2, precision=hp) + b2))
    h = jnp.dot(h, w3, precision=hp).reshape(seq_len, 2, 2, HY_DIM)
    deltas = jnp.linspace(math.log(HY_DECAY_TARGET) / HY_SLOW_DECAY,
                          math.log(HY_DECAY_TARGET) / HY_FAST_DECAY, HY_DIM, dtype=F32)
    h = h * jnp.exp(-t * jnp.abs(deltas))[:, None, None, :]
    return h * lax.rsqrt(jnp.sum(h * h, axis=(0, 1), keepdims=True))


def _filter_spec_kernel(a_ref, hf_ref, hb_ref, o_ref, *, seq_len, tm):
    sign = jnp.where(pl.program_id(1) * tm < seq_len, 1.0, -1.0)
    hb = hb_ref[...]
    hb = jnp.where(lax.broadcasted_iota(jnp.int32, hb.shape, 0) == 0, 0.0, hb)
    hi, lo = _split_bf16(hf_ref[...] + sign * hb)
    o_ref[...] = _dot(a_ref[...], hi) + _dot(a_ref[...], lo)


def _hyena_filter_spec(h, fwd_bf16, seq_len):
    h2d = h.reshape(seq_len, 4 * HY_DIM)
    tm = min(seq_len, 1024)
    tn = 512
    nb = 2 * HY_DIM // tn
    kern = functools.partial(_filter_spec_kernel, seq_len=seq_len, tm=tm)
    return pl.pallas_call(
        kern,
        grid=(nb, 2 * seq_len // tm),
        in_specs=[pl.BlockSpec((tm, seq_len), lambda j, i: (i, 0)),
                  pl.BlockSpec((seq_len, tn), lambda j, i: (0, j)),
                  pl.BlockSpec((seq_len, tn), lambda j, i: (0, nb + j))],
        out_specs=pl.BlockSpec((tm, tn), lambda j, i: (i, j)),
        out_shape=jax.ShapeDtypeStruct((2 * seq_len, 2 * HY_DIM), F32),
        compiler_params=_params(("parallel", "arbitrary"), 40),
        name=f"filter_spec_{seq_len}",
    )(fwd_bf16, h2d, h2d)


def _hyena(p, short_w, short_b, bias, h_time, n_seq, seq_len, row_block0):
    fwd, inv = _dft_matrices(seq_len)
    kspec = _hyena_filter_spec(h_time, fwd, seq_len)
    u = _shortconv(p, short_w, short_b, n_seq, seq_len, row_block0)
    yspec = _seqmm(fwd, u, n_seq, 2 * HY_DIM, HY_DIM, f"hyena_fwd_{seq_len}_0")
    y1 = _hyena_inv(inv, yspec, kspec, 0, u, 2 * HY_DIM, u, 0, bias, n_seq, seq_len)
    yspec = _seqmm(fwd, y1, n_seq, 0, HY_DIM, f"hyena_fwd_{seq_len}_1")
    return _hyena_inv(inv, yspec, kspec, 1, y1, 0, u, HY_DIM, bias, n_seq, seq_len)


def _nt_dot(a, b):
    return lax.dot_general(a, b, (((1,), (1,)), ((), ())), preferred_element_type=F32)


def _ctx_attn_kernel(q_ref, k_ref, v_ref, o_ref, ko_ref, vo_ref):
    scale = HEAD_DIM ** -0.5
    ko_ref[...] = k_ref[...]
    vo_ref[...] = v_ref[...]
    for h in range(N_HEADS):
        sl = slice(h * HEAD_DIM, (h + 1) * HEAD_DIM)
        s = _nt_dot(q_ref[:, sl].astype(BF16), k_ref[:, sl].astype(BF16)) * scale
        e = jnp.exp(s - jnp.max(s, axis=-1, keepdims=True))
        pr = e / jnp.sum(e, axis=-1, keepdims=True)
        o_ref[:, sl] = _dot(pr.astype(BF16), v_ref[:, sl].astype(BF16))


def _ctx_attn(p):
    qb = 3 * HY_DIM // NA_DIM
    spec = lambda c: pl.BlockSpec((L_PROMPT, NA_DIM), lambda s: (s, c))
    out = jax.ShapeDtypeStruct((T_PROMPT, NA_DIM), F32)
    return pl.pallas_call(
        _ctx_attn_kernel,
        grid=(N_PROMPT_SEQ,),
        in_specs=[spec(qb), spec(qb + 1), spec(qb + 2)],
        out_specs=[spec(0), spec(0), spec(0)],
        out_shape=[out, out, out],
        compiler_params=_params(("parallel",), 40),
        name="ctx_attn",
    )(p, p, p)


NA_Q_ROWS = 8
NA_K_ROWS = NA_Q_ROWS + WIN_R
NA_ROW_BLOCKS = GRID_ROWS // NA_Q_ROWS
NA_HEADS_PER_STEP = LANES // HEAD_DIM


def _na_key_row0(rb, xp=jnp):
    return xp.clip(rb * NA_Q_ROWS - WIN_R // 2, 0, GRID_ROWS - NA_K_ROWS)


def _na_kernel(q_ref, k_ref, v_ref, kc_ref, vc_ref, b_ref, o_ref):
    scale = HEAD_DIM ** -0.5
    k0 = pl.multiple_of(_na_key_row0(pl.program_id(1)) * GRID_W, GRID_W)
    n_loc = NA_K_ROWS * GRID_W
    for h in range(NA_HEADS_PER_STEP):
        sl = slice(h * HEAD_DIM, (h + 1) * HEAD_DIM)
        q = q_ref[:, sl].astype(BF16)
        s_loc = _nt_dot(q, k_ref[pl.ds(k0, n_loc), sl].astype(BF16)) * scale + b_ref[0, 0, h]
        s_ctx = _nt_dot(q, kc_ref[0, :, sl].astype(BF16)) * scale
        m = jnp.maximum(jnp.max(s_loc, axis=-1, keepdims=True), jnp.max(s_ctx, axis=-1, keepdims=True))
        e_loc = jnp.exp(s_loc - m)
        e_ctx = jnp.exp(s_ctx - m)
        den = jnp.sum(e_loc, axis=-1, keepdims=True) + jnp.sum(e_ctx, axis=-1, keepdims=True)
        o_ref[:, sl] = (_dot((e_loc / den).astype(BF16), v_ref[pl.ds(k0, n_loc), sl].astype(BF16))
                        + _dot((e_ctx / den).astype(BF16), vc_ref[0, :, sl].astype(BF16)))


NA_BIAS_KINDS = 3


def _na_bias_kind(rb):
    return jnp.where(rb == 0, 0, jnp.where(rb == NA_ROW_BLOCKS - 1, 2, 1))


def _na_bias_table(rpb):
    q = np.arange(GRID_W)[:, None]
    kc = np.arange(GRID_W)[None, :]
    cs = np.clip(q - WIN_C // 2, 0, GRID_W - WIN_C)
    col_valid = (kc >= cs) & (kc < cs + WIN_C)
    col_off = np.clip(kc - q, -(WIN_C - 1), WIN_C - 1) + WIN_C - 1
    b = rpb.astype(F32)[:, :, col_off]
    b = jnp.where(jnp.asarray(col_valid)[None, None], b, NEG_BIG)
    neg = jnp.full_like(b, NEG_BIG)
    b_next = jnp.concatenate([b[:, 1:], neg[:, :1]], axis=1)
    pairs = jnp.stack([jnp.concatenate([b, b_next], axis=-1), jnp.concatenate([b, neg], axis=-1),
                       jnp.concatenate([neg, b], axis=-1)], axis=1)

    def geometry(rb):
        q_row = rb * NA_Q_ROWS + np.arange(NA_Q_ROWS)[:, None]
        k_row = _na_key_row0(rb, np) + np.arange(NA_K_ROWS)[None, :]
        row_start = np.clip(q_row - WIN_R // 2, 0, GRID_ROWS - WIN_R)
        row_valid = (k_row >= row_start) & (k_row < row_start + WIN_R)
        return row_valid, k_row - q_row + WIN_R - 1

    for rb in range(2, NA_ROW_BLOCKS - 1):
        assert all(np.array_equal(a, c) for a, c in zip(geometry(rb), geometry(1)))
    geoms = [geometry(rb) for rb in (0, 1, NA_ROW_BLOCKS - 1)]
    kern = functools.partial(_na_bias_kernel, geoms=geoms)
    return pl.pallas_call(
        kern,
        grid=(N_HEADS, NA_BIAS_KINDS),
        in_specs=[pl.BlockSpec((1,) + pairs.shape[1:], lambda h, kind: (h, 0, 0, 0, 0))],
        out_specs=pl.BlockSpec((1, 1, 1, NA_Q_ROWS * GRID_W, NA_K_ROWS * GRID_W),
                               lambda h, kind: (kind, h // NA_HEADS_PER_STEP, h % NA_HEADS_PER_STEP, 0, 0)),
        out_shape=jax.ShapeDtypeStruct((NA_BIAS_KINDS, N_HEADS // NA_HEADS_PER_STEP, NA_HEADS_PER_STEP,
                                        NA_Q_ROWS * GRID_W, NA_K_ROWS * GRID_W), F32),
        compiler_params=_params(("parallel", "arbitrary"), 32),
        name="na_bias",
    )(pairs)


def _na_bias_kernel(p_ref, o_ref, *, geoms):
    for kind, (row_valid, d) in enumerate(geoms):
        @pl.when(pl.program_id(1) == kind)
        def _(row_valid=row_valid, d=d):
            for iq in range(NA_Q_ROWS):
                rows = slice(iq * GRID_W, (iq + 1) * GRID_W)
                for ik in range(0, NA_K_ROWS, 2):
                    cols = slice(ik * GRID_W, (ik + 2) * GRID_W)
                    left, right = row_valid[iq, ik], row_valid[iq, ik + 1]
                    if left and right:
                        o_ref[0, 0, 0, rows, cols] = p_ref[0, 0, int(d[iq, ik])]
                    elif left:
                        o_ref[0, 0, 0, rows, cols] = p_ref[0, 1, int(d[iq, ik])]
                    elif right:
                        o_ref[0, 0, 0, rows, cols] = p_ref[0, 2, int(d[iq, ik + 1])]
                    else:
                        o_ref[0, 0, 0, rows, cols] = jnp.full((GRID_W, 2 * GRID_W), NEG_BIG, F32)


def _na_attn(p, cache_k, cache_v, rpb):
    tq = NA_Q_ROWS * GRID_W
    q_col0 = 3 * HY_DIM // LANES
    k_col0 = q_col0 + NA_DIM // LANES
    v_col0 = k_col0 + NA_DIM // LANES
    q_blk0 = T_PROMPT // tq
    seq0 = T_PROMPT // L_LATENT
    bias = _na_bias_table(rpb)
    ctx_spec = pl.BlockSpec((1, cache_k.shape[1], LANES), lambda hp, rb, b: (b, 0, hp))
    return pl.pallas_call(
        _na_kernel,
        grid=(N_HEADS // NA_HEADS_PER_STEP, NA_ROW_BLOCKS, N_LATENT_SEQ),
        in_specs=[pl.BlockSpec((tq, LANES), lambda hp, rb, b: (q_blk0 + b * NA_ROW_BLOCKS + rb, q_col0 + hp)),
                  pl.BlockSpec((L_LATENT, LANES), lambda hp, rb, b: (seq0 + b, k_col0 + hp)),
                  pl.BlockSpec((L_LATENT, LANES), lambda hp, rb, b: (seq0 + b, v_col0 + hp)),
                  ctx_spec, ctx_spec,
                  pl.BlockSpec((1, 1, NA_HEADS_PER_STEP, tq, NA_K_ROWS * GRID_W),
                               lambda hp, rb, b: (_na_bias_kind(rb), hp, 0, 0, 0))],
        out_specs=pl.BlockSpec((tq, LANES), lambda hp, rb, b: (b * NA_ROW_BLOCKS + rb, hp)),
        out_shape=jax.ShapeDtypeStruct((T_LATENT, NA_DIM), F32),
        compiler_params=_params(("parallel", "parallel", "arbitrary"), 48),
        name="na_attn",
    )(p, p, p, cache_k, cache_v, bias)


def _pool_kernel(prev_ref, cur_ref, next_ref, gw_ref, sc_ref, o_ref, ext_ref, *, tm):
    i = pl.program_id(0)
    row0 = i * tm
    seq_len = jnp.where(row0 < T_PROMPT, L_PROMPT, L_LATENT)
    pos0 = jnp.where(row0 < T_PROMPT, row0 % L_PROMPT, (row0 - T_PROMPT) % L_LATENT)
    first = pos0 == 0
    last = pos0 + tm == seq_len
    h = POOL_HALO
    ext_ref[0:h] = jnp.where(first, 0.0, prev_ref[...])
    ext_ref[h:h + tm] = cur_ref[...]
    ext_ref[h + tm:] = jnp.where(last, 0.0, next_ref[...])
    pos = pos0 + lax.broadcasted_iota(jnp.int32, (tm, 1), 0)
    for g, w in enumerate(POOL_WINDOWS):
        cols = slice(g * POOL_GROUP, (g + 1) * POOL_GROUP)
        acc = ext_ref[h - w // 2:h - w // 2 + tm, cols]
        for j in range(1, w):
            acc = acc + ext_ref[h - w // 2 + j:h - w // 2 + j + tm, cols]
        cnt = jnp.minimum(pos + w // 2, seq_len) - jnp.maximum(pos - w // 2, 0)
        d = acc / cnt.astype(F32) - cur_ref[:, cols]
        o_ref[:, cols] = _dot(d.astype(BF16), gw_ref[g]) * sc_ref[:, cols]


def _pool(u, group_w_bf16, scale):
    tm = 256
    hb = tm // POOL_HALO
    n_halo = T_ALL // POOL_HALO
    kern = functools.partial(_pool_kernel, tm=tm)
    return pl.pallas_call(
        kern,
        grid=(T_ALL // tm,),
        in_specs=[pl.BlockSpec((POOL_HALO, D), lambda i: (jnp.maximum(i * hb - 1, 0), 0)),
                  pl.BlockSpec((tm, D), lambda i: (i, 0)),
                  pl.BlockSpec((POOL_HALO, D), lambda i: (jnp.minimum((i + 1) * hb, n_halo - 1), 0)),
                  pl.BlockSpec((len(POOL_WINDOWS), POOL_GROUP, POOL_GROUP), lambda i: (0, 0, 0)),
                  pl.BlockSpec((1, D), lambda i: (0, 0))],
        out_specs=pl.BlockSpec((tm, D), lambda i: (i, 0)),
        out_shape=jax.ShapeDtypeStruct((T_ALL, D), F32),
        scratch_shapes=[pltpu.VMEM((tm + 2 * POOL_HALO, D), F32)],
        compiler_params=_params(("parallel",), 40),
        name="pool",
    )(u, u, u, group_w_bf16, scale.reshape(1, D))


def _post_mixer_kernel(*refs, n1, n2, n_x, tm):
    a1_refs, a2_refs, x_refs = refs[:n1], refs[n1:n1 + n2], refs[n1 + n2:n1 + n2 + n_x]
    (w_ref, gpost_ref, gate_ref, gpre_ref, sh_ref, sc_ref, rwh_ref, rwl_ref, rb_ref,
     xo_ref, h_ref, lg_ref) = refs[n1 + n2 + n_x:]
    a1 = _token_rows(a1_refs, tm).astype(BF16)
    a2 = _token_rows(a2_refs, tm).astype(BF16)
    k1 = a1.shape[1]
    m = _dot(a1, w_ref[0:k1]) + _dot(a2, w_ref[k1:])
    xn = _token_rows(x_refs, tm) + gate_ref[0] * (_rms(m) * gpost_ref[...])
    xo_ref[...] = xn
    h = _rms(xn) * gpre_ref[...] * (1.0 + sc_ref[0]) + sh_ref[0]
    hh, hl = _split_bf16(h)
    h_ref[...] = _pack_bf16_pairs(hh)
    lg_ref[...] = _dot(hh, rwh_ref[...]) + _dot(hl, rwh_ref[...]) + _dot(hh, rwl_ref[...]) + rb_ref[...]


def _post_mixer(a1_parts, a1_col, a2_parts, a2_col, w_bf16, x_parts, g_post, gate, g_pre, shift, scale,
                rw_hi, rw_lo, rb, name):
    tm = 256
    kh = w_bf16.shape[0] // 2
    row = lambda i: (i, 0)
    const = lambda i: (0, 0)
    mod_spec = pl.BlockSpec((1, 1, D), lambda i: (_mod_row(i * tm), 0, 0))
    vec = pl.BlockSpec((1, D), const)
    kern = functools.partial(_post_mixer_kernel, n1=len(a1_parts), n2=len(a2_parts), n_x=len(x_parts), tm=tm)
    return pl.pallas_call(
        kern,
        grid=(T_ALL // tm,),
        in_specs=_token_specs(a1_parts, tm, kh, a1_col) + _token_specs(a2_parts, tm, kh, a2_col)
                 + _token_specs(x_parts, tm, D, 0) + [
                  pl.BlockSpec((2 * kh, D), const),
                  vec, mod_spec, vec, mod_spec, mod_spec,
                  pl.BlockSpec((D, LANES), const), pl.BlockSpec((D, LANES), const),
                  pl.BlockSpec((1, LANES), const)],
        out_specs=[pl.BlockSpec((tm, D), row), pl.BlockSpec((tm, D // 2), row), pl.BlockSpec((tm, LANES), row)],
        out_shape=[jax.ShapeDtypeStruct((T_ALL, D), F32), jax.ShapeDtypeStruct((T_ALL, D // 2), jnp.uint32),
                   jax.ShapeDtypeStruct((T_ALL, LANES), F32)],
        compiler_params=_params(("parallel",), 48),
        name=name,
    )(*a1_parts, *a2_parts, *x_parts, w_bf16, g_post, gate, g_pre, shift, scale, rw_hi, rw_lo, rb)


def _route_kernel(lg_ref, tri_ref, eidx_ref, gate_ref, rank_ref, cnt_ref, carry_ref):
    @pl.when(pl.program_id(0) == 0)
    def _():
        carry_ref[...] = jnp.zeros_like(carry_ref)

    shape = lg_ref.shape
    lane = lax.broadcasted_iota(jnp.int32, shape, 1).astype(F32)
    lg = jnp.where(lane < N_EXPERTS, lg_ref[...], -jnp.inf)
    multi = jnp.zeros(shape, F32)
    vals, sels = [], []
    eidx = jnp.zeros(shape, F32)
    for k in range(TOP_K):
        m = jnp.max(lg, axis=-1, keepdims=True)
        idx = jnp.min(jnp.where(lg == m, lane, float(LANES)), axis=-1, keepdims=True)
        sel = lane == idx
        multi = jnp.where(sel, 1.0, multi)
        lg = jnp.where(sel, -jnp.inf, lg)
        eidx = jnp.where(lane == k, idx, eidx)
        vals.append(m)
        sels.append(sel)
    exps = [jnp.exp(v - vals[0]) for v in vals]
    den = exps[0] + exps[1] + exps[2] + exps[3]
    rank_all = _dot(tri_ref[...], multi.astype(BF16)) + carry_ref[0:1]
    gate = jnp.zeros(shape, F32)
    rank = jnp.zeros(shape, F32)
    for k in range(TOP_K):
        gate = jnp.where(lane == k, exps[k] / den, gate)
        rank = jnp.where(lane == k, jnp.sum(jnp.where(sels[k], rank_all, 0.0), axis=-1, keepdims=True), rank)
    eidx_ref[...] = eidx.astype(jnp.int32)
    gate_ref[...] = gate
    rank_ref[...] = rank.astype(jnp.int32)
    carry_ref[0:1] = carry_ref[0:1] + jnp.sum(multi, axis=0, keepdims=True)
    cnt_ref[...] = carry_ref[...]


def _route(logits):
    tm = 256
    tri = (np.arange(tm)[:, None] > np.arange(tm)[None, :]).astype(np.float32)
    row = lambda i: (i, 0)
    spec = pl.BlockSpec((tm, LANES), row)
    return pl.pallas_call(
        _route_kernel,
        grid=(T_ALL // tm,),
        in_specs=[spec, pl.BlockSpec((tm, tm), lambda i: (0, 0))],
        out_specs=[spec, spec, spec, pl.BlockSpec((8, LANES), lambda i: (0, 0))],
        out_shape=[jax.ShapeDtypeStruct((T_ALL, LANES), jnp.int32), jax.ShapeDtypeStruct((T_ALL, LANES), F32),
                   jax.ShapeDtypeStruct((T_ALL, LANES), jnp.int32), jax.ShapeDtypeStruct((8, LANES), F32)],
        scratch_shapes=[pltpu.VMEM((8, LANES), F32)],
        compiler_params=_params(("arbitrary",), 32),
        name="route",
    )(logits, jnp.asarray(tri, BF16))


def _dest_kernel(eidx_ref, rank_ref, start_ref, o_ref):
    shape = eidx_ref.shape
    lane = lax.broadcasted_iota(jnp.int32, shape, 1).astype(F32)
    eidx = eidx_ref[...].astype(F32)
    starts = start_ref[...].astype(F32)
    dest = jnp.zeros(shape, F32)
    for k in range(TOP_K):
        e_k = jnp.sum(jnp.where(lane == k, eidx, 0.0), axis=-1, keepdims=True)
        start = jnp.sum(jnp.where(lane == e_k, starts, 0.0), axis=-1, keepdims=True)
        dest = jnp.where(lane == k, start, dest)
    o_ref[...] = jnp.where(lane < TOP_K, dest.astype(jnp.int32) + rank_ref[...], 0)


def _dest(eidx, rank, group_start):
    tm = 1024
    spec = pl.BlockSpec((tm, LANES), lambda i: (i, 0))
    return pl.pallas_call(
        _dest_kernel,
        grid=(T_ALL // tm,),
        in_specs=[spec, spec, pl.BlockSpec((1, LANES), lambda i: (0, 0))],
        out_specs=spec,
        out_shape=jax.ShapeDtypeStruct((T_ALL, LANES), jnp.int32),
        compiler_params=_params(("parallel",), 32),
        name="dest",
    )(eidx, rank, group_start)


def _dispatch_kernel(dest_ref, lo_ref, hi_ref, nu_ref, h_ref, o_ref, zero_ref, sem, zsem, *, tm):
    i = pl.program_id(0)
    base = i * tm * TOP_K

    @pl.when(i == 0)
    def _():
        zero_ref[...] = jnp.zeros_like(zero_ref)

    def issue(g, carry):
        t0 = pl.multiple_of(g * SUBLANES, SUBLANES)
        for s in range(SUBLANES):
            for k in range(TOP_K):
                d = dest_ref[base + (t0 + s) * TOP_K + k]
                pltpu.make_async_copy(h_ref.at[pl.ds(t0 + s, 1)], o_ref.at[pl.ds(d, 1)], sem).start()
        return carry

    def zero_copies(do):
        @pl.when(i < N_EXPERTS)
        def _():
            lo = lo_ref[i]
            n_pad = hi_ref[i] - lo
            head = jnp.minimum((SUBLANES - lo % SUBLANES) % SUBLANES, n_pad)

            def zero_row(r, carry):
                do(pltpu.make_async_copy(zero_ref.at[pl.ds(0, 1)], o_ref.at[pl.ds(lo + r, 1)], zsem))
                return carry

            lax.fori_loop(0, head, zero_row, 0)
            pos = lo + head
            rest = n_pad - head
            size = MOE_TM // 2
            while size >= SUBLANES:
                @pl.when((rest & size) != 0)
                def _(pos=pos, size=size):
                    do(pltpu.make_async_copy(zero_ref.at[pl.ds(0, size)],
                                             o_ref.at[pl.ds(pl.multiple_of(pos, SUBLANES), size)], zsem))

                pos = pos + (rest & size)
                size //= 2

        blk = nu_ref[0] + i - N_EXPERTS

        @pl.when((i >= N_EXPERTS) & (blk < MOE_BLOCKS))
        def _():
            r0 = pl.multiple_of(blk * MOE_TM, MOE_TM)
            do(pltpu.make_async_copy(zero_ref, o_ref.at[pl.ds(r0, MOE_TM)], zsem))

    zero_copies(lambda cp: cp.start())
    lax.fori_loop(0, tm // SUBLANES, issue, 0)
    zero_copies(lambda cp: cp.wait())
    for k in range(TOP_K):
        pltpu.make_async_copy(h_ref, o_ref.at[pl.ds(0, tm)], sem).wait()


def _dispatch(dest_flat, pad_lo, pad_hi, n_used, h):
    tm = 128
    assert T_ALL // tm >= 2 * N_EXPERTS
    kern = functools.partial(_dispatch_kernel, tm=tm)
    return pl.pallas_call(
        kern,
        grid_spec=pltpu.PrefetchScalarGridSpec(
            num_scalar_prefetch=4,
            grid=(T_ALL // tm,),
            in_specs=[pl.BlockSpec((tm, D // 2), lambda i, *_: (i, 0))],
            out_specs=pl.BlockSpec(memory_space=pl.ANY),
            scratch_shapes=[pltpu.VMEM((MOE_TM, D // 2), jnp.uint32), pltpu.SemaphoreType.DMA,
                            pltpu.SemaphoreType.DMA]),
        out_shape=jax.ShapeDtypeStruct((MOE_ROWS, D // 2), jnp.uint32),
        compiler_params=_params(("arbitrary",), 32),
        name="dispatch",
    )(dest_flat, pad_lo, pad_hi, n_used, h)


def _new_weights(be_ref, i):
    return (i == 0) | (be_ref[i] != be_ref[jnp.maximum(i - 1, 0)])


def _for_valid_rows(n_valid, compute):
    many = n_valid > MOE_TM // 2
    pl.when(many)(lambda: compute(MOE_TM))
    pl.when(jnp.logical_not(many))(lambda: compute(MOE_TM // 2))


def _expert_up_kernel(be_ref, nu_ref, bv_ref, x_ref, w_ref, b_ref, o_ref, wbf_ref):
    i = pl.program_id(1)
    tn = o_ref.shape[1]

    def compute(rows):
        x = jnp.concatenate(_unpack_bf16_pairs(x_ref[0:rows]), axis=1)
        even = lax.broadcasted_iota(jnp.int32, (rows, LANES), 1) % 2 == 0
        for c in range(tn // LANES):
            cols = slice(2 * LANES * c, 2 * LANES * (c + 1))
            hb = _dot(x, wbf_ref[:, cols]) + b_ref[0, 0, :, cols]
            first, second = hb[:, :LANES], hb[:, LANES:]
            g = jnp.where(even, first, pltpu.roll(second, 1, 1))
            lin = jnp.where(even, pltpu.roll(first, LANES - 1, 1), second)
            g = jnp.minimum(g, SWIGLU_LIMIT)
            lin = jnp.clip(lin, -SWIGLU_LIMIT, SWIGLU_LIMIT)
            act = g * jax.nn.sigmoid(SWIGLU_ALPHA * g) * (lin + 1.0)
            o_ref[0:rows, LANES * c:LANES * (c + 1)] = act.astype(BF16)
        if rows < MOE_TM:
            o_ref[rows:] = jnp.zeros((MOE_TM - rows, tn), BF16)

    @pl.when(i < nu_ref[0])
    def _():
        @pl.when(_new_weights(be_ref, i))
        def _():
            wbf_ref[...] = w_ref[0, 0].astype(BF16)

        _for_valid_rows(bv_ref[i], compute)

    @pl.when(i >= nu_ref[0])
    def _():
        o_ref[...] = jnp.zeros_like(o_ref)


def _expert_up(blk_e, n_used, blk_valid, xs, w_gu, b_gu, layer):
    tn = FF_TILE
    blk = lambda j, i, be, nu, bv: (jnp.minimum(i, nu[0] - 1), 0)
    return pl.pallas_call(
        _expert_up_kernel,
        grid_spec=pltpu.PrefetchScalarGridSpec(
            num_scalar_prefetch=3,
            grid=(D_FF // tn, MOE_BLOCKS),
            in_specs=[pl.BlockSpec((MOE_TM, D // 2), blk),
                      pl.BlockSpec((1, 1, D, 2 * tn), lambda j, i, be, nu, bv: (layer, be[i], 0, j)),
                      pl.BlockSpec((1, 1, 1, 2 * tn), lambda j, i, be, nu, bv: (layer, be[i], 0, j))],
            out_specs=pl.BlockSpec((MOE_TM, tn), lambda j, i, be, nu, bv: (i, j)),
            scratch_shapes=[pltpu.VMEM((D, 2 * tn), BF16)]),
        out_shape=jax.ShapeDtypeStruct((MOE_ROWS, D_FF), BF16),
        compiler_params=_params(("arbitrary", "arbitrary"), 56),
        name="expert_up",
    )(blk_e, n_used, blk_valid, xs, w_gu, b_gu.reshape(N_LAYERS, N_EXPERTS, 1, 2 * D_FF))


DOWN_TILE = D


def _expert_down_kernel(be_ref, nu_ref, bv_ref, a_ref, w_ref, b_ref, o_ref, wbf_ref, wtmp_ref):
    i = pl.program_id(1)

    def compute(rows):
        y = _dot(a_ref[0:rows], wbf_ref[...]) + b_ref[0, 0]
        o_ref[0:rows] = _pack_bf16_pairs(y.astype(BF16))
        if rows < MOE_TM:
            o_ref[rows:] = jnp.zeros((MOE_TM - rows, o_ref.shape[1]), jnp.uint32)

    @pl.when(i < nu_ref[0])
    def _():
        @pl.when(_new_weights(be_ref, i))
        def _():
            half = LANES // 2
            for g in range(D_FF // LANES):
                r0 = LANES * g
                for c in range(wbf_ref.shape[1] // LANES):
                    cols = slice(LANES * c, LANES * (c + 1))
                    wtmp_ref[c, pl.ds(0, half, stride=2), :] = w_ref[0, 0, r0:r0 + half, cols]
                    wtmp_ref[c, pl.ds(1, half, stride=2), :] = w_ref[0, 0, r0 + half:r0 + LANES, cols]
                    wbf_ref[r0:r0 + LANES, cols] = wtmp_ref[c].astype(BF16)

        _for_valid_rows(bv_ref[i], compute)

    @pl.when(i >= nu_ref[0])
    def _():
        o_ref[...] = jnp.zeros_like(o_ref)


def _expert_down(blk_e, n_used, blk_valid, a, w_down, b_down, layer):
    tn = DOWN_TILE
    blk = lambda j, i, be, nu, bv: (jnp.minimum(i, nu[0] - 1), 0)
    return pl.pallas_call(
        _expert_down_kernel,
        grid_spec=pltpu.PrefetchScalarGridSpec(
            num_scalar_prefetch=3,
            grid=(D // tn, MOE_BLOCKS),
            in_specs=[pl.BlockSpec((MOE_TM, D_FF), blk),
                      pl.BlockSpec((1, 1, D_FF, tn), lambda j, i, be, nu, bv: (layer, be[i], 0, j)),
                      pl.BlockSpec((1, 1, 1, tn), lambda j, i, be, nu, bv: (layer, be[i], 0, j))],
            out_specs=pl.BlockSpec((MOE_TM, tn // 2), lambda j, i, be, nu, bv: (i, j)),
            scratch_shapes=[pltpu.VMEM((D_FF, tn), BF16), pltpu.VMEM((tn // LANES, LANES, LANES), F32)]),
        out_shape=jax.ShapeDtypeStruct((MOE_ROWS, D // 2), jnp.uint32),
        compiler_params=_params(("arbitrary", "arbitrary"), 60),
        name="expert_down",
    )(blk_e, n_used, blk_valid, a, w_down, b_down.reshape(N_LAYERS, N_EXPERTS, 1, D))


def _combine_kernel(dest_ref, y_ref, gate_ref, x_ref, gpost_ref, g2_ref, *rest, tm):
    (*o_refs, buf_ref, sem) = rest
    i = pl.program_id(0)

    def gather(step, slot):
        base = step * tm * TOP_K

        def issue(g, carry):
            t0 = pl.multiple_of(g * SUBLANES, SUBLANES)
            for s in range(SUBLANES):
                for k in range(TOP_K):
                    d = dest_ref[base + (t0 + s) * TOP_K + k]
                    pltpu.make_async_copy(y_ref.at[pl.ds(d, 1)], buf_ref.at[slot, k, pl.ds(t0 + s, 1)],
                                          sem.at[slot]).start()
            return carry

        lax.fori_loop(0, tm // SUBLANES, issue, 0)

    @pl.when(i == 0)
    def _():
        gather(0, 0)

    @pl.when(i + 1 < pl.num_programs(0))
    def _():
        gather(i + 1, (i + 1) % 2)

    slot = i % 2
    for k in range(TOP_K):
        pltpu.make_async_copy(y_ref.at[pl.ds(0, tm)], buf_ref.at[slot, k], sem.at[slot]).wait()

    def expert_rows(k):
        hi, lo = _unpack_bf16_pairs(buf_ref[slot, k])
        tw = DOWN_TILE // 2
        parts = []
        for j in range(D // DOWN_TILE):
            parts += [hi[:, j * tw:(j + 1) * tw], lo[:, j * tw:(j + 1) * tw]]
        return jnp.concatenate(parts, axis=1).astype(F32)

    f = gate_ref[:, 0:1] * expert_rows(0)
    for k in range(1, TOP_K):
        f = f + gate_ref[:, k:k + 1] * expert_rows(k)
    out = x_ref[...] + g2_ref[0] * (_rms(f) * gpost_ref[...])
    if len(o_refs) == 1:
        o_refs[0][...] = out
    else:
        in_prompt = i * tm < T_PROMPT
        @pl.when(in_prompt)
        def _():
            o_refs[0][...] = out

        @pl.when(jnp.logical_not(in_prompt))
        def _():
            o_refs[1][...] = out


def _combine(dest_flat, ys, gate, x, g_post, gate2, name, split_out=False):
    tm = 128
    kern = functools.partial(_combine_kernel, tm=tm)
    n_prompt_tiles = T_PROMPT // tm
    if split_out:
        out_specs = [pl.BlockSpec((tm, D), lambda i, d: (jnp.minimum(i, n_prompt_tiles - 1), 0)),
                     pl.BlockSpec((tm, D), lambda i, d: (jnp.maximum(i - n_prompt_tiles, 0), 0))]
        out_shape = [jax.ShapeDtypeStruct((T_PROMPT, D), F32), jax.ShapeDtypeStruct((T_LATENT, D), F32)]
    else:
        out_specs = pl.BlockSpec((tm, D), lambda i, d: (i, 0))
        out_shape = jax.ShapeDtypeStruct((T_ALL, D), F32)
    return pl.pallas_call(
        kern,
        grid_spec=pltpu.PrefetchScalarGridSpec(
            num_scalar_prefetch=1,
            grid=(T_ALL // tm,),
            in_specs=[pl.BlockSpec(memory_space=pl.ANY),
                      pl.BlockSpec((tm, LANES), lambda i, d: (i, 0)),
                      pl.BlockSpec((tm, D), lambda i, d: (i, 0)),
                      pl.BlockSpec((1, D), lambda i, d: (0, 0)),
                      pl.BlockSpec((1, 1, D), lambda i, d: (_mod_row(i * tm), 0, 0))],
            out_specs=out_specs,
            scratch_shapes=[pltpu.VMEM((2, TOP_K, tm, D // 2), jnp.uint32), pltpu.SemaphoreType.DMA((2,))]),
        out_shape=out_shape,
        compiler_params=_params(("arbitrary",), 32),
        name=name,
    )(dest_flat, ys, gate, x, g_post, gate2)


def _moe(h, logits, x, g_post, gate2, w_gu, b_gu, w_down, b_down, layer):
    eidx, gate, rank, counts = _route(logits)
    cnt = counts[0, :N_EXPERTS].astype(jnp.int32)
    padded = (cnt + MOE_TM - 1) // MOE_TM * MOE_TM
    group_end = jnp.cumsum(padded)
    group_start = jnp.zeros((1, LANES), jnp.int32).at[0, :N_EXPERTS].set(group_end - padded)
    n_used = group_end[-1] // MOE_TM
    blk_row0 = jnp.arange(MOE_BLOCKS, dtype=jnp.int32) * MOE_TM
    blk_row0 = jnp.minimum(blk_row0, (n_used - 1) * MOE_TM)
    blk_e = jnp.sum(group_end[None, :] <= blk_row0[:, None], axis=1).astype(jnp.int32)
    n_used = n_used.astype(jnp.int32).reshape(1)
    filled_end = group_end - padded + cnt
    blk_valid = jnp.clip(filled_end[jnp.minimum(blk_e, N_EXPERTS - 1)] - blk_row0, 0, MOE_TM).astype(jnp.int32)
    dest = _dest(eidx, rank, group_start)[:, :TOP_K].reshape(N_ASSIGN)
    xs = _dispatch(dest, filled_end, group_end, n_used, h)
    a = _expert_up(blk_e, n_used, blk_valid, xs, w_gu, b_gu, layer)
    ys = _expert_down(blk_e, n_used, blk_valid, a, w_down, b_down, layer)
    return _combine(dest, ys, gate, x, g_post, gate2, f"combine_{layer}", split_out=layer == N_LAYERS - 1)


def kernel(x_prompt, x_sample, cache_k, cache_v, c, c_ctx, ada_w, ada_b, norm_mix_pre, norm_mix_post,
           norm_ffn_pre, norm_ffn_post, ab_in_w, ab_out_w, hy_short_w, hy_short_b, hy_w1, hy_b1, hy_w2,
           hy_b2, hy_w3, hy_freq, hy_bias, na_rpb, pool_in_w, pool_group_w, pool_scale, pool_out_w,
           router_w, router_b, exp_w_gu, exp_b_gu, exp_w_down, exp_b_down):
    x = [x_prompt.reshape(T_PROMPT, D), x_sample.reshape(T_LATENT, D)]
    cvec = jnp.concatenate([c_ctx[None], c, jnp.zeros((8 - 1 - N_LATENT_SEQ, D), F32)], axis=0)
    mods = _adaln(cvec, ada_w, ada_b).reshape(N_LAYERS, 8, 6, D)

    new_k = new_v = None
    for layer in range(N_LAYERS):
        sh1, sc1, g1, sh2, sc2, g2 = [mods[layer, :, k][:, None, :] for k in range(6)]
        row = lambda a: a[layer].reshape(1, D)
        if layer % 2 == 0:
            j = layer // 2
            p = _modmm(x, row(norm_mix_pre), sh1, sc1, ab_in_w[j].astype(BF16), "ab_in")
            filt = (hy_w1[j], hy_b1[j], hy_w2[j], hy_b2[j], hy_w3[j], hy_freq[j])
            hy = (hy_short_w[j], hy_short_b[j], hy_bias[j])
            yh_p = _hyena(p, *hy, _hyena_filter_time(L_PROMPT, *filt), N_PROMPT_SEQ, L_PROMPT, 0)
            yh_s = _hyena(p, *hy, _hyena_filter_time(L_LATENT, *filt), N_LATENT_SEQ, L_LATENT,
                          T_PROMPT // L_LATENT)
            ya_p, new_k, new_v = _ctx_attn(p)
            ck = cache_k[:, j].reshape(N_LATENT_SEQ, -1, NA_DIM)
            cv = cache_v[:, j].reshape(N_LATENT_SEQ, -1, NA_DIM)
            ya_s = _na_attn(p, ck, cv, na_rpb[j])
            a1 = [yh_p, yh_s]
            a2 = [ya_p, ya_s]
            a1_col = a2_col = 0
            w_out = ab_out_w[j].astype(BF16)
        else:
            j = layer // 2
            u = _modmm(x, row(norm_mix_pre), sh1, sc1, pool_in_w[j].astype(BF16), "pool_in")
            a1 = a2 = [_pool(u, pool_group_w[j].astype(BF16), pool_scale[j])]
            a1_col, a2_col = 0, 1
            w_out = pool_out_w[j].astype(BF16)
        rw = jnp.zeros((D, LANES), F32).at[:, :N_EXPERTS].set(router_w[layer])
        rw_hi, rw_lo = _split_bf16(rw)
        rb = jnp.zeros((1, LANES), F32).at[0, :N_EXPERTS].set(router_b[layer])
        xm, h, logits = _post_mixer(a1, a1_col, a2, a2_col, w_out, x, row(norm_mix_post), g1,
                                    row(norm_ffn_pre), sh2, sc2, rw_hi, rw_lo, rb, f"post_mixer_{layer}")
        x = _moe(h, logits, xm, row(norm_ffn_post), g2, exp_w_gu, exp_b_gu, exp_w_down, exp_b_down, layer)
        x = list(x) if isinstance(x, (list, tuple)) else [x]

    y_prompt = x[0].reshape(x_prompt.shape)
    y_sample = x[1].reshape(x_sample.shape)
    kv_shape = (N_PROMPT_SEQ, 1, L_PROMPT, N_HEADS, HEAD_DIM)
    return y_prompt, y_sample, new_k.reshape(kv_shape), new_v.reshape(kv_shape)
```

```python
import functools
import math

import jax
import jax.numpy as jnp
import numpy as np
from jax import lax
from jax.experimental import pallas as pl
from jax.experimental.pallas import tpu as pltpu

F32 = jnp.float32
BF16 = jnp.bfloat16

D = 2048
N_PROMPT_SEQ = 32
L_PROMPT = 256
N_LATENT_SEQ = 4
L_LATENT = 2048
T_PROMPT = N_PROMPT_SEQ * L_PROMPT
T_LATENT = N_LATENT_SEQ * L_LATENT
T_ALL = T_PROMPT + T_LATENT
N_LAYERS = 2
GRID_W = 64
GRID_ROWS = L_LATENT // GRID_W
HY_DIM = 1024
HY_EMB = 33
HY_DECAY_TARGET = 1e-2
HY_FAST_DECAY = 0.3
HY_SLOW_DECAY = 1.5
N_HEADS = 16
HEAD_DIM = 64
NA_DIM = N_HEADS * HEAD_DIM
WIN_R = 8
WIN_C = 16
AB_IN = 3 * HY_DIM + 3 * NA_DIM
POOL_WINDOWS = (2, 4, 8, 16)
POOL_GROUP = D // len(POOL_WINDOWS)
POOL_HALO = 8
N_EXPERTS = 32
TOP_K = 4
D_FF = D
SWIGLU_LIMIT = 7.0
SWIGLU_ALPHA = 1.702
RMS_EPS = 1e-6
NEG_BIG = -1e30

LANES = 128
SUBLANES = 8
MOE_TM = 512
FF_TILE = 1024
N_ASSIGN = T_ALL * TOP_K
MOE_BLOCKS = -(-(N_ASSIGN + N_EXPERTS * (MOE_TM - 1)) // MOE_TM)
MOE_ROWS = MOE_BLOCKS * MOE_TM
MIB = 1 << 20


def _params(semantics, vmem_mib):
    return pltpu.CompilerParams(dimension_semantics=semantics, vmem_limit_bytes=vmem_mib * MIB)


def _mod_row(row0):
    return jnp.where(row0 < T_PROMPT, 0, 1 + (row0 - T_PROMPT) // L_LATENT)


def _rms(x):
    return x * lax.rsqrt(jnp.mean(x * x, axis=-1, keepdims=True) + RMS_EPS)


def _split_bf16(x):
    hi = x.astype(BF16)
    lo = (x - hi.astype(F32)).astype(BF16)
    return hi, lo


def _dot(a, b):
    return jnp.dot(a, b, preferred_element_type=F32)


def _pack_bf16_pairs(x_bf16):
    half = x_bf16.shape[1] // 2
    bits = lax.bitcast_convert_type(x_bf16.astype(F32), jnp.uint32)
    return (bits[:, :half] & jnp.uint32(0xFFFF0000)) | (bits[:, half:] >> 16)


def _unpack_bf16_pairs(packed):
    hi = lax.bitcast_convert_type(packed & jnp.uint32(0xFFFF0000), F32).astype(BF16)
    lo = lax.bitcast_convert_type(packed << 16, F32).astype(BF16)
    return hi, lo


def _adaln_kernel(cv_ref, w_ref, b_ref, o_ref):
    s = jax.nn.silu(cv_ref[...]).astype(BF16)
    o_ref[0] = _dot(s, w_ref[0].astype(BF16)) + b_ref[0]


def _adaln(cvec, ada_w, ada_b):
    n = ada_w.shape[-1]
    tn = 1024
    return pl.pallas_call(
        _adaln_kernel,
        grid=(N_LAYERS, n // tn),
        in_specs=[pl.BlockSpec((8, D), lambda l, j: (0, 0)),
                  pl.BlockSpec((1, D, tn), lambda l, j: (l, 0, j)),
                  pl.BlockSpec((1, 1, tn), lambda l, j: (l, 0, j))],
        out_specs=pl.BlockSpec((1, 8, tn), lambda l, j: (l, 0, j)),
        out_shape=jax.ShapeDtypeStruct((N_LAYERS, 8, n), F32),
        compiler_params=_params(("parallel", "arbitrary"), 40),
        name="adaln",
    )(cvec, ada_w, ada_b.reshape(N_LAYERS, 1, n))


def _token_specs(parts, tm, width, col):
    if len(parts) == 1:
        return [pl.BlockSpec((tm, width), lambda i, *_: (i, col))]
    n_prompt_tiles = T_PROMPT // tm
    return [pl.BlockSpec((tm, width), lambda i, *_: (jnp.minimum(i, n_prompt_tiles - 1), col)),
            pl.BlockSpec((tm, width), lambda i, *_: (jnp.maximum(i - n_prompt_tiles, 0), col))]


def _token_rows(part_refs, tm):
    if len(part_refs) == 1:
        return part_refs[0][...]
    return jnp.where(pl.program_id(0) * tm < T_PROMPT, part_refs[0][...], part_refs[1][...])


def _modmm_kernel(*refs, n_x, tm):
    x_refs = refs[:n_x]
    g_ref, sh_ref, sc_ref, w_ref, o_ref, h_ref = refs[n_x:]

    @pl.when(pl.program_id(1) == 0)
    def _():
        h = _rms(_token_rows(x_refs, tm)) * g_ref[...] * (1.0 + sc_ref[0]) + sh_ref[0]
        h_ref[...] = h.astype(BF16)

    o_ref[...] = _dot(h_ref[...], w_ref[...])


def _modmm(x_parts, g, shift, scale, w_bf16, name):
    n = w_bf16.shape[1]
    tm, tn = (1024, 1024) if len(x_parts) == 1 else (512, 2048)
    mod_spec = pl.BlockSpec((1, 1, D), lambda i, j: (_mod_row(i * tm), 0, 0))
    return pl.pallas_call(
        functools.partial(_modmm_kernel, n_x=len(x_parts), tm=tm),
        grid=(T_ALL // tm, n // tn),
        in_specs=_token_specs(x_parts, tm, D, 0) + [
                  pl.BlockSpec((1, D), lambda i, j: (0, 0)),
                  mod_spec, mod_spec,
                  pl.BlockSpec((D, tn), lambda i, j: (0, j))],
        out_specs=pl.BlockSpec((tm, tn), lambda i, j: (i, j)),
        out_shape=jax.ShapeDtypeStruct((T_ALL, n), F32),
        scratch_shapes=[pltpu.VMEM((tm, D), BF16)],
        compiler_params=_params(("parallel", "arbitrary"), 48),
        name=name,
    )(*x_parts, g, shift, scale, w_bf16)


def _shortconv_kernel(z_ref, w_ref, b_ref, o_ref):
    z = z_ref[...]
    n = z.shape[0]
    row = lax.broadcasted_iota(jnp.int32, z.shape, 0)
    prev = jnp.where(row == 0, 0.0, pltpu.roll(z, 1, 0))
    nxt = jnp.where(row == n - 1, 0.0, pltpu.roll(z, n - 1, 0))
    o_ref[...] = w_ref[0:1] * prev + w_ref[1:2] * z + w_ref[2:3] * nxt + b_ref[...]


def _shortconv(p, w, b, n_seq, seq_len, row_block0):
    c = 3 * HY_DIM
    tc = 512 if seq_len > 512 else c
    return pl.pallas_call(
        _shortconv_kernel,
        grid=(n_seq, c // tc),
        in_specs=[pl.BlockSpec((seq_len, tc), lambda s, j: (row_block0 + s, j)),
                  pl.BlockSpec((3, tc), lambda s, j: (0, j)),
                  pl.BlockSpec((1, tc), lambda s, j: (0, j))],
        out_specs=pl.BlockSpec((seq_len, tc), lambda s, j: (s, j)),
        out_shape=jax.ShapeDtypeStruct((n_seq * seq_len, c), F32),
        compiler_params=_params(("parallel", "parallel"), 40),
        name=f"shortconv_{seq_len}",
    )(p, w, b.reshape(1, c))


def _seqmm_kernel(a_ref, x_ref, o_ref):
    o_ref[...] = _dot(a_ref[...], x_ref[...].astype(BF16))


def _seqmm(a_bf16, x, n_seq, col0, n_cols, name):
    m, k = a_bf16.shape
    tm = min(m, 1024)
    tn = 512 if k > 512 else 1024
    col_block0 = col0 // tn
    return pl.pallas_call(
        _seqmm_kernel,
        grid=(n_seq, n_cols // tn, m // tm),
        in_specs=[pl.BlockSpec((tm, k), lambda s, j, i: (i, 0)),
                  pl.BlockSpec((k, tn), lambda s, j, i: (s, col_block0 + j))],
        out_specs=pl.BlockSpec((tm, tn), lambda s, j, i: (s * (m // tm) + i, j)),
        out_shape=jax.ShapeDtypeStruct((n_seq * m, n_cols), F32),
        compiler_params=_params(("parallel", "parallel", "arbitrary"), 40),
        name=name,
    )(a_bf16, x)


def _hyena_inv_kernel(g_ref, y_ref, k_ref, yin_ref, gate_ref, bias_ref, o_ref, z_ref, *, seq_len):
    @pl.when(pl.program_id(2) == 0)
    def _():
        yc, ys = y_ref[0:seq_len], y_ref[seq_len:]
        kc, ks = k_ref[0:seq_len], k_ref[seq_len:]
        z_ref[0:seq_len] = (yc * kc - ys * ks).astype(BF16)
        z_ref[seq_len:] = (yc * ks + ys * kc).astype(BF16)

    conv = _dot(g_ref[...], z_ref[...]) * (1.0 / seq_len)
    o_ref[...] = gate_ref[...] * (conv + yin_ref[...] * bias_ref[...])


def _hyena_inv(g_bf16, yspec, kspec, order, yin, yin_col0, u, gate_col0, bias, n_seq, seq_len):
    tn = 256 if seq_len > 512 else 1024
    tt = min(seq_len, 512)
    nb = HY_DIM // tn
    kern = functools.partial(_hyena_inv_kernel, seq_len=seq_len)
    return pl.pallas_call(
        kern,
        grid=(n_seq, nb, seq_len // tt),
        in_specs=[pl.BlockSpec((tt, 2 * seq_len), lambda s, j, t: (t, 0)),
                  pl.BlockSpec((2 * seq_len, tn), lambda s, j, t: (s, j)),
                  pl.BlockSpec((2 * seq_len, tn), lambda s, j, t: (0, order * nb + j)),
                  pl.BlockSpec((tt, tn), lambda s, j, t: (s * (seq_len // tt) + t, yin_col0 // tn + j)),
                  pl.BlockSpec((tt, tn), lambda s, j, t: (s * (seq_len // tt) + t, gate_col0 // tn + j)),
                  pl.BlockSpec((1, tn), lambda s, j, t: (0, order * nb + j))],
        out_specs=pl.BlockSpec((tt, tn), lambda s, j, t: (s * (seq_len // tt) + t, j)),
        out_shape=jax.ShapeDtypeStruct((n_seq * seq_len, HY_DIM), F32),
        scratch_shapes=[pltpu.VMEM((2 * seq_len, tn), BF16)],
        compiler_params=_params(("parallel", "parallel", "arbitrary"), 48),
        name=f"hyena_inv_{seq_len}_{order}",
    )(g_bf16, yspec, kspec, yin, u, bias.reshape(1, 2 * HY_DIM))


def _dft_matrices(seq_len):
    f = lax.broadcasted_iota(jnp.int32, (seq_len, seq_len), 0)
    s = lax.broadcasted_iota(jnp.int32, (seq_len, seq_len), 1)
    ang = (((2 * f + 1) * s) % (4 * seq_len)).astype(F32) * (math.pi / (2 * seq_len))
    c, sn = jnp.cos(ang), jnp.sin(ang)
    fwd = jnp.concatenate([c, sn], axis=0).astype(BF16)
    inv = jnp.concatenate([c.T, sn.T], axis=1).astype(BF16)
    return fwd, inv


def _hyena_filter_time(seq_len, w1, b1, w2, b2, w3, freq):
    hp = lax.Precision.HIGHEST
    t = jnp.linspace(0.0, 1.0, seq_len, dtype=F32)[:, None]
    bands = (HY_EMB - 1) // 2
    ang = 2.0 * math.pi * jnp.arange(seq_len, dtype=F32)[:, None] / seq_len
    fb = jnp.linspace(1e-4, bands - 1, bands, dtype=F32)[None, :]
    z = jnp.concatenate([t, jnp.cos(fb * ang), -jnp.sin(fb * ang)], axis=-1)
    h = jnp.sin(freq * (jnp.dot(z, w1, precision=hp) + b1))
    h = jnp.sin(freq * (jnp.dot(h, w2, precision=hp) + b2))
    h = jnp.dot(h, w3, precision=hp).reshape(seq_len, 2, 2, HY_DIM)
    deltas = jnp.linspace(math.log(HY_DECAY_TARGET) / HY_SLOW_DECAY,
                          math.log(HY_DECAY_TARGET) / HY_FAST_DECAY, HY_DIM, dtype=F32)
    h = h * jnp.exp(-t * jnp.abs(deltas))[:, None, None, :]
    return h * lax.rsqrt(jnp.sum(h * h, axis=(0, 1), keepdims=True))


def _filter_spec_kernel(a_ref, hf_ref, hb_ref, o_ref, *, seq_len, tm):
    sign = jnp.where(pl.program_id(1) * tm < seq_len, 1.0, -1.0)
    hb = hb_ref[...]
    hb = jnp.where(lax.broadcasted_iota(jnp.int32, hb.shape, 0) == 0, 0.0, hb)
    hi, lo = _split_bf16(hf_ref[...] + sign * hb)
    o_ref[...] = _dot(a_ref[...], hi) + _dot(a_ref[...], lo)


def _hyena_filter_spec(h, fwd_bf16, seq_len):
    h2d = h.reshape(seq_len, 4 * HY_DIM)
    tm = min(seq_len, 1024)
    tn = 512
    nb = 2 * HY_DIM // tn
    kern = functools.partial(_filter_spec_kernel, seq_len=seq_len, tm=tm)
    return pl.pallas_call(
        kern,
        grid=(nb, 2 * seq_len // tm),
        in_specs=[pl.BlockSpec((tm, seq_len), lambda j, i: (i, 0)),
                  pl.BlockSpec((seq_len, tn), lambda j, i: (0, j)),
                  pl.BlockSpec((seq_len, tn), lambda j, i: (0, nb + j))],
        out_specs=pl.BlockSpec((tm, tn), lambda j, i: (i, j)),
        out_shape=jax.ShapeDtypeStruct((2 * seq_len, 2 * HY_DIM), F32),
        compiler_params=_params(("parallel", "arbitrary"), 40),
        name=f"filter_spec_{seq_len}",
    )(fwd_bf16, h2d, h2d)


def _hyena(p, short_w, short_b, bias, h_time, n_seq, seq_len, row_block0):
    fwd, inv = _dft_matrices(seq_len)
    kspec = _hyena_filter_spec(h_time, fwd, seq_len)
    u = _shortconv(p, short_w, short_b, n_seq, seq_len, row_block0)
    yspec = _seqmm(fwd, u, n_seq, 2 * HY_DIM, HY_DIM, f"hyena_fwd_{seq_len}_0")
    y1 = _hyena_inv(inv, yspec, kspec, 0, u, 2 * HY_DIM, u, 0, bias, n_seq, seq_len)
    yspec = _seqmm(fwd, y1, n_seq, 0, HY_DIM, f"hyena_fwd_{seq_len}_1")
    return _hyena_inv(inv, yspec, kspec, 1, y1, 0, u, HY_DIM, bias, n_seq, seq_len)


def _nt_dot(a, b):
    return lax.dot_general(a, b, (((1,), (1,)), ((), ())), preferred_element_type=F32)


def _ctx_attn_kernel(q_ref, k_ref, v_ref, o_ref, ko_ref, vo_ref):
    scale = HEAD_DIM ** -0.5
    ko_ref[...] = k_ref[...]
    vo_ref[...] = v_ref[...]
    for h in range(N_HEADS):
        sl = slice(h * HEAD_DIM, (h + 1) * HEAD_DIM)
        s = _nt_dot(q_ref[:, sl].astype(BF16), k_ref[:, sl].astype(BF16)) * scale
        e = jnp.exp(s - jnp.max(s, axis=-1, keepdims=True))
        pr = e / jnp.sum(e, axis=-1, keepdims=True)
        o_ref[:, sl] = _dot(pr.astype(BF16), v_ref[:, sl].astype(BF16))


def _ctx_attn(p):
    qb = 3 * HY_DIM // NA_DIM
    spec = lambda c: pl.BlockSpec((L_PROMPT, NA_DIM), lambda s: (s, c))
    out = jax.ShapeDtypeStruct((T_PROMPT, NA_DIM), F32)
    return pl.pallas_call(
        _ctx_attn_kernel,
        grid=(N_PROMPT_SEQ,),
        in_specs=[spec(qb), spec(qb + 1), spec(qb + 2)],
        out_specs=[spec(0), spec(0), spec(0)],
        out_shape=[out, out, out],
        compiler_params=_params(("parallel",), 40),
        name="ctx_attn",
    )(p, p, p)


NA_Q_ROWS = 8
NA_K_ROWS = NA_Q_ROWS + WIN_R
NA_ROW_BLOCKS = GRID_ROWS // NA_Q_ROWS
NA_HEADS_PER_STEP = LANES // HEAD_DIM


def _na_key_row0(rb, xp=jnp):
    return xp.clip(rb * NA_Q_ROWS - WIN_R // 2, 0, GRID_ROWS - NA_K_ROWS)


def _na_kernel(q_ref, k_ref, v_ref, kc_ref, vc_ref, b_ref, o_ref):
    scale = HEAD_DIM ** -0.5
    k0 = pl.multiple_of(_na_key_row0(pl.program_id(1)) * GRID_W, GRID_W)
    n_loc = NA_K_ROWS * GRID_W
    for h in range(NA_HEADS_PER_STEP):
        sl = slice(h * HEAD_DIM, (h + 1) * HEAD_DIM)
        q = q_ref[:, sl].astype(BF16)
        s_loc = _nt_dot(q, k_ref[pl.ds(k0, n_loc), sl].astype(BF16)) * scale + b_ref[0, 0, h]
        s_ctx = _nt_dot(q, kc_ref[0, :, sl].astype(BF16)) * scale
        m = jnp.maximum(jnp.max(s_loc, axis=-1, keepdims=True), jnp.max(s_ctx, axis=-1, keepdims=True))
        e_loc = jnp.exp(s_loc - m)
        e_ctx = jnp.exp(s_ctx - m)
        den = jnp.sum(e_loc, axis=-1, keepdims=True) + jnp.sum(e_ctx, axis=-1, keepdims=True)
        o_ref[:, sl] = (_dot((e_loc / den).astype(BF16), v_ref[pl.ds(k0, n_loc), sl].astype(BF16))
                        + _dot((e_ctx / den).astype(BF16), vc_ref[0, :, sl].astype(BF16)))


NA_BIAS_KINDS = 3


def _na_bias_kind(rb):
    return jnp.where(rb == 0, 0, jnp.where(rb == NA_ROW_BLOCKS - 1, 2, 1))


def _na_bias_table(rpb):
    q = np.arange(GRID_W)[:, None]
    kc = np.arange(GRID_W)[None, :]
    cs = np.clip(q - WIN_C // 2, 0, GRID_W - WIN_C)
    col_valid = (kc >= cs) & (kc < cs + WIN_C)
    col_off = np.clip(kc - q, -(WIN_C - 1), WIN_C - 1) + WIN_C - 1
    b = rpb.astype(F32)[:, :, col_off]
    b = jnp.where(jnp.asarray(col_valid)[None, None], b, NEG_BIG)
    neg = jnp.full_like(b, NEG_BIG)
    b_next = jnp.concatenate([b[:, 1:], neg[:, :1]], axis=1)
    pairs = jnp.stack([jnp.concatenate([b, b_next], axis=-1), jnp.concatenate([b, neg], axis=-1),
                       jnp.concatenate([neg, b], axis=-1)], axis=1)

    def geometry(rb):
        q_row = rb * NA_Q_ROWS + np.arange(NA_Q_ROWS)[:, None]
        k_row = _na_key_row0(rb, np) + np.arange(NA_K_ROWS)[None, :]
        row_start = np.clip(q_row - WIN_R // 2, 0, GRID_ROWS - WIN_R)
        row_valid = (k_row >= row_start) & (k_row < row_start + WIN_R)
        return row_valid, k_row - q_row + WIN_R - 1

    for rb in range(2, NA_ROW_BLOCKS - 1):
        assert all(np.array_equal(a, c) for a, c in zip(geometry(rb), geometry(1)))
    geoms = [geometry(rb) for rb in (0, 1, NA_ROW_BLOCKS - 1)]
    kern = functools.partial(_na_bias_kernel, geoms=geoms)
    return pl.pallas_call(
        kern,
        grid=(N_HEADS, NA_BIAS_KINDS),
        in_specs=[pl.BlockSpec((1,) + pairs.shape[1:], lambda h, kind: (h, 0, 0, 0, 0))],
        out_specs=pl.BlockSpec((1, 1, 1, NA_Q_ROWS * GRID_W, NA_K_ROWS * GRID_W),
                               lambda h, kind: (kind, h // NA_HEADS_PER_STEP, h % NA_HEADS_PER_STEP, 0, 0)),
        out_shape=jax.ShapeDtypeStruct((NA_BIAS_KINDS, N_HEADS // NA_HEADS_PER_STEP, NA_HEADS_PER_STEP,
                                        NA_Q_ROWS * GRID_W, NA_K_ROWS * GRID_W), F32),
        compiler_params=_params(("parallel", "arbitrary"), 32),
        name="na_bias",
    )(pairs)


def _na_bias_kernel(p_ref, o_ref, *, geoms):
    for kind, (row_valid, d) in enumerate(geoms):
        @pl.when(pl.program_id(1) == kind)
        def _(row_valid=row_valid, d=d):
            for iq in range(NA_Q_ROWS):
                rows = slice(iq * GRID_W, (iq + 1) * GRID_W)
                for ik in range(0, NA_K_ROWS, 2):
                    cols = slice(ik * GRID_W, (ik + 2) * GRID_W)
                    left, right = row_valid[iq, ik], row_valid[iq, ik + 1]
                    if left and right:
                        o_ref[0, 0, 0, rows, cols] = p_ref[0, 0, int(d[iq, ik])]
                    elif left:
                        o_ref[0, 0, 0, rows, cols] = p_ref[0, 1, int(d[iq, ik])]
                    elif right:
                        o_ref[0, 0, 0, rows, cols] = p_ref[0, 2, int(d[iq, ik + 1])]
                    else:
                        o_ref[0, 0, 0, rows, cols] = jnp.full((GRID_W, 2 * GRID_W), NEG_BIG, F32)


def _na_attn(p, cache_k, cache_v, rpb):
    tq = NA_Q_ROWS * GRID_W
    q_col0 = 3 * HY_DIM // LANES
    k_col0 = q_col0 + NA_DIM // LANES
    v_col0 = k_col0 + NA_DIM // LANES
    q_blk0 = T_PROMPT // tq
    seq0 = T_PROMPT // L_LATENT
    bias = _na_bias_table(rpb)
    ctx_spec = pl.BlockSpec((1, cache_k.shape[1], LANES), lambda hp, rb, b: (b, 0, hp))
    return pl.pallas_call(
        _na_kernel,
        grid=(N_HEADS // NA_HEADS_PER_STEP, NA_ROW_BLOCKS, N_LATENT_SEQ),
        in_specs=[pl.BlockSpec((tq, LANES), lambda hp, rb, b: (q_blk0 + b * NA_ROW_BLOCKS + rb, q_col0 + hp)),
                  pl.BlockSpec((L_LATENT, LANES), lambda hp, rb, b: (seq0 + b, k_col0 + hp)),
                  pl.BlockSpec((L_LATENT, LANES), lambda hp, rb, b: (seq0 + b, v_col0 + hp)),
                  ctx_spec, ctx_spec,
                  pl.BlockSpec((1, 1, NA_HEADS_PER_STEP, tq, NA_K_ROWS * GRID_W),
                               lambda hp, rb, b: (_na_bias_kind(rb), hp, 0, 0, 0))],
        out_specs=pl.BlockSpec((tq, LANES), lambda hp, rb, b: (b * NA_ROW_BLOCKS + rb, hp)),
        out_shape=jax.ShapeDtypeStruct((T_LATENT, NA_DIM), F32),
        compiler_params=_params(("parallel", "parallel", "arbitrary"), 48),
        name="na_attn",
    )(p, p, p, cache_k, cache_v, bias)


def _pool_kernel(prev_ref, cur_ref, next_ref, gw_ref, sc_ref, o_ref, ext_ref, *, tm):
    i = pl.program_id(0)
    row0 = i * tm
    seq_len = jnp.where(row0 < T_PROMPT, L_PROMPT, L_LATENT)
    pos0 = jnp.where(row0 < T_PROMPT, row0 % L_PROMPT, (row0 - T_PROMPT) % L_LATENT)
    first = pos0 == 0
    last = pos0 + tm == seq_len
    h = POOL_HALO
    ext_ref[0:h] = jnp.where(first, 0.0, prev_ref[...])
    ext_ref[h:h + tm] = cur_ref[...]
    ext_ref[h + tm:] = jnp.where(last, 0.0, next_ref[...])
    pos = pos0 + lax.broadcasted_iota(jnp.int32, (tm, 1), 0)
    for g, w in enumerate(POOL_WINDOWS):
        cols = slice(g * POOL_GROUP, (g + 1) * POOL_GROUP)
        acc = ext_ref[h - w // 2:h - w // 2 + tm, cols]
        for j in range(1, w):
            acc = acc + ext_ref[h - w // 2 + j:h - w // 2 + j + tm, cols]
        cnt = jnp.minimum(pos + w // 2, seq_len) - jnp.maximum(pos - w // 2, 0)
        d = acc / cnt.astype(F32) - cur_ref[:, cols]
        o_ref[:, cols] = _dot(d.astype(BF16), gw_ref[g]) * sc_ref[:, cols]


def _pool(u, group_w_bf16, scale):
    tm = 256
    hb = tm // POOL_HALO
    n_halo = T_ALL // POOL_HALO
    kern = functools.partial(_pool_kernel, tm=tm)
    return pl.pallas_call(
        kern,
        grid=(T_ALL // tm,),
        in_specs=[pl.BlockSpec((POOL_HALO, D), lambda i: (jnp.maximum(i * hb - 1, 0), 0)),
                  pl.BlockSpec((tm, D), lambda i: (i, 0)),
                  pl.BlockSpec((POOL_HALO, D), lambda i: (jnp.minimum((i + 1) * hb, n_halo - 1), 0)),
                  pl.BlockSpec((len(POOL_WINDOWS), POOL_GROUP, POOL_GROUP), lambda i: (0, 0, 0)),
                  pl.BlockSpec((1, D), lambda i: (0, 0))],
        out_specs=pl.BlockSpec((tm, D), lambda i: (i, 0)),
        out_shape=jax.ShapeDtypeStruct((T_ALL, D), F32),
        scratch_shapes=[pltpu.VMEM((tm + 2 * POOL_HALO, D), F32)],
        compiler_params=_params(("parallel",), 40),
        name="pool",
    )(u, u, u, group_w_bf16, scale.reshape(1, D))


def _post_mixer_kernel(*refs, n1, n2, n_x, tm):
    a1_refs, a2_refs, x_refs = refs[:n1], refs[n1:n1 + n2], refs[n1 + n2:n1 + n2 + n_x]
    (w_ref, gpost_ref, gate_ref, gpre_ref, sh_ref, sc_ref, rwh_ref, rwl_ref, rb_ref,
     xo_ref, h_ref, lg_ref) = refs[n1 + n2 + n_x:]
    a1 = _token_rows(a1_refs, tm).astype(BF16)
    a2 = _token_rows(a2_refs, tm).astype(BF16)
    k1 = a1.shape[1]
    m = _dot(a1, w_ref[0:k1]) + _dot(a2, w_ref[k1:])
    xn = _token_rows(x_refs, tm) + gate_ref[0] * (_rms(m) * gpost_ref[...])
    xo_ref[...] = xn
    h = _rms(xn) * gpre_ref[...] * (1.0 + sc_ref[0]) + sh_ref[0]
    hh, hl = _split_bf16(h)
    h_ref[...] = _pack_bf16_pairs(hh)
    lg_ref[...] = _dot(hh, rwh_ref[...]) + _dot(hl, rwh_ref[...]) + _dot(hh, rwl_ref[...]) + rb_ref[...]


def _post_mixer(a1_parts, a1_col, a2_parts, a2_col, w_bf16, x_parts, g_post, gate, g_pre, shift, scale,
                rw_hi, rw_lo, rb, name):
    tm = 256
    kh = w_bf16.shape[0] // 2
    row = lambda i: (i, 0)
    const = lambda i: (0, 0)
    mod_spec = pl.BlockSpec((1, 1, D), lambda i: (_mod_row(i * tm), 0, 0))
    vec = pl.BlockSpec((1, D), const)
    kern = functools.partial(_post_mixer_kernel, n1=len(a1_parts), n2=len(a2_parts), n_x=len(x_parts), tm=tm)
    return pl.pallas_call(
        kern,
        grid=(T_ALL // tm,),
        in_specs=_token_specs(a1_parts, tm, kh, a1_col) + _token_specs(a2_parts, tm, kh, a2_col)
                 + _token_specs(x_parts, tm, D, 0) + [
                  pl.BlockSpec((2 * kh, D), const),
                  vec, mod_spec, vec, mod_spec, mod_spec,
                  pl.BlockSpec((D, LANES), const), pl.BlockSpec((D, LANES), const),
                  pl.BlockSpec((1, LANES), const)],
        out_specs=[pl.BlockSpec((tm, D), row), pl.BlockSpec((tm, D // 2), row), pl.BlockSpec((tm, LANES), row)],
        out_shape=[jax.ShapeDtypeStruct((T_ALL, D), F32), jax.ShapeDtypeStruct((T_ALL, D // 2), jnp.uint32),
                   jax.ShapeDtypeStruct((T_ALL, LANES), F32)],
        compiler_params=_params(("parallel",), 48),
        name=name,
    )(*a1_parts, *a2_parts, *x_parts, w_bf16, g_post, gate, g_pre, shift, scale, rw_hi, rw_lo, rb)


def _route_kernel(lg_ref, tri_ref, eidx_ref, gate_ref, rank_ref, cnt_ref, carry_ref):
    @pl.when(pl.program_id(0) == 0)
    def _():
        carry_ref[...] = jnp.zeros_like(carry_ref)

    shape = lg_ref.shape
    lane = lax.broadcasted_iota(jnp.int32, shape, 1).astype(F32)
    lg = jnp.where(lane < N_EXPERTS, lg_ref[...], -jnp.inf)
    multi = jnp.zeros(shape, F32)
    vals, sels = [], []
    eidx = jnp.zeros(shape, F32)
    for k in range(TOP_K):
        m = jnp.max(lg, axis=-1, keepdims=True)
        idx = jnp.min(jnp.where(lg == m, lane, float(LANES)), axis=-1, keepdims=True)
        sel = lane == idx
        multi = jnp.where(sel, 1.0, multi)
        lg = jnp.where(sel, -jnp.inf, lg)
        eidx = jnp.where(lane == k, idx, eidx)
        vals.append(m)
        sels.append(sel)
    exps = [jnp.exp(v - vals[0]) for v in vals]
    den = exps[0] + exps[1] + exps[2] + exps[3]
    rank_all = _dot(tri_ref[...], multi.astype(BF16)) + carry_ref[0:1]
    gate = jnp.zeros(shape, F32)
    rank = jnp.zeros(shape, F32)
    for k in range(TOP_K):
        gate = jnp.where(lane == k, exps[k] / den, gate)
        rank = jnp.where(lane == k, jnp.sum(jnp.where(sels[k], rank_all, 0.0), axis=-1, keepdims=True), rank)
    eidx_ref[...] = eidx.astype(jnp.int32)
    gate_ref[...] = gate
    rank_ref[...] = rank.astype(jnp.int32)
    carry_ref[0:1] = carry_ref[0:1] + jnp.sum(multi, axis=0, keepdims=True)
    cnt_ref[...] = carry_ref[...]


def _route(logits):
    tm = 256
    tri = (np.arange(tm)[:, None] > np.arange(tm)[None, :]).astype(np.float32)
    row = lambda i: (i, 0)
    spec = pl.BlockSpec((tm, LANES), row)
    return pl.pallas_call(
        _route_kernel,
        grid=(T_ALL // tm,),
        in_specs=[spec, pl.BlockSpec((tm, tm), lambda i: (0, 0))],
        out_specs=[spec, spec, spec, pl.BlockSpec((8, LANES), lambda i: (0, 0))],
        out_shape=[jax.ShapeDtypeStruct((T_ALL, LANES), jnp.int32), jax.ShapeDtypeStruct((T_ALL, LANES), F32),
                   jax.ShapeDtypeStruct((T_ALL, LANES), jnp.int32), jax.ShapeDtypeStruct((8, LANES), F32)],
        scratch_shapes=[pltpu.VMEM((8, LANES), F32)],
        compiler_params=_params(("arbitrary",), 32),
        name="route",
    )(logits, jnp.asarray(tri, BF16))


def _dest_kernel(eidx_ref, rank_ref, start_ref, o_ref):
    shape = eidx_ref.shape
    lane = lax.broadcasted_iota(jnp.int32, shape, 1).astype(F32)
    eidx = eidx_ref[...].astype(F32)
    starts = start_ref[...].astype(F32)
    dest = jnp.zeros(shape, F32)
    for k in range(TOP_K):
        e_k = jnp.sum(jnp.where(lane == k, eidx, 0.0), axis=-1, keepdims=True)
        start = jnp.sum(jnp.where(lane == e_k, starts, 0.0), axis=-1, keepdims=True)
        dest = jnp.where(lane == k, start, dest)
    o_ref[...] = jnp.where(lane < TOP_K, dest.astype(jnp.int32) + rank_ref[...], 0)


def _dest(eidx, rank, group_start):
    tm = 1024
    spec = pl.BlockSpec((tm, LANES), lambda i: (i, 0))
    return pl.pallas_call(
        _dest_kernel,
        grid=(T_ALL // tm,),
        in_specs=[spec, spec, pl.BlockSpec((1, LANES), lambda i: (0, 0))],
        out_specs=spec,
        out_shape=jax.ShapeDtypeStruct((T_ALL, LANES), jnp.int32),
        compiler_params=_params(("parallel",), 32),
        name="dest",
    )(eidx, rank, group_start)


def _dispatch_kernel(dest_ref, lo_ref, hi_ref, nu_ref, h_ref, o_ref, zero_ref, sem, zsem, *, tm):
    i = pl.program_id(0)
    base = i * tm * TOP_K

    @pl.when(i == 0)
    def _():
        zero_ref[...] = jnp.zeros_like(zero_ref)

    def issue(g, carry):
        t0 = pl.multiple_of(g * SUBLANES, SUBLANES)
        for s in range(SUBLANES):
            for k in range(TOP_K):
                d = dest_ref[base + (t0 + s) * TOP_K + k]
                pltpu.make_async_copy(h_ref.at[pl.ds(t0 + s, 1)], o_ref.at[pl.ds(d, 1)], sem).start(priority=k % 2)
        return carry

    def zero_copies(do):
        @pl.when(i < N_EXPERTS)
        def _():
            lo = lo_ref[i]
            n_pad = hi_ref[i] - lo
            head = jnp.minimum((SUBLANES - lo % SUBLANES) % SUBLANES, n_pad)

            def zero_row(r, carry):
                do(pltpu.make_async_copy(zero_ref.at[pl.ds(0, 1)], o_ref.at[pl.ds(lo + r, 1)], zsem))
                return carry

            lax.fori_loop(0, head, zero_row, 0)
            pos = lo + head
            rest = n_pad - head
            size = MOE_TM // 2
            while size >= SUBLANES:
                @pl.when((rest & size) != 0)
                def _(pos=pos, size=size):
                    do(pltpu.make_async_copy(zero_ref.at[pl.ds(0, size)],
                                             o_ref.at[pl.ds(pl.multiple_of(pos, SUBLANES), size)], zsem))

                pos = pos + (rest & size)
                size //= 2

        blk = nu_ref[0] + i - N_EXPERTS

        @pl.when((i >= N_EXPERTS) & (blk < MOE_BLOCKS))
        def _():
            r0 = pl.multiple_of(blk * MOE_TM, MOE_TM)
            do(pltpu.make_async_copy(zero_ref, o_ref.at[pl.ds(r0, MOE_TM)], zsem))

    zero_copies(lambda cp: cp.start())
    lax.fori_loop(0, tm // SUBLANES, issue, 0)
    zero_copies(lambda cp: cp.wait())
    for k in range(TOP_K):
        pltpu.make_async_copy(h_ref, o_ref.at[pl.ds(0, tm)], sem).wait()


def _dispatch(dest_flat, pad_lo, pad_hi, n_used, h):
    tm = 128
    assert T_ALL // tm >= 2 * N_EXPERTS
    kern = functools.partial(_dispatch_kernel, tm=tm)
    return pl.pallas_call(
        kern,
        grid_spec=pltpu.PrefetchScalarGridSpec(
            num_scalar_prefetch=4,
            grid=(T_ALL // tm,),
            in_specs=[pl.BlockSpec((tm, D // 2), lambda i, *_: (i, 0))],
            out_specs=pl.BlockSpec(memory_space=pl.ANY),
            scratch_shapes=[pltpu.VMEM((MOE_TM, D // 2), jnp.uint32), pltpu.SemaphoreType.DMA,
                            pltpu.SemaphoreType.DMA]),
        out_shape=jax.ShapeDtypeStruct((MOE_ROWS, D // 2), jnp.uint32),
        compiler_params=_params(("arbitrary",), 32),
        name="dispatch",
    )(dest_flat, pad_lo, pad_hi, n_used, h)


def _new_weights(be_ref, i):
    return (i == 0) | (be_ref[i] != be_ref[jnp.maximum(i - 1, 0)])


def _for_valid_rows(n_valid, compute):
    many = n_valid > MOE_TM // 2
    pl.when(many)(lambda: compute(MOE_TM))
    pl.when(jnp.logical_not(many))(lambda: compute(MOE_TM // 2))


def _expert_up_kernel(be_ref, nu_ref, bv_ref, x_ref, w_ref, b_ref, o_ref, wbf_ref):
    i = pl.program_id(1)
    tn = o_ref.shape[1]

    def compute(rows):
        x = jnp.concatenate(_unpack_bf16_pairs(x_ref[0:rows]), axis=1)
        even = lax.broadcasted_iota(jnp.int32, (rows, LANES), 1) % 2 == 0
        for c in range(tn // LANES):
            cols = slice(2 * LANES * c, 2 * LANES * (c + 1))
            hb = _dot(x, wbf_ref[:, cols]) + b_ref[0, 0, :, cols]
            first, second = hb[:, :LANES], hb[:, LANES:]
            g = jnp.where(even, first, pltpu.roll(second, 1, 1))
            lin = jnp.where(even, pltpu.roll(first, LANES - 1, 1), second)
            g = jnp.minimum(g, SWIGLU_LIMIT)
            lin = jnp.clip(lin, -SWIGLU_LIMIT, SWIGLU_LIMIT)
            act = g * jax.nn.sigmoid(SWIGLU_ALPHA * g) * (lin + 1.0)
            o_ref[0:rows, LANES * c:LANES * (c + 1)] = act.astype(BF16)
        if rows < MOE_TM:
            o_ref[rows:] = jnp.zeros((MOE_TM - rows, tn), BF16)

    @pl.when(i < nu_ref[0])
    def _():
        @pl.when(_new_weights(be_ref, i))
        def _():
            wbf_ref[...] = w_ref[0, 0].astype(BF16)

        _for_valid_rows(bv_ref[i], compute)

    @pl.when(i >= nu_ref[0])
    def _():
        o_ref[...] = jnp.zeros_like(o_ref)


def _expert_up(blk_e, n_used, blk_valid, xs, w_gu, b_gu, layer):
    tn = FF_TILE
    blk = lambda j, i, be, nu, bv: (jnp.minimum(i, nu[0] - 1), 0)
    return pl.pallas_call(
        _expert_up_kernel,
        grid_spec=pltpu.PrefetchScalarGridSpec(
            num_scalar_prefetch=3,
            grid=(D_FF // tn, MOE_BLOCKS),
            in_specs=[pl.BlockSpec((MOE_TM, D // 2), blk),
                      pl.BlockSpec((1, 1, D, 2 * tn), lambda j, i, be, nu, bv: (layer, be[i], 0, j)),
                      pl.BlockSpec((1, 1, 1, 2 * tn), lambda j, i, be, nu, bv: (layer, be[i], 0, j))],
            out_specs=pl.BlockSpec((MOE_TM, tn), lambda j, i, be, nu, bv: (i, j)),
            scratch_shapes=[pltpu.VMEM((D, 2 * tn), BF16)]),
        out_shape=jax.ShapeDtypeStruct((MOE_ROWS, D_FF), BF16),
        compiler_params=_params(("arbitrary", "arbitrary"), 56),
        name="expert_up",
    )(blk_e, n_used, blk_valid, xs, w_gu, b_gu.reshape(N_LAYERS, N_EXPERTS, 1, 2 * D_FF))


DOWN_TILE = D


def _expert_down_kernel(be_ref, nu_ref, bv_ref, a_ref, w_ref, b_ref, o_ref, wbf_ref, wtmp_ref):
    i = pl.program_id(1)

    def compute(rows):
        y = _dot(a_ref[0:rows], wbf_ref[...]) + b_ref[0, 0]
        o_ref[0:rows] = _pack_bf16_pairs(y.astype(BF16))
        if rows < MOE_TM:
            o_ref[rows:] = jnp.zeros((MOE_TM - rows, o_ref.shape[1]), jnp.uint32)

    @pl.when(i < nu_ref[0])
    def _():
        @pl.when(_new_weights(be_ref, i))
        def _():
            half = LANES // 2
            for g in range(D_FF // LANES):
                r0 = LANES * g
                for c in range(wbf_ref.shape[1] // LANES):
                    cols = slice(LANES * c, LANES * (c + 1))
                    wtmp_ref[c, pl.ds(0, half, stride=2), :] = w_ref[0, 0, r0:r0 + half, cols]
                    wtmp_ref[c, pl.ds(1, half, stride=2), :] = w_ref[0, 0, r0 + half:r0 + LANES, cols]
                    wbf_ref[r0:r0 + LANES, cols] = wtmp_ref[c].astype(BF16)

        _for_valid_rows(bv_ref[i], compute)

    @pl.when(i >= nu_ref[0])
    def _():
        o_ref[...] = jnp.zeros_like(o_ref)


def _expert_down(blk_e, n_used, blk_valid, a, w_down, b_down, layer):
    tn = DOWN_TILE
    blk = lambda j, i, be, nu, bv: (jnp.minimum(i, nu[0] - 1), 0)
    return pl.pallas_call(
        _expert_down_kernel,
        grid_spec=pltpu.PrefetchScalarGridSpec(
            num_scalar_prefetch=3,
            grid=(D // tn, MOE_BLOCKS),
            in_specs=[pl.BlockSpec((MOE_TM, D_FF), blk),
                      pl.BlockSpec((1, 1, D_FF, tn), lambda j, i, be, nu, bv: (layer, be[i], 0, j)),
                      pl.BlockSpec((1, 1, 1, tn), lambda j, i, be, nu, bv: (layer, be[i], 0, j))],
            out_specs=pl.BlockSpec((MOE_TM, tn // 2), lambda j, i, be, nu, bv: (i, j)),
            scratch_shapes=[pltpu.VMEM((D_FF, tn), BF16), pltpu.VMEM((tn // LANES, LANES, LANES), F32)]),
        out_shape=jax.ShapeDtypeStruct((MOE_ROWS, D // 2), jnp.uint32),
        compiler_params=_params(("arbitrary", "arbitrary"), 60),
        name="expert_down",
    )(blk_e, n_used, blk_valid, a, w_down, b_down.reshape(N_LAYERS, N_EXPERTS, 1, D))


def _combine_kernel(dest_ref, y_ref, gate_ref, x_ref, gpost_ref, g2_ref, *rest, tm):
    (*o_refs, buf_ref, sem) = rest
    i = pl.program_id(0)

    def gather(step, slot):
        base = step * tm * TOP_K

        def issue(g, carry):
            t0 = pl.multiple_of(g * SUBLANES, SUBLANES)
            for s in range(SUBLANES):
                for k in range(TOP_K):
                    d = dest_ref[base + (t0 + s) * TOP_K + k]
                    pltpu.make_async_copy(y_ref.at[pl.ds(d, 1)], buf_ref.at[slot, k, pl.ds(t0 + s, 1)],
                                          sem.at[slot]).start(priority=k % 2)
            return carry

        lax.fori_loop(0, tm // SUBLANES, issue, 0)

    @pl.when(i == 0)
    def _():
        gather(0, 0)

    @pl.when(i + 1 < pl.num_programs(0))
    def _():
        gather(i + 1, (i + 1) % 2)

    slot = i % 2
    for k in range(TOP_K):
        pltpu.make_async_copy(y_ref.at[pl.ds(0, tm)], buf_ref.at[slot, k], sem.at[slot]).wait()

    def expert_rows(k):
        hi, lo = _unpack_bf16_pairs(buf_ref[slot, k])
        tw = DOWN_TILE // 2
        parts = []
        for j in range(D // DOWN_TILE):
            parts += [hi[:, j * tw:(j + 1) * tw], lo[:, j * tw:(j + 1) * tw]]
        return jnp.concatenate(parts, axis=1).astype(F32)

    f = gate_ref[:, 0:1] * expert_rows(0)
    for k in range(1, TOP_K):
        f = f + gate_ref[:, k:k + 1] * expert_rows(k)
    out = x_ref[...] + g2_ref[0] * (_rms(f) * gpost_ref[...])
    if len(o_refs) == 1:
        o_refs[0][...] = out
    else:
        in_prompt = i * tm < T_PROMPT
        @pl.when(in_prompt)
        def _():
            o_refs[0][...] = out

        @pl.when(jnp.logical_not(in_prompt))
        def _():
            o_refs[1][...] = out


def _combine(dest_flat, ys, gate, x, g_post, gate2, name, split_out=False):
    tm = 128
    kern = functools.partial(_combine_kernel, tm=tm)
    n_prompt_tiles = T_PROMPT // tm
    if split_out:
        out_specs = [pl.BlockSpec((tm, D), lambda i, d: (jnp.minimum(i, n_prompt_tiles - 1), 0)),
                     pl.BlockSpec((tm, D), lambda i, d: (jnp.maximum(i - n_prompt_tiles, 0), 0))]
        out_shape = [jax.ShapeDtypeStruct((T_PROMPT, D), F32), jax.ShapeDtypeStruct((T_LATENT, D), F32)]
    else:
        out_specs = pl.BlockSpec((tm, D), lambda i, d: (i, 0))
        out_shape = jax.ShapeDtypeStruct((T_ALL, D), F32)
    return pl.pallas_call(
        kern,
        grid_spec=pltpu.PrefetchScalarGridSpec(
            num_scalar_prefetch=1,
            grid=(T_ALL // tm,),
            in_specs=[pl.BlockSpec(memory_space=pl.ANY),
                      pl.BlockSpec((tm, LANES), lambda i, d: (i, 0)),
                      pl.BlockSpec((tm, D), lambda i, d: (i, 0)),
                      pl.BlockSpec((1, D), lambda i, d: (0, 0)),
                      pl.BlockSpec((1, 1, D), lambda i, d: (_mod_row(i * tm), 0, 0))],
            out_specs=out_specs,
            scratch_shapes=[pltpu.VMEM((2, TOP_K, tm, D // 2), jnp.uint32), pltpu.SemaphoreType.DMA((2,))]),
        out_shape=out_shape,
        compiler_params=_params(("arbitrary",), 32),
        name=name,
    )(dest_flat, ys, gate, x, g_post, gate2)


def _moe(h, logits, x, g_post, gate2, w_gu, b_gu, w_down, b_down, layer):
    eidx, gate, rank, counts = _route(logits)
    cnt = counts[0, :N_EXPERTS].astype(jnp.int32)
    padded = (cnt + MOE_TM - 1) // MOE_TM * MOE_TM
    group_end = jnp.cumsum(padded)
    group_start = jnp.zeros((1, LANES), jnp.int32).at[0, :N_EXPERTS].set(group_end - padded)
    n_used = group_end[-1] // MOE_TM
    blk_row0 = jnp.arange(MOE_BLOCKS, dtype=jnp.int32) * MOE_TM
    blk_row0 = jnp.minimum(blk_row0, (n_used - 1) * MOE_TM)
    blk_e = jnp.sum(group_end[None, :] <= blk_row0[:, None], axis=1).astype(jnp.int32)
    n_used = n_used.astype(jnp.int32).reshape(1)
    filled_end = group_end - padded + cnt
    blk_valid = jnp.clip(filled_end[jnp.minimum(blk_e, N_EXPERTS - 1)] - blk_row0, 0, MOE_TM).astype(jnp.int32)
    dest = _dest(eidx, rank, group_start)[:, :TOP_K].reshape(N_ASSIGN)
    xs = _dispatch(dest, filled_end, group_end, n_used, h)
    a = _expert_up(blk_e, n_used, blk_valid, xs, w_gu, b_gu, layer)
    ys = _expert_down(blk_e, n_used, blk_valid, a, w_down, b_down, layer)
    return _combine(dest, ys, gate, x, g_post, gate2, f"combine_{layer}", split_out=layer == N_LAYERS - 1)


def kernel(x_prompt, x_sample, cache_k, cache_v, c, c_ctx, ada_w, ada_b, norm_mix_pre, norm_mix_post,
           norm_ffn_pre, norm_ffn_post, ab_in_w, ab_out_w, hy_short_w, hy_short_b, hy_w1, hy_b1, hy_w2,
           hy_b2, hy_w3, hy_freq, hy_bias, na_rpb, pool_in_w, pool_group_w, pool_scale, pool_out_w,
           router_w, router_b, exp_w_gu, exp_b_gu, exp_w_down, exp_b_down):
    x = [x_prompt.reshape(T_PROMPT, D), x_sample.reshape(T_LATENT, D)]
    cvec = jnp.concatenate([c_ctx[None], c, jnp.zeros((8 - 1 - N_LATENT_SEQ, D), F32)], axis=0)
    mods = _adaln(cvec, ada_w, ada_b).reshape(N_LAYERS, 8, 6, D)

    new_k = new_v = None
    for layer in range(N_LAYERS):
        sh1, sc1, g1, sh2, sc2, g2 = [mods[layer, :, k][:, None, :] for k in range(6)]
        row = lambda a: a[layer].reshape(1, D)
        if layer % 2 == 0:
            j = layer // 2
            p = _modmm(x, row(norm_mix_pre), sh1, sc1, ab_in_w[j].astype(BF16), "ab_in")
            filt = (hy_w1[j], hy_b1[j], hy_w2[j], hy_b2[j], hy_w3[j], hy_freq[j])
            hy = (hy_short_w[j], hy_short_b[j], hy_bias[j])
            yh_p = _hyena(p, *hy, _hyena_filter_time(L_PROMPT, *filt), N_PROMPT_SEQ, L_PROMPT, 0)
            yh_s = _hyena(p, *hy, _hyena_filter_time(L_LATENT, *filt), N_LATENT_SEQ, L_LATENT,
                          T_PROMPT // L_LATENT)
            ya_p, new_k, new_v = _ctx_attn(p)
            ck = cache_k[:, j].reshape(N_LATENT_SEQ, -1, NA_DIM)
            cv = cache_v[:, j].reshape(N_LATENT_SEQ, -1, NA_DIM)
            ya_s = _na_attn(p, ck, cv, na_rpb[j])
            a1 = [yh_p, yh_s]
            a2 = [ya_p, ya_s]
            a1_col = a2_col = 0
            w_out = ab_out_w[j].astype(BF16)
        else:
            j = layer // 2
            u = _modmm(x, row(norm_mix_pre), sh1, sc1, pool_in_w[j].astype(BF16), "pool_in")
            a1 = a2 = [_pool(u, pool_group_w[j].astype(BF16), pool_scale[j])]
            a1_col, a2_col = 0, 1
            w_out = pool_out_w[j].astype(BF16)
        rw = jnp.zeros((D, LANES), F32).at[:, :N_EXPERTS].set(router_w[layer])
        rw_hi, rw_lo = _split_bf16(rw)
        rb = jnp.zeros((1, LANES), F32).at[0, :N_EXPERTS].set(router_b[layer])
        xm, h, logits = _post_mixer(a1, a1_col, a2, a2_col, w_out, x, row(norm_mix_post), g1,
                                    row(norm_ffn_pre), sh2, sc2, rw_hi, rw_lo, rb, f"post_mixer_{layer}")
        x = _moe(h, logits, xm, row(norm_ffn_post), g2, exp_w_gu, exp_b_gu, exp_w_down, exp_b_down, layer)
        x = list(x) if isinstance(x, (list, tuple)) else [x]

    y_prompt = x[0].reshape(x_prompt.shape)
    y_sample = x[1].reshape(x_sample.shape)
    kv_shape = (N_PROMPT_SEQ, 1, L_PROMPT, N_HEADS, HEAD_DIM)
    return y_prompt, y_sample, new_k.reshape(kv_shape), new_v.reshape(kv_shape)
```

```python
import functools
import math

import jax
import jax.numpy as jnp
import numpy as np
from jax import lax
from jax.experimental import pallas as pl
from jax.experimental.pallas import tpu as pltpu

F32 = jnp.float32
BF16 = jnp.bfloat16

D = 2048
N_PROMPT_SEQ = 32
L_PROMPT = 256
N_LATENT_SEQ = 4
L_LATENT = 2048
T_PROMPT = N_PROMPT_SEQ * L_PROMPT
T_LATENT = N_LATENT_SEQ * L_LATENT
T_ALL = T_PROMPT + T_LATENT
N_LAYERS = 2
GRID_W = 64
GRID_ROWS = L_LATENT // GRID_W
HY_DIM = 1024
HY_EMB = 33
HY_DECAY_TARGET = 1e-2
HY_FAST_DECAY = 0.3
HY_SLOW_DECAY = 1.5
N_HEADS = 16
HEAD_DIM = 64
NA_DIM = N_HEADS * HEAD_DIM
WIN_R = 8
WIN_C = 16
AB_IN = 3 * HY_DIM + 3 * NA_DIM
POOL_WINDOWS = (2, 4, 8, 16)
POOL_GROUP = D // len(POOL_WINDOWS)
POOL_HALO = 8
N_EXPERTS = 32
TOP_K = 4
D_FF = D
SWIGLU_LIMIT = 7.0
SWIGLU_ALPHA = 1.702
RMS_EPS = 1e-6
NEG_BIG = -1e30

LANES = 128
SUBLANES = 8
MOE_TM = 512
FF_TILE = 1024
N_ASSIGN = T_ALL * TOP_K
MOE_BLOCKS = -(-(N_ASSIGN + N_EXPERTS * (MOE_TM - 1)) // MOE_TM)
MOE_ROWS = MOE_BLOCKS * MOE_TM
MIB = 1 << 20


def _params(semantics, vmem_mib):
    return pltpu.CompilerParams(dimension_semantics=semantics, vmem_limit_bytes=vmem_mib * MIB)


def _mod_row(row0):
    return jnp.where(row0 < T_PROMPT, 0, 1 + (row0 - T_PROMPT) // L_LATENT)


def _rms(x):
    return x * lax.rsqrt(jnp.mean(x * x, axis=-1, keepdims=True) + RMS_EPS)


def _split_bf16(x):
    hi = x.astype(BF16)
    lo = (x - hi.astype(F32)).astype(BF16)
    return hi, lo


def _dot(a, b):
    return jnp.dot(a, b, preferred_element_type=F32)


def _pack_bf16_pairs(x_bf16):
    half = x_bf16.shape[1] // 2
    bits = lax.bitcast_convert_type(x_bf16.astype(F32), jnp.uint32)
    return (bits[:, :half] & jnp.uint32(0xFFFF0000)) | (bits[:, half:] >> 16)


def _unpack_bf16_pairs(packed):
    hi = lax.bitcast_convert_type(packed & jnp.uint32(0xFFFF0000), F32).astype(BF16)
    lo = lax.bitcast_convert_type(packed << 16, F32).astype(BF16)
    return hi, lo


def _adaln_kernel(cv_ref, w_ref, b_ref, o_ref):
    s = jax.nn.silu(cv_ref[...]).astype(BF16)
    o_ref[0] = _dot(s, w_ref[0].astype(BF16)) + b_ref[0]


def _adaln(cvec, ada_w, ada_b):
    n = ada_w.shape[-1]
    tn = 1024
    return pl.pallas_call(
        _adaln_kernel,
        grid=(N_LAYERS, n // tn),
        in_specs=[pl.BlockSpec((8, D), lambda l, j: (0, 0)),
                  pl.BlockSpec((1, D, tn), lambda l, j: (l, 0, j)),
                  pl.BlockSpec((1, 1, tn), lambda l, j: (l, 0, j))],
        out_specs=pl.BlockSpec((1, 8, tn), lambda l, j: (l, 0, j)),
        out_shape=jax.ShapeDtypeStruct((N_LAYERS, 8, n), F32),
        compiler_params=_params(("parallel", "arbitrary"), 40),
        name="adaln",
    )(cvec, ada_w, ada_b.reshape(N_LAYERS, 1, n))


def _token_specs(parts, tm, width, col):
    if len(parts) == 1:
        return [pl.BlockSpec((tm, width), lambda i, *_: (i, col))]
    n_prompt_tiles = T_PROMPT // tm
    return [pl.BlockSpec((tm, width), lambda i, *_: (jnp.minimum(i, n_prompt_tiles - 1), col)),
            pl.BlockSpec((tm, width), lambda i, *_: (jnp.maximum(i - n_prompt_tiles, 0), col))]


def _token_rows(part_refs, tm):
    if len(part_refs) == 1:
        return part_refs[0][...]
    return jnp.where(pl.program_id(0) * tm < T_PROMPT, part_refs[0][...], part_refs[1][...])


def _modmm_kernel(*refs, n_x, tm):
    x_refs = refs[:n_x]
    g_ref, sh_ref, sc_ref, w_ref, o_ref, h_ref = refs[n_x:]

    @pl.when(pl.program_id(1) == 0)
    def _():
        h = _rms(_token_rows(x_refs, tm)) * g_ref[...] * (1.0 + sc_ref[0]) + sh_ref[0]
        h_ref[...] = h.astype(BF16)

    o_ref[...] = _dot(h_ref[...], w_ref[...])


def _modmm(x_parts, g, shift, scale, w_bf16, name):
    n = w_bf16.shape[1]
    tm, tn = (1024, 1024) if len(x_parts) == 1 else (512, 2048)
    mod_spec = pl.BlockSpec((1, 1, D), lambda i, j: (_mod_row(i * tm), 0, 0))
    return pl.pallas_call(
        functools.partial(_modmm_kernel, n_x=len(x_parts), tm=tm),
        grid=(T_ALL // tm, n // tn),
        in_specs=_token_specs(x_parts, tm, D, 0) + [
                  pl.BlockSpec((1, D), lambda i, j: (0, 0)),
                  mod_spec, mod_spec,
                  pl.BlockSpec((D, tn), lambda i, j: (0, j))],
        out_specs=pl.BlockSpec((tm, tn), lambda i, j: (i, j)),
        out_shape=jax.ShapeDtypeStruct((T_ALL, n), F32),
        scratch_shapes=[pltpu.VMEM((tm, D), BF16)],
        compiler_params=_params(("parallel", "arbitrary"), 48),
        name=name,
    )(*x_parts, g, shift, scale, w_bf16)


def _shortconv_kernel(z_ref, w_ref, b_ref, o_ref):
    z = z_ref[...]
    n = z.shape[0]
    row = lax.broadcasted_iota(jnp.int32, z.shape, 0)
    prev = jnp.where(row == 0, 0.0, pltpu.roll(z, 1, 0))
    nxt = jnp.where(row == n - 1, 0.0, pltpu.roll(z, n - 1, 0))
    o_ref[...] = w_ref[0:1] * prev + w_ref[1:2] * z + w_ref[2:3] * nxt + b_ref[...]


def _shortconv(p, w, b, n_seq, seq_len, row_block0):
    c = 3 * HY_DIM
    tc = 512 if seq_len > 512 else c
    return pl.pallas_call(
        _shortconv_kernel,
        grid=(n_seq, c // tc),
        in_specs=[pl.BlockSpec((seq_len, tc), lambda s, j: (row_block0 + s, j)),
                  pl.BlockSpec((3, tc), lambda s, j: (0, j)),
                  pl.BlockSpec((1, tc), lambda s, j: (0, j))],
        out_specs=pl.BlockSpec((seq_len, tc), lambda s, j: (s, j)),
        out_shape=jax.ShapeDtypeStruct((n_seq * seq_len, c), F32),
        compiler_params=_params(("parallel", "parallel"), 40),
        name=f"shortconv_{seq_len}",
    )(p, w, b.reshape(1, c))


def _seqmm_kernel(a_ref, x_ref, o_ref):
    o_ref[...] = _dot(a_ref[...], x_ref[...].astype(BF16))


def _seqmm(a_bf16, x, n_seq, col0, n_cols, name):
    m, k = a_bf16.shape
    tm = min(m, 1024)
    tn = 512 if k > 512 else 1024
    col_block0 = col0 // tn
    return pl.pallas_call(
        _seqmm_kernel,
        grid=(n_seq, n_cols // tn, m // tm),
        in_specs=[pl.BlockSpec((tm, k), lambda s, j, i: (i, 0)),
                  pl.BlockSpec((k, tn), lambda s, j, i: (s, col_block0 + j))],
        out_specs=pl.BlockSpec((tm, tn), lambda s, j, i: (s * (m // tm) + i, j)),
        out_shape=jax.ShapeDtypeStruct((n_seq * m, n_cols), F32),
        compiler_params=_params(("parallel", "parallel", "arbitrary"), 40),
        name=name,
    )(a_bf16, x)


def _hyena_inv_kernel(g_ref, y_ref, k_ref, yin_ref, gate_ref, bias_ref, o_ref, z_ref, *, seq_len):
    @pl.when(pl.program_id(2) == 0)
    def _():
        yc, ys = y_ref[0:seq_len], y_ref[seq_len:]
        kc, ks = k_ref[0:seq_len], k_ref[seq_len:]
        z_ref[0:seq_len] = (yc * kc - ys * ks).astype(BF16)
        z_ref[seq_len:] = (yc * ks + ys * kc).astype(BF16)

    conv = _dot(g_ref[...], z_ref[...]) * (1.0 / seq_len)
    o_ref[...] = gate_ref[...] * (conv + yin_ref[...] * bias_ref[...])


def _hyena_inv(g_bf16, yspec, kspec, order, yin, yin_col0, u, gate_col0, bias, n_seq, seq_len):
    tn = 256 if seq_len > 512 else 1024
    tt = min(seq_len, 512)
    nb = HY_DIM // tn
    kern = functools.partial(_hyena_inv_kernel, seq_len=seq_len)
    return pl.pallas_call(
        kern,
        grid=(n_seq, nb, seq_len // tt),
        in_specs=[pl.BlockSpec((tt, 2 * seq_len), lambda s, j, t: (t, 0)),
                  pl.BlockSpec((2 * seq_len, tn), lambda s, j, t: (s, j)),
                  pl.BlockSpec((2 * seq_len, tn), lambda s, j, t: (0, order * nb + j)),
                  pl.BlockSpec((tt, tn), lambda s, j, t: (s * (seq_len // tt) + t, yin_col0 // tn + j)),
                  pl.BlockSpec((tt, tn), lambda s, j, t: (s * (seq_len // tt) + t, gate_col0 // tn + j)),
                  pl.BlockSpec((1, tn), lambda s, j, t: (0, order * nb + j))],
        out_specs=pl.BlockSpec((tt, tn), lambda s, j, t: (s * (seq_len // tt) + t, j)),
        out_shape=jax.ShapeDtypeStruct((n_seq * seq_len, HY_DIM), F32),
        scratch_shapes=[pltpu.VMEM((2 * seq_len, tn), BF16)],
        compiler_params=_params(("parallel", "parallel", "arbitrary"), 48),
        name=f"hyena_inv_{seq_len}_{order}",
    )(g_bf16, yspec, kspec, yin, u, bias.reshape(1, 2 * HY_DIM))


def _dft_matrices(seq_len):
    f = lax.broadcasted_iota(jnp.int32, (seq_len, seq_len), 0)
    s = lax.broadcasted_iota(jnp.int32, (seq_len, seq_len), 1)
    ang = (((2 * f + 1) * s) % (4 * seq_len)).astype(F32) * (math.pi / (2 * seq_len))
    c, sn = jnp.cos(ang), jnp.sin(ang)
    fwd = jnp.concatenate([c, sn], axis=0).astype(BF16)
    inv = jnp.concatenate([c.T, sn.T], axis=1).astype(BF16)
    return fwd, inv


def _hyena_filter_time(seq_len, w1, b1, w2, b2, w3, freq):
    hp = lax.Precision.HIGHEST
    t = jnp.linspace(0.0, 1.0, seq_len, dtype=F32)[:, None]
    bands = (HY_EMB - 1) // 2
    ang = 2.0 * math.pi * jnp.arange(seq_len, dtype=F32)[:, None] / seq_len
    fb = jnp.linspace(1e-4, bands - 1, bands, dtype=F32)[None, :]
    z = jnp.concatenate([t, jnp.cos(fb * ang), -jnp.sin(fb * ang)], axis=-1)
    h = jnp.sin(freq * (jnp.dot(z, w1, precision=hp) + b1))
    h = jnp.sin(freq * (jnp.dot(h, w2, precision=hp) + b2))
    h = jnp.dot(h, w3, precision=hp).reshape(seq_len, 2, 2, HY_DIM)
    deltas = jnp.linspace(math.log(HY_DECAY_TARGET) / HY_SLOW_DECAY,
                          math.log(HY_DECAY_TARGET) / HY_FAST_DECAY, HY_DIM, dtype=F32)
    h = h * jnp.exp(-t * jnp.abs(deltas))[:, None, None, :]
    return h * lax.rsqrt(jnp.sum(h * h, axis=(0, 1), keepdims=True))


def _filter_spec_kernel(a_ref, hf_ref, hb_ref, o_ref, *, seq_len, tm):
    sign = jnp.where(pl.program_id(1) * tm < seq_len, 1.0, -1.0)
    hb = hb_ref[...]
    hb = jnp.where(lax.broadcasted_iota(jnp.int32, hb.shape, 0) == 0, 0.0, hb)
    hi, lo = _split_bf16(hf_ref[...] + sign * hb)
    o_ref[...] = _dot(a_ref[...], hi) + _dot(a_ref[...], lo)


def _hyena_filter_spec(h, fwd_bf16, seq_len):
    h2d = h.reshape(seq_len, 4 * HY_DIM)
    tm = min(seq_len, 1024)
    tn = 512
    nb = 2 * HY_DIM // tn
    kern = functools.partial(_filter_spec_kernel, seq_len=seq_len, tm=tm)
    return pl.pallas_call(
        kern,
        grid=(nb, 2 * seq_len // tm),
        in_specs=[pl.BlockSpec((tm, seq_len), lambda j, i: (i, 0)),
                  pl.BlockSpec((seq_len, tn), lambda j, i: (0, j)),
                  pl.BlockSpec((seq_len, tn), lambda j, i: (0, nb + j))],
        out_specs=pl.BlockSpec((tm, tn), lambda j, i: (i, j)),
        out_shape=jax.ShapeDtypeStruct((2 * seq_len, 2 * HY_DIM), F32),
        compiler_params=_params(("parallel", "arbitrary"), 40),
        name=f"filter_spec_{seq_len}",
    )(fwd_bf16, h2d, h2d)


def _hyena(p, short_w, short_b, bias, h_time, n_seq, seq_len, row_block0):
    fwd, inv = _dft_matrices(seq_len)
    kspec = _hyena_filter_spec(h_time, fwd, seq_len)
    u = _shortconv(p, short_w, short_b, n_seq, seq_len, row_block0)
    yspec = _seqmm(fwd, u, n_seq, 2 * HY_DIM, HY_DIM, f"hyena_fwd_{seq_len}_0")
    y1 = _hyena_inv(inv, yspec, kspec, 0, u, 2 * HY_DIM, u, 0, bias, n_seq, seq_len)
    yspec = _seqmm(fwd, y1, n_seq, 0, HY_DIM, f"hyena_fwd_{seq_len}_1")
    return _hyena_inv(inv, yspec, kspec, 1, y1, 0, u, HY_DIM, bias, n_seq, seq_len)


def _nt_dot(a, b):
    return lax.dot_general(a, b, (((1,), (1,)), ((), ())), preferred_element_type=F32)


def _ctx_attn_kernel(q_ref, k_ref, v_ref, o_ref, ko_ref, vo_ref):
    scale = HEAD_DIM ** -0.5
    ko_ref[...] = k_ref[...]
    vo_ref[...] = v_ref[...]
    for h in range(N_HEADS):
        sl = slice(h * HEAD_DIM, (h + 1) * HEAD_DIM)
        s = _nt_dot(q_ref[:, sl].astype(BF16), k_ref[:, sl].astype(BF16)) * scale
        e = jnp.exp(s - jnp.max(s, axis=-1, keepdims=True))
        pr = e / jnp.sum(e, axis=-1, keepdims=True)
        o_ref[:, sl] = _dot(pr.astype(BF16), v_ref[:, sl].astype(BF16))


def _ctx_attn(p):
    qb = 3 * HY_DIM // NA_DIM
    spec = lambda c: pl.BlockSpec((L_PROMPT, NA_DIM), lambda s: (s, c))
    out = jax.ShapeDtypeStruct((T_PROMPT, NA_DIM), F32)
    return pl.pallas_call(
        _ctx_attn_kernel,
        grid=(N_PROMPT_SEQ,),
        in_specs=[spec(qb), spec(qb + 1), spec(qb + 2)],
        out_specs=[spec(0), spec(0), spec(0)],
        out_shape=[out, out, out],
        compiler_params=_params(("parallel",), 40),
        name="ctx_attn",
    )(p, p, p)


NA_Q_ROWS = 8
NA_K_ROWS = NA_Q_ROWS + WIN_R
NA_ROW_BLOCKS = GRID_ROWS // NA_Q_ROWS
NA_HEADS_PER_STEP = LANES // HEAD_DIM


def _na_key_row0(rb, xp=jnp):
    return xp.clip(rb * NA_Q_ROWS - WIN_R // 2, 0, GRID_ROWS - NA_K_ROWS)


def _na_kernel(q_ref, k_ref, v_ref, kc_ref, vc_ref, b_ref, o_ref):
    scale = HEAD_DIM ** -0.5
    k0 = pl.multiple_of(_na_key_row0(pl.program_id(1)) * GRID_W, GRID_W)
    n_loc = NA_K_ROWS * GRID_W
    for h in range(NA_HEADS_PER_STEP):
        sl = slice(h * HEAD_DIM, (h + 1) * HEAD_DIM)
        q = q_ref[:, sl].astype(BF16)
        s_loc = _nt_dot(q, k_ref[pl.ds(k0, n_loc), sl].astype(BF16)) * scale + b_ref[0, 0, h]
        s_ctx = _nt_dot(q, kc_ref[0, :, sl].astype(BF16)) * scale
        m = jnp.maximum(jnp.max(s_loc, axis=-1, keepdims=True), jnp.max(s_ctx, axis=-1, keepdims=True))
        e_loc = jnp.exp(s_loc - m)
        e_ctx = jnp.exp(s_ctx - m)
        den = jnp.sum(e_loc, axis=-1, keepdims=True) + jnp.sum(e_ctx, axis=-1, keepdims=True)
        o_ref[:, sl] = (_dot((e_loc / den).astype(BF16), v_ref[pl.ds(k0, n_loc), sl].astype(BF16))
                        + _dot((e_ctx / den).astype(BF16), vc_ref[0, :, sl].astype(BF16)))


NA_BIAS_KINDS = 3


def _na_bias_kind(rb):
    return jnp.where(rb == 0, 0, jnp.where(rb == NA_ROW_BLOCKS - 1, 2, 1))


def _na_bias_table(rpb):
    q = np.arange(GRID_W)[:, None]
    kc = np.arange(GRID_W)[None, :]
    cs = np.clip(q - WIN_C // 2, 0, GRID_W - WIN_C)
    col_valid = (kc >= cs) & (kc < cs + WIN_C)
    col_off = np.clip(kc - q, -(WIN_C - 1), WIN_C - 1) + WIN_C - 1
    b = rpb.astype(F32)[:, :, col_off]
    b = jnp.where(jnp.asarray(col_valid)[None, None], b, NEG_BIG)
    neg = jnp.full_like(b, NEG_BIG)
    b_next = jnp.concatenate([b[:, 1:], neg[:, :1]], axis=1)
    pairs = jnp.stack([jnp.concatenate([b, b_next], axis=-1), jnp.concatenate([b, neg], axis=-1),
                       jnp.concatenate([neg, b], axis=-1)], axis=1)

    def geometry(rb):
        q_row = rb * NA_Q_ROWS + np.arange(NA_Q_ROWS)[:, None]
        k_row = _na_key_row0(rb, np) + np.arange(NA_K_ROWS)[None, :]
        row_start = np.clip(q_row - WIN_R // 2, 0, GRID_ROWS - WIN_R)
        row_valid = (k_row >= row_start) & (k_row < row_start + WIN_R)
        return row_valid, k_row - q_row + WIN_R - 1

    for rb in range(2, NA_ROW_BLOCKS - 1):
        assert all(np.array_equal(a, c) for a, c in zip(geometry(rb), geometry(1)))
    geoms = [geometry(rb) for rb in (0, 1, NA_ROW_BLOCKS - 1)]
    kern = functools.partial(_na_bias_kernel, geoms=geoms)
    return pl.pallas_call(
        kern,
        grid=(N_HEADS, NA_BIAS_KINDS),
        in_specs=[pl.BlockSpec((1,) + pairs.shape[1:], lambda h, kind: (h, 0, 0, 0, 0))],
        out_specs=pl.BlockSpec((1, 1, 1, NA_Q_ROWS * GRID_W, NA_K_ROWS * GRID_W),
                               lambda h, kind: (kind, h // NA_HEADS_PER_STEP, h % NA_HEADS_PER_STEP, 0, 0)),
        out_shape=jax.ShapeDtypeStruct((NA_BIAS_KINDS, N_HEADS // NA_HEADS_PER_STEP, NA_HEADS_PER_STEP,
                                        NA_Q_ROWS * GRID_W, NA_K_ROWS * GRID_W), F32),
        compiler_params=_params(("parallel", "arbitrary"), 32),
        name="na_bias",
    )(pairs)


def _na_bias_kernel(p_ref, o_ref, *, geoms):
    for kind, (row_valid, d) in enumerate(geoms):
        @pl.when(pl.program_id(1) == kind)
        def _(row_valid=row_valid, d=d):
            for iq in range(NA_Q_ROWS):
                rows = slice(iq * GRID_W, (iq + 1) * GRID_W)
                for ik in range(0, NA_K_ROWS, 2):
                    cols = slice(ik * GRID_W, (ik + 2) * GRID_W)
                    left, right = row_valid[iq, ik], row_valid[iq, ik + 1]
                    if left and right:
                        o_ref[0, 0, 0, rows, cols] = p_ref[0, 0, int(d[iq, ik])]
                    elif left:
                        o_ref[0, 0, 0, rows, cols] = p_ref[0, 1, int(d[iq, ik])]
                    elif right:
                        o_ref[0, 0, 0, rows, cols] = p_ref[0, 2, int(d[iq, ik + 1])]
                    else:
                        o_ref[0, 0, 0, rows, cols] = jnp.full((GRID_W, 2 * GRID_W), NEG_BIG, F32)


def _na_attn(p, cache_k, cache_v, rpb):
    tq = NA_Q_ROWS * GRID_W
    q_col0 = 3 * HY_DIM // LANES
    k_col0 = q_col0 + NA_DIM // LANES
    v_col0 = k_col0 + NA_DIM // LANES
    q_blk0 = T_PROMPT // tq
    seq0 = T_PROMPT // L_LATENT
    bias = _na_bias_table(rpb)
    ctx_spec = pl.BlockSpec((1, cache_k.shape[1], LANES), lambda hp, rb, b: (b, 0, hp))
    return pl.pallas_call(
        _na_kernel,
        grid=(N_HEADS // NA_HEADS_PER_STEP, NA_ROW_BLOCKS, N_LATENT_SEQ),
        in_specs=[pl.BlockSpec((tq, LANES), lambda hp, rb, b: (q_blk0 + b * NA_ROW_BLOCKS + rb, q_col0 + hp)),
                  pl.BlockSpec((L_LATENT, LANES), lambda hp, rb, b: (seq0 + b, k_col0 + hp)),
                  pl.BlockSpec((L_LATENT, LANES), lambda hp, rb, b: (seq0 + b, v_col0 + hp)),
                  ctx_spec, ctx_spec,
                  pl.BlockSpec((1, 1, NA_HEADS_PER_STEP, tq, NA_K_ROWS * GRID_W),
                               lambda hp, rb, b: (_na_bias_kind(rb), hp, 0, 0, 0))],
        out_specs=pl.BlockSpec((tq, LANES), lambda hp, rb, b: (b * NA_ROW_BLOCKS + rb, hp)),
        out_shape=jax.ShapeDtypeStruct((T_LATENT, NA_DIM), F32),
        compiler_params=_params(("parallel", "parallel", "arbitrary"), 48),
        name="na_attn",
    )(p, p, p, cache_k, cache_v, bias)


def _pool_kernel(prev_ref, cur_ref, next_ref, gw_ref, sc_ref, o_ref, ext_ref, *, tm):
    i = pl.program_id(0)
    row0 = i * tm
    seq_len = jnp.where(row0 < T_PROMPT, L_PROMPT, L_LATENT)
    pos0 = jnp.where(row0 < T_PROMPT, row0 % L_PROMPT, (row0 - T_PROMPT) % L_LATENT)
    first = pos0 == 0
    last = pos0 + tm == seq_len
    h = POOL_HALO
    ext_ref[0:h] = jnp.where(first, 0.0, prev_ref[...])
    ext_ref[h:h + tm] = cur_ref[...]
    ext_ref[h + tm:] = jnp.where(last, 0.0, next_ref[...])
    pos = pos0 + lax.broadcasted_iota(jnp.int32, (tm, 1), 0)
    for g, w in enumerate(POOL_WINDOWS):
        cols = slice(g * POOL_GROUP, (g + 1) * POOL_GROUP)
        acc = ext_ref[h - w // 2:h - w // 2 + tm, cols]
        for j in range(1, w):
            acc = acc + ext_ref[h - w // 2 + j:h - w // 2 + j + tm, cols]
        cnt = jnp.minimum(pos + w // 2, seq_len) - jnp.maximum(pos - w // 2, 0)
        d = acc / cnt.astype(F32) - cur_ref[:, cols]
        o_ref[:, cols] = _dot(d.astype(BF16), gw_ref[g]) * sc_ref[:, cols]


def _pool(u, group_w_bf16, scale):
    tm = 256
    hb = tm // POOL_HALO
    n_halo = T_ALL // POOL_HALO
    kern = functools.partial(_pool_kernel, tm=tm)
    return pl.pallas_call(
        kern,
        grid=(T_ALL // tm,),
        in_specs=[pl.BlockSpec((POOL_HALO, D), lambda i: (jnp.maximum(i * hb - 1, 0), 0)),
                  pl.BlockSpec((tm, D), lambda i: (i, 0)),
                  pl.BlockSpec((POOL_HALO, D), lambda i: (jnp.minimum((i + 1) * hb, n_halo - 1), 0)),
                  pl.BlockSpec((len(POOL_WINDOWS), POOL_GROUP, POOL_GROUP), lambda i: (0, 0, 0)),
                  pl.BlockSpec((1, D), lambda i: (0, 0))],
        out_specs=pl.BlockSpec((tm, D), lambda i: (i, 0)),
        out_shape=jax.ShapeDtypeStruct((T_ALL, D), F32),
        scratch_shapes=[pltpu.VMEM((tm + 2 * POOL_HALO, D), F32)],
        compiler_params=_params(("parallel",), 40),
        name="pool",
    )(u, u, u, group_w_bf16, scale.reshape(1, D))


def _post_mixer_kernel(*refs, n1, n2, n_x, tm):
    a1_refs, a2_refs, x_refs = refs[:n1], refs[n1:n1 + n2], refs[n1 + n2:n1 + n2 + n_x]
    (w_ref, gpost_ref, gate_ref, gpre_ref, sh_ref, sc_ref, rwh_ref, rwl_ref, rb_ref,
     xo_ref, h_ref, lg_ref) = refs[n1 + n2 + n_x:]
    a1 = _token_rows(a1_refs, tm).astype(BF16)
    a2 = _token_rows(a2_refs, tm).astype(BF16)
    k1 = a1.shape[1]
    m = _dot(a1, w_ref[0:k1]) + _dot(a2, w_ref[k1:])
    xn = _token_rows(x_refs, tm) + gate_ref[0] * (_rms(m) * gpost_ref[...])
    xo_ref[...] = xn
    h = _rms(xn) * gpre_ref[...] * (1.0 + sc_ref[0]) + sh_ref[0]
    hh, hl = _split_bf16(h)
    h_ref[...] = _pack_bf16_pairs(hh)
    lg_ref[...] = _dot(hh, rwh_ref[...]) + _dot(hl, rwh_ref[...]) + _dot(hh, rwl_ref[...]) + rb_ref[...]


def _post_mixer(a1_parts, a1_col, a2_parts, a2_col, w_bf16, x_parts, g_post, gate, g_pre, shift, scale,
                rw_hi, rw_lo, rb, name):
    tm = 256
    kh = w_bf16.shape[0] // 2
    row = lambda i: (i, 0)
    const = lambda i: (0, 0)
    mod_spec = pl.BlockSpec((1, 1, D), lambda i: (_mod_row(i * tm), 0, 0))
    vec = pl.BlockSpec((1, D), const)
    kern = functools.partial(_post_mixer_kernel, n1=len(a1_parts), n2=len(a2_parts), n_x=len(x_parts), tm=tm)
    return pl.pallas_call(
        kern,
        grid=(T_ALL // tm,),
        in_specs=_token_specs(a1_parts, tm, kh, a1_col) + _token_specs(a2_parts, tm, kh, a2_col)
                 + _token_specs(x_parts, tm, D, 0) + [
                  pl.BlockSpec((2 * kh, D), const),
                  vec, mod_spec, vec, mod_spec, mod_spec,
                  pl.BlockSpec((D, LANES), const), pl.BlockSpec((D, LANES), const),
                  pl.BlockSpec((1, LANES), const)],
        out_specs=[pl.BlockSpec((tm, D), row), pl.BlockSpec((tm, D // 2), row), pl.BlockSpec((tm, LANES), row)],
        out_shape=[jax.ShapeDtypeStruct((T_ALL, D), F32), jax.ShapeDtypeStruct((T_ALL, D // 2), jnp.uint32),
                   jax.ShapeDtypeStruct((T_ALL, LANES), F32)],
        compiler_params=_params(("parallel",), 48),
        name=name,
    )(*a1_parts, *a2_parts, *x_parts, w_bf16, g_post, gate, g_pre, shift, scale, rw_hi, rw_lo, rb)


def _route_kernel(lg_ref, tri_ref, eidx_ref, gate_ref, rank_ref, cnt_ref, carry_ref):
    @pl.when(pl.program_id(0) == 0)
    def _():
        carry_ref[...] = jnp.zeros_like(carry_ref)

    shape = lg_ref.shape
    lane = lax.broadcasted_iota(jnp.int32, shape, 1).astype(F32)
    lg = jnp.where(lane < N_EXPERTS, lg_ref[...], -jnp.inf)
    multi = jnp.zeros(shape, F32)
    vals, sels = [], []
    eidx = jnp.zeros(shape, F32)
    for k in range(TOP_K):
        m = jnp.max(lg, axis=-1, keepdims=True)
        idx = jnp.min(jnp.where(lg == m, lane, float(LANES)), axis=-1, keepdims=True)
        sel = lane == idx
        multi = jnp.where(sel, 1.0, multi)
        lg = jnp.where(sel, -jnp.inf, lg)
        eidx = jnp.where(lane == k, idx, eidx)
        vals.append(m)
        sels.append(sel)
    exps = [jnp.exp(v - vals[0]) for v in vals]
    den = exps[0] + exps[1] + exps[2] + exps[3]
    rank_all = _dot(tri_ref[...], multi.astype(BF16)) + carry_ref[0:1]
    gate = jnp.zeros(shape, F32)
    rank = jnp.zeros(shape, F32)
    for k in range(TOP_K):
        gate = jnp.where(lane == k, exps[k] / den, gate)
        rank = jnp.where(lane == k, jnp.sum(jnp.where(sels[k], rank_all, 0.0), axis=-1, keepdims=True), rank)
    eidx_ref[...] = eidx.astype(jnp.int32)
    gate_ref[...] = gate
    rank_ref[...] = rank.astype(jnp.int32)
    carry_ref[0:1] = carry_ref[0:1] + jnp.sum(multi, axis=0, keepdims=True)
    cnt_ref[...] = carry_ref[...]


def _route(logits):
    tm = 256
    tri = (np.arange(tm)[:, None] > np.arange(tm)[None, :]).astype(np.float32)
    row = lambda i: (i, 0)
    spec = pl.BlockSpec((tm, LANES), row)
    return pl.pallas_call(
        _route_kernel,
        grid=(T_ALL // tm,),
        in_specs=[spec, pl.BlockSpec((tm, tm), lambda i: (0, 0))],
        out_specs=[spec, spec, spec, pl.BlockSpec((8, LANES), lambda i: (0, 0))],
        out_shape=[jax.ShapeDtypeStruct((T_ALL, LANES), jnp.int32), jax.ShapeDtypeStruct((T_ALL, LANES), F32),
                   jax.ShapeDtypeStruct((T_ALL, LANES), jnp.int32), jax.ShapeDtypeStruct((8, LANES), F32)],
        scratch_shapes=[pltpu.VMEM((8, LANES), F32)],
        compiler_params=_params(("arbitrary",), 32),
        name="route",
    )(logits, jnp.asarray(tri, BF16))


def _dest_kernel(eidx_ref, rank_ref, start_ref, o_ref):
    shape = eidx_ref.shape
    lane = lax.broadcasted_iota(jnp.int32, shape, 1).astype(F32)
    eidx = eidx_ref[...].astype(F32)
    starts = start_ref[...].astype(F32)
    dest = jnp.zeros(shape, F32)
    for k in range(TOP_K):
        e_k = jnp.sum(jnp.where(lane == k, eidx, 0.0), axis=-1, keepdims=True)
        start = jnp.sum(jnp.where(lane == e_k, starts, 0.0), axis=-1, keepdims=True)
        dest = jnp.where(lane == k, start, dest)
    o_ref[...] = jnp.where(lane < TOP_K, dest.astype(jnp.int32) + rank_ref[...], 0)


def _dest(eidx, rank, group_start):
    tm = 1024
    spec = pl.BlockSpec((tm, LANES), lambda i: (i, 0))
    return pl.pallas_call(
        _dest_kernel,
        grid=(T_ALL // tm,),
        in_specs=[spec, spec, pl.BlockSpec((1, LANES), lambda i: (0, 0))],
        out_specs=spec,
        out_shape=jax.ShapeDtypeStruct((T_ALL, LANES), jnp.int32),
        compiler_params=_params(("parallel",), 32),
        name="dest",
    )(eidx, rank, group_start)


def _dispatch_kernel(dest_ref, lo_ref, hi_ref, nu_ref, h_ref, o_ref, zero_ref, sem, zsem, *, tm):
    i = pl.program_id(0)
    base = i * tm * TOP_K

    @pl.when(i == 0)
    def _():
        zero_ref[...] = jnp.zeros_like(zero_ref)

    def issue(g, carry):
        t0 = pl.multiple_of(g * SUBLANES, SUBLANES)
        for s in range(SUBLANES):
            for k in range(TOP_K):
                d = dest_ref[base + (t0 + s) * TOP_K + k]
                pltpu.make_async_copy(h_ref.at[pl.ds(t0 + s, 1)], o_ref.at[pl.ds(d, 1)], sem).start(priority=k % 2)
        return carry

    def zero_copies(do):
        @pl.when(i < N_EXPERTS)
        def _():
            lo = lo_ref[i]
            n_pad = hi_ref[i] - lo
            head = jnp.minimum((SUBLANES - lo % SUBLANES) % SUBLANES, n_pad)

            def zero_row(r, carry):
                do(pltpu.make_async_copy(zero_ref.at[pl.ds(0, 1)], o_ref.at[pl.ds(lo + r, 1)], zsem))
                return carry

            lax.fori_loop(0, head, zero_row, 0)
            pos = lo + head
            rest = n_pad - head
            size = MOE_TM // 2
            while size >= SUBLANES:
                @pl.when((rest & size) != 0)
                def _(pos=pos, size=size):
                    do(pltpu.make_async_copy(zero_ref.at[pl.ds(0, size)],
                                             o_ref.at[pl.ds(pl.multiple_of(pos, SUBLANES), size)], zsem))

                pos = pos + (rest & size)
                size //= 2

        blk = nu_ref[0] + i - N_EXPERTS

        @pl.when((i >= N_EXPERTS) & (blk < MOE_BLOCKS))
        def _():
            r0 = pl.multiple_of(blk * MOE_TM, MOE_TM)
            do(pltpu.make_async_copy(zero_ref, o_ref.at[pl.ds(r0, MOE_TM)], zsem))

    zero_copies(lambda cp: cp.start())
    lax.fori_loop(0, tm // SUBLANES, issue, 0)
    zero_copies(lambda cp: cp.wait())
    for k in range(TOP_K):
        pltpu.make_async_copy(h_ref, o_ref.at[pl.ds(0, tm)], sem).wait()


def _dispatch(dest_flat, pad_lo, pad_hi, n_used, h):
    tm = 256
    assert T_ALL // tm >= 2 * N_EXPERTS
    kern = functools.partial(_dispatch_kernel, tm=tm)
    return pl.pallas_call(
        kern,
        grid_spec=pltpu.PrefetchScalarGridSpec(
            num_scalar_prefetch=4,
            grid=(T_ALL // tm,),
            in_specs=[pl.BlockSpec((tm, D // 2), lambda i, *_: (i, 0))],
            out_specs=pl.BlockSpec(memory_space=pl.ANY),
            scratch_shapes=[pltpu.VMEM((MOE_TM, D // 2), jnp.uint32), pltpu.SemaphoreType.DMA,
                            pltpu.SemaphoreType.DMA]),
        out_shape=jax.ShapeDtypeStruct((MOE_ROWS, D // 2), jnp.uint32),
        compiler_params=_params(("arbitrary",), 32),
        name="dispatch",
    )(dest_flat, pad_lo, pad_hi, n_used, h)


def _new_weights(be_ref, i):
    return (i == 0) | (be_ref[i] != be_ref[jnp.maximum(i - 1, 0)])


def _for_valid_rows(n_valid, compute):
    many = n_valid > MOE_TM // 2
    pl.when(many)(lambda: compute(MOE_TM))
    pl.when(jnp.logical_not(many))(lambda: compute(MOE_TM // 2))


def _expert_up_kernel(be_ref, nu_ref, bv_ref, x_ref, w_ref, b_ref, o_ref, wbf_ref):
    i = pl.program_id(1)
    tn = o_ref.shape[1]

    def compute(rows):
        x = jnp.concatenate(_unpack_bf16_pairs(x_ref[0:rows]), axis=1)
        even = lax.broadcasted_iota(jnp.int32, (rows, LANES), 1) % 2 == 0
        for c in range(tn // LANES):
            cols = slice(2 * LANES * c, 2 * LANES * (c + 1))
            hb = _dot(x, wbf_ref[:, cols]) + b_ref[0, 0, :, cols]
            first, second = hb[:, :LANES], hb[:, LANES:]
            g = jnp.where(even, first, pltpu.roll(second, 1, 1))
            lin = jnp.where(even, pltpu.roll(first, LANES - 1, 1), second)
            g = jnp.minimum(g, SWIGLU_LIMIT)
            lin = jnp.clip(lin, -SWIGLU_LIMIT, SWIGLU_LIMIT)
            act = g * jax.nn.sigmoid(SWIGLU_ALPHA * g) * (lin + 1.0)
            o_ref[0:rows, LANES * c:LANES * (c + 1)] = act.astype(BF16)
        if rows < MOE_TM:
            o_ref[rows:] = jnp.zeros((MOE_TM - rows, tn), BF16)

    @pl.when(i < nu_ref[0])
    def _():
        @pl.when(_new_weights(be_ref, i))
        def _():
            wbf_ref[...] = w_ref[0, 0].astype(BF16)

        _for_valid_rows(bv_ref[i], compute)

    @pl.when(i >= nu_ref[0])
    def _():
        o_ref[...] = jnp.zeros_like(o_ref)


def _expert_up(blk_e, n_used, blk_valid, xs, w_gu, b_gu, layer):
    tn = FF_TILE
    blk = lambda j, i, be, nu, bv: (jnp.minimum(i, nu[0] - 1), 0)
    return pl.pallas_call(
        _expert_up_kernel,
        grid_spec=pltpu.PrefetchScalarGridSpec(
            num_scalar_prefetch=3,
            grid=(D_FF // tn, MOE_BLOCKS),
            in_specs=[pl.BlockSpec((MOE_TM, D // 2), blk),
                      pl.BlockSpec((1, 1, D, 2 * tn), lambda j, i, be, nu, bv: (layer, be[i], 0, j)),
                      pl.BlockSpec((1, 1, 1, 2 * tn), lambda j, i, be, nu, bv: (layer, be[i], 0, j))],
            out_specs=pl.BlockSpec((MOE_TM, tn), lambda j, i, be, nu, bv: (i, j)),
            scratch_shapes=[pltpu.VMEM((D, 2 * tn), BF16)]),
        out_shape=jax.ShapeDtypeStruct((MOE_ROWS, D_FF), BF16),
        compiler_params=_params(("arbitrary", "arbitrary"), 56),
        name="expert_up",
    )(blk_e, n_used, blk_valid, xs, w_gu, b_gu.reshape(N_LAYERS, N_EXPERTS, 1, 2 * D_FF))


DOWN_TILE = D


def _expert_down_kernel(be_ref, nu_ref, bv_ref, a_ref, w_ref, b_ref, o_ref, wbf_ref, wtmp_ref):
    i = pl.program_id(1)

    def compute(rows):
        y = _dot(a_ref[0:rows], wbf_ref[...]) + b_ref[0, 0]
        o_ref[0:rows] = _pack_bf16_pairs(y.astype(BF16))
        if rows < MOE_TM:
            o_ref[rows:] = jnp.zeros((MOE_TM - rows, o_ref.shape[1]), jnp.uint32)

    @pl.when(i < nu_ref[0])
    def _():
        @pl.when(_new_weights(be_ref, i))
        def _():
            half = LANES // 2
            for g in range(D_FF // LANES):
                r0 = LANES * g
                for c in range(wbf_ref.shape[1] // LANES):
                    cols = slice(LANES * c, LANES * (c + 1))
                    wtmp_ref[c, pl.ds(0, half, stride=2), :] = w_ref[0, 0, r0:r0 + half, cols]
                    wtmp_ref[c, pl.ds(1, half, stride=2), :] = w_ref[0, 0, r0 + half:r0 + LANES, cols]
                    wbf_ref[r0:r0 + LANES, cols] = wtmp_ref[c].astype(BF16)

        _for_valid_rows(bv_ref[i], compute)

    @pl.when(i >= nu_ref[0])
    def _():
        o_ref[...] = jnp.zeros_like(o_ref)


def _expert_down(blk_e, n_used, blk_valid, a, w_down, b_down, layer):
    tn = DOWN_TILE
    blk = lambda j, i, be, nu, bv: (jnp.minimum(i, nu[0] - 1), 0)
    return pl.pallas_call(
        _expert_down_kernel,
        grid_spec=pltpu.PrefetchScalarGridSpec(
            num_scalar_prefetch=3,
            grid=(D // tn, MOE_BLOCKS),
            in_specs=[pl.BlockSpec((MOE_TM, D_FF), blk),
                      pl.BlockSpec((1, 1, D_FF, tn), lambda j, i, be, nu, bv: (layer, be[i], 0, j)),
                      pl.BlockSpec((1, 1, 1, tn), lambda j, i, be, nu, bv: (layer, be[i], 0, j))],
            out_specs=pl.BlockSpec((MOE_TM, tn // 2), lambda j, i, be, nu, bv: (i, j)),
            scratch_shapes=[pltpu.VMEM((D_FF, tn), BF16), pltpu.VMEM((tn // LANES, LANES, LANES), F32)]),
        out_shape=jax.ShapeDtypeStruct((MOE_ROWS, D // 2), jnp.uint32),
        compiler_params=_params(("arbitrary", "arbitrary"), 60),
        name="expert_down",
    )(blk_e, n_used, blk_valid, a, w_down, b_down.reshape(N_LAYERS, N_EXPERTS, 1, D))


def _combine_kernel(dest_ref, y_ref, gate_ref, x_ref, gpost_ref, g2_ref, *rest, tm):
    (*o_refs, buf_ref, sem) = rest
    i = pl.program_id(0)

    def gather(step, slot):
        base = step * tm * TOP_K

        def issue(g, carry):
            t0 = pl.multiple_of(g * SUBLANES, SUBLANES)
            for s in range(SUBLANES):
                for k in range(TOP_K):
                    d = dest_ref[base + (t0 + s) * TOP_K + k]
                    pltpu.make_async_copy(y_ref.at[pl.ds(d, 1)], buf_ref.at[slot, k, pl.ds(t0 + s, 1)],
                                          sem.at[slot]).start(priority=k % 2)
            return carry

        lax.fori_loop(0, tm // SUBLANES, issue, 0)

    @pl.when(i == 0)
    def _():
        gather(0, 0)

    @pl.when(i + 1 < pl.num_programs(0))
    def _():
        gather(i + 1, (i + 1) % 2)

    slot = i % 2
    for k in range(TOP_K):
        pltpu.make_async_copy(y_ref.at[pl.ds(0, tm)], buf_ref.at[slot, k], sem.at[slot]).wait()

    def expert_rows(k):
        hi, lo = _unpack_bf16_pairs(buf_ref[slot, k])
        tw = DOWN_TILE // 2
        parts = []
        for j in range(D // DOWN_TILE):
            parts += [hi[:, j * tw:(j + 1) * tw], lo[:, j * tw:(j + 1) * tw]]
        return jnp.concatenate(parts, axis=1).astype(F32)

    f = gate_ref[:, 0:1] * expert_rows(0)
    for k in range(1, TOP_K):
        f = f + gate_ref[:, k:k + 1] * expert_rows(k)
    out = x_ref[...] + g2_ref[0] * (_rms(f) * gpost_ref[...])
    if len(o_refs) == 1:
        o_refs[0][...] = out
    else:
        in_prompt = i * tm < T_PROMPT
        @pl.when(in_prompt)
        def _():
            o_refs[0][...] = out

        @pl.when(jnp.logical_not(in_prompt))
        def _():
            o_refs[1][...] = out


def _combine(dest_flat, ys, gate, x, g_post, gate2, name, split_out=False):
    tm = 256
    kern = functools.partial(_combine_kernel, tm=tm)
    n_prompt_tiles = T_PROMPT // tm
    if split_out:
        out_specs = [pl.BlockSpec((tm, D), lambda i, d: (jnp.minimum(i, n_prompt_tiles - 1), 0)),
                     pl.BlockSpec((tm, D), lambda i, d: (jnp.maximum(i - n_prompt_tiles, 0), 0))]
        out_shape = [jax.ShapeDtypeStruct((T_PROMPT, D), F32), jax.ShapeDtypeStruct((T_LATENT, D), F32)]
    else:
        out_specs = pl.BlockSpec((tm, D), lambda i, d: (i, 0))
        out_shape = jax.ShapeDtypeStruct((T_ALL, D), F32)
    return pl.pallas_call(
        kern,
        grid_spec=pltpu.PrefetchScalarGridSpec(
            num_scalar_prefetch=1,
            grid=(T_ALL // tm,),
            in_specs=[pl.BlockSpec(memory_space=pl.ANY),
                      pl.BlockSpec((tm, LANES), lambda i, d: (i, 0)),
                      pl.BlockSpec((tm, D), lambda i, d: (i, 0)),
                      pl.BlockSpec((1, D), lambda i, d: (0, 0)),
                      pl.BlockSpec((1, 1, D), lambda i, d: (_mod_row(i * tm), 0, 0))],
            out_specs=out_specs,
            scratch_shapes=[pltpu.VMEM((2, TOP_K, tm, D // 2), jnp.uint32), pltpu.SemaphoreType.DMA((2,))]),
        out_shape=out_shape,
        compiler_params=_params(("arbitrary",), 32),
        name=name,
    )(dest_flat, ys, gate, x, g_post, gate2)


def _moe(h, logits, x, g_post, gate2, w_gu, b_gu, w_down, b_down, layer):
    eidx, gate, rank, counts = _route(logits)
    cnt = counts[0, :N_EXPERTS].astype(jnp.int32)
    padded = (cnt + MOE_TM - 1) // MOE_TM * MOE_TM
    group_end = jnp.cumsum(padded)
    group_start = jnp.zeros((1, LANES), jnp.int32).at[0, :N_EXPERTS].set(group_end - padded)
    n_used = group_end[-1] // MOE_TM
    blk_row0 = jnp.arange(MOE_BLOCKS, dtype=jnp.int32) * MOE_TM
    blk_row0 = jnp.minimum(blk_row0, (n_used - 1) * MOE_TM)
    blk_e = jnp.sum(group_end[None, :] <= blk_row0[:, None], axis=1).astype(jnp.int32)
    n_used = n_used.astype(jnp.int32).reshape(1)
    filled_end = group_end - padded + cnt
    blk_valid = jnp.clip(filled_end[jnp.minimum(blk_e, N_EXPERTS - 1)] - blk_row0, 0, MOE_TM).astype(jnp.int32)
    dest = _dest(eidx, rank, group_start)[:, :TOP_K].reshape(N_ASSIGN)
    xs = _dispatch(dest, filled_end, group_end, n_used, h)
    a = _expert_up(blk_e, n_used, blk_valid, xs, w_gu, b_gu, layer)
    ys = _expert_down(blk_e, n_used, blk_valid, a, w_down, b_down, layer)
    return _combine(dest, ys, gate, x, g_post, gate2, f"combine_{layer}", split_out=layer == N_LAYERS - 1)


def kernel(x_prompt, x_sample, cache_k, cache_v, c, c_ctx, ada_w, ada_b, norm_mix_pre, norm_mix_post,
           norm_ffn_pre, norm_ffn_post, ab_in_w, ab_out_w, hy_short_w, hy_short_b, hy_w1, hy_b1, hy_w2,
           hy_b2, hy_w3, hy_freq, hy_bias, na_rpb, pool_in_w, pool_group_w, pool_scale, pool_out_w,
           router_w, router_b, exp_w_gu, exp_b_gu, exp_w_down, exp_b_down):
    x = [x_prompt.reshape(T_PROMPT, D), x_sample.reshape(T_LATENT, D)]
    cvec = jnp.concatenate([c_ctx[None], c, jnp.zeros((8 - 1 - N_LATENT_SEQ, D), F32)], axis=0)
    mods = _adaln(cvec, ada_w, ada_b).reshape(N_LAYERS, 8, 6, D)

    new_k = new_v = None
    for layer in range(N_LAYERS):
        sh1, sc1, g1, sh2, sc2, g2 = [mods[layer, :, k][:, None, :] for k in range(6)]
        row = lambda a: a[layer].reshape(1, D)
        if layer % 2 == 0:
            j = layer // 2
            p = _modmm(x, row(norm_mix_pre), sh1, sc1, ab_in_w[j].astype(BF16), "ab_in")
            filt = (hy_w1[j], hy_b1[j], hy_w2[j], hy_b2[j], hy_w3[j], hy_freq[j])
            hy = (hy_short_w[j], hy_short_b[j], hy_bias[j])
            yh_p = _hyena(p, *hy, _hyena_filter_time(L_PROMPT, *filt), N_PROMPT_SEQ, L_PROMPT, 0)
            yh_s = _hyena(p, *hy, _hyena_filter_time(L_LATENT, *filt), N_LATENT_SEQ, L_LATENT,
                          T_PROMPT // L_LATENT)
            ya_p, new_k, new_v = _ctx_attn(p)
            ck = cache_k[:, j].reshape(N_LATENT_SEQ, -1, NA_DIM)
            cv = cache_v[:, j].reshape(N_LATENT_SEQ, -1, NA_DIM)
            ya_s = _na_attn(p, ck, cv, na_rpb[j])
            a1 = [yh_p, yh_s]
            a2 = [ya_p, ya_s]
            a1_col = a2_col = 0
            w_out = ab_out_w[j].astype(BF16)
        else:
            j = layer // 2
            u = _modmm(x, row(norm_mix_pre), sh1, sc1, pool_in_w[j].astype(BF16), "pool_in")
            a1 = a2 = [_pool(u, pool_group_w[j].astype(BF16), pool_scale[j])]
            a1_col, a2_col = 0, 1
            w_out = pool_out_w[j].astype(BF16)
        rw = jnp.zeros((D, LANES), F32).at[:, :N_EXPERTS].set(router_w[layer])
        rw_hi, rw_lo = _split_bf16(rw)
        rb = jnp.zeros((1, LANES), F32).at[0, :N_EXPERTS].set(router_b[layer])
        xm, h, logits = _post_mixer(a1, a1_col, a2, a2_col, w_out, x, row(norm_mix_post), g1,
                                    row(norm_ffn_pre), sh2, sc2, rw_hi, rw_lo, rb, f"post_mixer_{layer}")
        x = _moe(h, logits, xm, row(norm_ffn_post), g2, exp_w_gu, exp_b_gu, exp_w_down, exp_b_down, layer)
        x = list(x) if isinstance(x, (list, tuple)) else [x]

    y_prompt = x[0].reshape(x_prompt.shape)
    y_sample = x[1].reshape(x_sample.shape)
    kv_shape = (N_PROMPT_SEQ, 1, L_PROMPT, N_HEADS, HEAD_DIM)
    return y_prompt, y_sample, new_k.reshape(kv_shape), new_v.reshape(kv_shape)
```
